```python
import math
import jax, jax.numpy as jnp
from jax import lax
import numpy as np

D_MODEL = 1024
BATCH = 8
SEQ = 4096
DEPTH = 1

MEM_LEN = 256
MLA_HEADS = 8
MLA_Q_RANK = 384
MLA_KV_RANK = 128
MLA_NOPE = 64
MLA_ROPE = 32
MLA_V = 64
Q_BLOCK = 128
RET_HEADS = 4
RET_DK = 64
RET_DV = 128
RET_CHUNK = 128
X_HEADS = 4
X_DH = 128
N_BRANCH = 3
PEER_HEADS = 8
PEER_N_KEYS = 128
PEER_N_EXPERTS = PEER_N_KEYS * PEER_N_KEYS
PEER_TOPK = 16
PEER_DQ = 256
PEER_DHALF = PEER_DQ // 2
PEER_TOKEN_BLOCK = 128

ROPE_BASE = 10000.0
EPS = 1e-6
IN_SPLITS = (MLA_Q_RANK, MLA_KV_RANK, MLA_ROPE,
             RET_HEADS * RET_DK, RET_HEADS * RET_DK, RET_HEADS * RET_DV, RET_HEADS * RET_DV,
             X_HEADS * X_DH, N_BRANCH * D_MODEL)
IN_COLS = sum(IN_SPLITS)

kernel_name = "hybrid_mla_retention_memxattn_peer"


def rmsnorm(x, g):
    xf = x.astype(jnp.float32)
    y = xf * lax.rsqrt(jnp.mean(xf * xf, axis=-1, keepdims=True) + EPS)
    return (y * g.astype(jnp.float32)).astype(x.dtype)


def rope(x, positions):
    d = x.shape[-1]
    inv_freq = ROPE_BASE ** (-jnp.arange(0, d, 2, dtype=jnp.float32) / d)
    ang = positions.astype(jnp.float32)[..., None] * inv_freq
    cos = jnp.cos(ang)[:, :, None, :].astype(x.dtype)
    sin = jnp.sin(ang)[:, :, None, :].astype(x.dtype)
    x1, x2 = x[..., : d // 2], x[..., d // 2:]
    return jnp.concatenate([x1 * cos - x2 * sin, x1 * sin + x2 * cos], axis=-1)


def causal_attention_blocked(q, k, v, scale):
    B, H, S, dqk = q.shape
    nb = S // Q_BLOCK
    qb = q.reshape(B, H, nb, Q_BLOCK, dqk).transpose(2, 0, 1, 3, 4)
    kpos = jnp.arange(S)

    def one_block(args):
        q_blk, bi = args
        qpos = bi * Q_BLOCK + jnp.arange(Q_BLOCK)
        s = jnp.einsum('bhqd,bhkd->bhqk', q_blk, k).astype(jnp.float32) * scale
        s = jnp.where(kpos[None, :] <= qpos[:, None], s, jnp.float32(-1e30))
        p = jax.nn.softmax(s, axis=-1).astype(v.dtype)
        return jnp.einsum('bhqk,bhkd->bhqd', p, v)

    o = lax.map(one_block, (qb, jnp.arange(nb)))
    return o.transpose(1, 2, 0, 3, 4).reshape(B, H, S, v.shape[-1])


def mla_branch(c_q, c_kv, k_r, positions, q_norm_g, w_uq, kv_norm_g, w_ukv, qn_g, kn_g, w_o):
    B, S, _ = c_q.shape
    q = (rmsnorm(c_q, q_norm_g) @ w_uq).reshape(B, S, MLA_HEADS, MLA_NOPE + MLA_ROPE)
    kv = (rmsnorm(c_kv, kv_norm_g) @ w_ukv).reshape(B, S, MLA_HEADS, MLA_NOPE + MLA_V)
    k_nope, v = kv[..., :MLA_NOPE], kv[..., MLA_NOPE:]
    k_rope = jnp.broadcast_to(k_r[:, :, None, :], (B, S, MLA_HEADS, MLA_ROPE))
    k = jnp.concatenate([k_nope, k_rope], axis=-1)
    q = rmsnorm(q, qn_g)
    k = rmsnorm(k, kn_g)
    q = jnp.concatenate([q[..., :MLA_NOPE], rope(q[..., MLA_NOPE:], positions)], axis=-1)
    k = jnp.concatenate([k[..., :MLA_NOPE], rope(k[..., MLA_NOPE:], positions)], axis=-1)
    scale = (MLA_NOPE + MLA_ROPE) ** -0.5
    o = causal_attention_blocked(q.transpose(0, 2, 1, 3), k.transpose(0, 2, 1, 3),
                                 v.transpose(0, 2, 1, 3), scale)
    o = o.transpose(0, 2, 1, 3).reshape(B, S, MLA_HEADS * MLA_V)
    return o @ w_o


def retention_chunkwise(q, k, v, log_gamma):
    B, H, S, dk = q.shape
    dv = v.shape[-1]
    L = RET_CHUNK
    C = S // L
    dt = q.dtype
    qc = q.reshape(B, H, C, L, dk)
    kc = k.reshape(B, H, C, L, dk)
    vc = v.reshape(B, H, C, L, dv)
    idx = jnp.arange(L, dtype=jnp.float32)
    diff = idx[:, None] - idx[None, :]
    lg = log_gamma[:, None, None]
    decay = jnp.where(diff >= 0, jnp.exp(lg * jnp.maximum(diff, 0.0)), 0.0).astype(dt)
    xi = jnp.exp(log_gamma[:, None] * (idx + 1.0)).astype(dt)
    zeta = jnp.exp(log_gamma[:, None] * (L - 1.0 - idx)).astype(dt)
    inner = jnp.einsum('bhcid,bhcjd->bhcij', qc, kc) * decay[None, :, None]
    inner_out = jnp.einsum('bhcij,bhcje->bhcie', inner, vc)
    incr = jnp.einsum('bhcjd,bhcje->cbhde', kc * zeta[None, :, None, :, None], vc)
    chunk_decay = jnp.exp(log_gamma * L).astype(incr.dtype)[None, :, None, None]

    def step(R, inc):
        return chunk_decay * R + inc, R

    _, R_prev = lax.scan(step, jnp.zeros((B, H, dk, dv), incr.dtype), incr)
    cross = jnp.einsum('bhcid,cbhde->bhcie', qc * xi[None, :, None, :, None], R_prev)
    return (inner_out + cross).reshape(B, H, S, dv)


def retention_branch(r_q, r_k, r_v, r_g, positions, gn_g, w_o):
    B, S, _ = r_q.shape
    q = rope(r_q.reshape(B, S, RET_HEADS, RET_DK), positions)
    k = rope(r_k.reshape(B, S, RET_HEADS, RET_DK), positions) * (RET_DK ** -0.5)
    v = r_v.reshape(B, S, RET_HEADS, RET_DV)
    log_gamma = jnp.log(1.0 - 2.0 ** (-5.0 - jnp.arange(RET_HEADS, dtype=jnp.float32)))
    o = retention_chunkwise(q.transpose(0, 2, 1, 3), k.transpose(0, 2, 1, 3),
                            v.transpose(0, 2, 1, 3), log_gamma)
    o = rmsnorm(o.transpose(0, 2, 1, 3), gn_g)
    o = o.reshape(B, S, RET_HEADS * RET_DV) * jax.nn.silu(r_g)
    return o @ w_o


def memory_branch(x_q, mem, mem_g, w_mem_kv, qn_g, kn_g, w_o):
    B, S, _ = x_q.shape
    M = mem.shape[1]
    q = rmsnorm(x_q.reshape(B, S, X_HEADS, X_DH), qn_g)
    kv = (rmsnorm(mem, mem_g) @ w_mem_kv).reshape(B, M, X_HEADS, 2 * X_DH)
    k = rmsnorm(kv[..., :X_DH], kn_g)
    v = kv[..., X_DH:]
    s = jnp.einsum('bshd,bmhd->bhsm', q, k).astype(jnp.float32) * (X_DH ** -0.5)
    p = jax.nn.softmax(s, axis=-1).astype(v.dtype)
    o = jnp.einsum('bhsm,bmhd->bshd', p, v).reshape(B, S, X_HEADS * X_DH)
    return o @ w_o


def peer_ffn(xn, w_q, sub_keys, u_tab, v_tab):
    B, S, D = xn.shape
    T = B * S
    xt = xn.reshape(T, D)
    q = (xt @ w_q).reshape(T, PEER_HEADS, 2, PEER_DHALF)
    s = jnp.einsum('thcd,hcnd->thcn', q, sub_keys)
    sv, si = lax.top_k(s, PEER_TOPK)
    cand = (sv[:, :, 0, :, None] + sv[:, :, 1, None, :]).reshape(T, PEER_HEADS, PEER_TOPK * PEER_TOPK)
    cand_idx = (si[:, :, 0, :, None] * PEER_N_KEYS + si[:, :, 1, None, :]).reshape(
        T, PEER_HEADS, PEER_TOPK * PEER_TOPK)
    best, pos = lax.top_k(cand, PEER_TOPK)
    experts = jnp.take_along_axis(cand_idx, pos, axis=-1)
    gate = jax.nn.softmax(best.astype(jnp.float32), axis=-1).astype(xn.dtype)
    nb = T // PEER_TOKEN_BLOCK

    def one_block(args):
        xb, eb, gb = args
        u = u_tab[eb]
        a = jax.nn.gelu(jnp.einsum('thkd,td->thk', u, xb))
        vv = v_tab[eb]
        return jnp.einsum('thk,thkd->td', gb * a, vv)

    out = lax.map(one_block, (xt.reshape(nb, PEER_TOKEN_BLOCK, D),
                              experts.reshape(nb, PEER_TOKEN_BLOCK, PEER_HEADS, PEER_TOPK),
                              gate.reshape(nb, PEER_TOKEN_BLOCK, PEER_HEADS, PEER_TOPK)))
    return out.reshape(B, S, D)


def setup_inputs(seed: int = 0) -> dict:
    key = jax.random.key(seed)
    ks = jax.random.split(key, 32)
    f32 = jnp.float32

    def nrm(k, shape, scale):
        return jax.random.normal(k, shape, f32) * scale

    def gain(k, shape):
        return 1.0 + 0.02 * jax.random.normal(k, shape, f32)

    start = jax.random.randint(ks[2], (BATCH, 1), 0, 1024, dtype=jnp.int32)
    positions = (start + jnp.arange(SEQ, dtype=jnp.int32)[None, :]).astype(jnp.int32)
    return {
        "x": nrm(ks[0], (BATCH, SEQ, D_MODEL), 1.0),
        "mem": nrm(ks[1], (BATCH, MEM_LEN, D_MODEL), 1.0),
        "positions": positions,
        "attn_norm_g": gain(ks[3], (DEPTH, D_MODEL)),
        "w_in": nrm(ks[4], (DEPTH, D_MODEL, IN_COLS), D_MODEL ** -0.5),
        "mla_q_norm_g": gain(ks[5], (DEPTH, MLA_Q_RANK)),
        "w_uq": nrm(ks[6], (DEPTH, MLA_Q_RANK, MLA_HEADS * (MLA_NOPE + MLA_ROPE)), MLA_Q_RANK ** -0.5),
        "mla_kv_norm_g": gain(ks[7], (DEPTH, MLA_KV_RANK)),
        "w_ukv": nrm(ks[8], (DEPTH, MLA_KV_RANK, MLA_HEADS * (MLA_NOPE + MLA_V)), MLA_KV_RANK ** -0.5),
        "mla_qn_g": gain(ks[9], (DEPTH, MLA_NOPE + MLA_ROPE)),
        "mla_kn_g": gain(ks[10], (DEPTH, MLA_NOPE + MLA_ROPE)),
        "w_o_mla": nrm(ks[11], (DEPTH, MLA_HEADS * MLA_V, D_MODEL), (MLA_HEADS * MLA_V) ** -0.5),
        "ret_gn_g": gain(ks[12], (DEPTH, RET_HEADS, RET_DV)),
        "w_o_ret": nrm(ks[13], (DEPTH, RET_HEADS * RET_DV, D_MODEL), (RET_HEADS * RET_DV) ** -0.5),
        "mem_norm_g": gain(ks[14], (DEPTH, D_MODEL)),
        "w_mem_kv": nrm(ks[15], (DEPTH, D_MODEL, X_HEADS * 2 * X_DH), D_MODEL ** -0.5),
        "x_qn_g": gain(ks[16], (DEPTH, X_DH)),
        "x_kn_g": gain(ks[17], (DEPTH, X_DH)),
        "w_o_cross": nrm(ks[18], (DEPTH, X_HEADS * X_DH, D_MODEL), (X_HEADS * X_DH) ** -0.5),
        "w_out": nrm(ks[19], (DEPTH, D_MODEL, D_MODEL), D_MODEL ** -0.5),
        "ffn_norm_g": gain(ks[20], (DEPTH, D_MODEL)),
        "peer_w_q": nrm(ks[21], (DEPTH, D_MODEL, PEER_HEADS * PEER_DQ), D_MODEL ** -0.5),
        "peer_keys": nrm(ks[22], (DEPTH, PEER_HEADS, 2, PEER_N_KEYS, PEER_DHALF), PEER_DHALF ** -0.5),
        "peer_u": nrm(ks[23], (DEPTH, PEER_N_EXPERTS, D_MODEL), D_MODEL ** -0.5),
        "peer_v": nrm(ks[24], (DEPTH, PEER_N_EXPERTS, D_MODEL), (PEER_HEADS * PEER_TOPK) ** -0.5),
    }


def reference(x, mem, positions, attn_norm_g, w_in, mla_q_norm_g, w_uq, mla_kv_norm_g, w_ukv,
              mla_qn_g, mla_kn_g, w_o_mla, ret_gn_g, w_o_ret, mem_norm_g, w_mem_kv, x_qn_g, x_kn_g,
              w_o_cross, w_out, ffn_norm_g, peer_w_q, peer_keys, peer_u, peer_v):
    B, S, _ = x.shape
    split_points = []
    acc = 0
    for sz in IN_SPLITS[:-1]:
        acc += sz
        split_points.append(acc)
    h = x
    for layer in range(DEPTH):
        n = rmsnorm(h, attn_norm_g[layer])
        proj = n @ w_in[layer]
        c_q, c_kv, k_r, r_q, r_k, r_v, r_g, x_q, gates = jnp.split(proj, split_points, axis=-1)
        y_mla = mla_branch(c_q, c_kv, k_r, positions, mla_q_norm_g[layer], w_uq[layer],
                           mla_kv_norm_g[layer], w_ukv[layer], mla_qn_g[layer], mla_kn_g[layer],
                           w_o_mla[layer])
        y_ret = retention_branch(r_q, r_k, r_v, r_g, positions, ret_gn_g[layer], w_o_ret[layer])
        y_mem = memory_branch(x_q, mem, mem_norm_g[layer], w_mem_kv[layer], x_qn_g[layer],
                              x_kn_g[layer], w_o_cross[layer])
        g = jax.nn.sigmoid(gates.reshape(B, S, N_BRANCH, D_MODEL))
        merged = g[:, :, 0] * y_mla + g[:, :, 1] * y_ret + g[:, :, 2] * y_mem
        h = h + merged @ w_out[layer]
        h = h + peer_ffn(rmsnorm(h, ffn_norm_g[layer]), peer_w_q[layer], peer_keys[layer],
                         peer_u[layer], peer_v[layer])
    return h
```

```python
import functools
import math

import jax
import jax.numpy as jnp
from jax import lax
from jax.experimental import pallas as pl
from jax.experimental.pallas import tpu as pltpu

D_MODEL = 1024
MEM_LEN = 256
MLA_HEADS = 8
MLA_Q_RANK = 384
MLA_KV_RANK = 128
MLA_NOPE = 64
MLA_ROPE = 32
MLA_V = 64
Q_BLOCK = 128
RET_HEADS = 4
RET_DK = 64
RET_DV = 128
RET_CHUNK = 128
X_HEADS = 4
X_DH = 128
N_BRANCH = 3
PEER_HEADS = 8
PEER_N_KEYS = 128
PEER_TOPK = 16
PEER_DQ = 256
PEER_DHALF = PEER_DQ // 2
PEER_TOKEN_BLOCK = 128
ROPE_BASE = 10000.0
EPS = 1e-6
IN_SPLITS = (MLA_Q_RANK, MLA_KV_RANK, MLA_ROPE,
             RET_HEADS * RET_DK, RET_HEADS * RET_DK, RET_HEADS * RET_DV, RET_HEADS * RET_DV,
             X_HEADS * X_DH, N_BRANCH * D_MODEL)

VMEM_LIMIT = 48 * 1024 * 1024


def _rmsnorm(x, g):
    xf = x.astype(jnp.float32)
    y = xf * lax.rsqrt(jnp.mean(xf * xf, axis=-1, keepdims=True) + EPS)
    return (y * g.astype(jnp.float32)).astype(x.dtype)


def _in_proj_kernel(x_ref, g_ref, *refs):
    n_w = len(refs) // 2
    w_refs, o_refs = refs[:n_w], refs[n_w:]
    x = x_ref[...]
    n = x * lax.rsqrt(jnp.mean(x * x, axis=-1, keepdims=True) + EPS) * g_ref[...]
    nb = n.astype(jnp.bfloat16)
    for w_ref, o_ref in zip(w_refs, o_refs):
        o_ref[...] = jnp.dot(nb, w_ref[...], preferred_element_type=jnp.float32).astype(o_ref.dtype)


def _in_proj(x2d, g, w_pieces, tm=512):
    t, d = x2d.shape
    in_specs = [pl.BlockSpec((tm, d), lambda i: (i, 0)), pl.BlockSpec((1, d), lambda i: (0, 0))]
    in_specs += [pl.BlockSpec(w.shape, lambda i: (0, 0)) for w in w_pieces]
    out_specs = [pl.BlockSpec((tm, w.shape[1]), lambda i: (i, 0)) for w in w_pieces]
    out_shape = [jax.ShapeDtypeStruct((t, w.shape[1]), jnp.float32) for w in w_pieces]
    return pl.pallas_call(
        _in_proj_kernel,
        grid=(t // tm,),
        in_specs=in_specs,
        out_specs=out_specs,
        out_shape=out_shape,
        compiler_params=pltpu.CompilerParams(dimension_semantics=("parallel",),
                                             vmem_limit_bytes=VMEM_LIMIT),
        name="in_proj",
    )(x2d, g.reshape(1, d), *w_pieces)


def _rope(x, positions):
    d = x.shape[-1]
    inv_freq = ROPE_BASE ** (-jnp.arange(0, d, 2, dtype=jnp.float32) / d)
    ang = positions.astype(jnp.float32)[..., None] * inv_freq
    cos = jnp.cos(ang)[:, :, None, :].astype(x.dtype)
    sin = jnp.sin(ang)[:, :, None, :].astype(x.dtype)
    x1, x2 = x[..., : d // 2], x[..., d // 2:]
    return jnp.concatenate([x1 * cos - x2 * sin, x1 * sin + x2 * cos], axis=-1)


def _causal_attention_blocked(q, k, v, scale):
    B, H, S, dqk = q.shape
    nb = S // Q_BLOCK
    qb = q.reshape(B, H, nb, Q_BLOCK, dqk).transpose(2, 0, 1, 3, 4)
    kpos = jnp.arange(S)

    def one_block(args):
        q_blk, bi = args
        qpos = bi * Q_BLOCK + jnp.arange(Q_BLOCK)
        s = jnp.einsum('bhqd,bhkd->bhqk', q_blk, k).astype(jnp.float32) * scale
        s = jnp.where(kpos[None, :] <= qpos[:, None], s, jnp.float32(-1e30))
        p = jax.nn.softmax(s, axis=-1).astype(v.dtype)
        return jnp.einsum('bhqk,bhkd->bhqd', p, v)

    o = lax.map(one_block, (qb, jnp.arange(nb)))
    return o.transpose(1, 2, 0, 3, 4).reshape(B, H, S, v.shape[-1])


def _mla_branch(c_q, c_kv, k_r, positions, q_norm_g, w_uq, kv_norm_g, w_ukv, qn_g, kn_g, w_o):
    B, S, _ = c_q.shape
    q = (_rmsnorm(c_q, q_norm_g) @ w_uq).reshape(B, S, MLA_HEADS, MLA_NOPE + MLA_ROPE)
    kv = (_rmsnorm(c_kv, kv_norm_g) @ w_ukv).reshape(B, S, MLA_HEADS, MLA_NOPE + MLA_V)
    k_nope, v = kv[..., :MLA_NOPE], kv[..., MLA_NOPE:]
    k_rope = jnp.broadcast_to(k_r[:, :, None, :], (B, S, MLA_HEADS, MLA_ROPE))
    k = jnp.concatenate([k_nope, k_rope], axis=-1)
    q = _rmsnorm(q, qn_g)
    k = _rmsnorm(k, kn_g)
    q = jnp.concatenate([q[..., :MLA_NOPE], _rope(q[..., MLA_NOPE:], positions)], axis=-1)
    k = jnp.concatenate([k[..., :MLA_NOPE], _rope(k[..., MLA_NOPE:], positions)], axis=-1)
    scale = (MLA_NOPE + MLA_ROPE) ** -0.5
    o = _causal_attention_blocked(q.transpose(0, 2, 1, 3), k.transpose(0, 2, 1, 3),
                                  v.transpose(0, 2, 1, 3), scale)
    o = o.transpose(0, 2, 1, 3).reshape(B, S, MLA_HEADS * MLA_V)
    return o @ w_o


def _retention_chunkwise(q, k, v, log_gamma):
    B, H, S, dk = q.shape
    dv = v.shape[-1]
    L = RET_CHUNK
    C = S // L
    dt = q.dtype
    qc = q.reshape(B, H, C, L, dk)
    kc = k.reshape(B, H, C, L, dk)
    vc = v.reshape(B, H, C, L, dv)
    idx = jnp.arange(L, dtype=jnp.float32)
    diff = idx[:, None] - idx[None, :]
    lg = log_gamma[:, None, None]
    decay = jnp.where(diff >= 0, jnp.exp(lg * jnp.maximum(diff, 0.0)), 0.0).astype(dt)
    xi = jnp.exp(log_gamma[:, None] * (idx + 1.0)).astype(dt)
    zeta = jnp.exp(log_gamma[:, None] * (L - 1.0 - idx)).astype(dt)
    inner = jnp.einsum('bhcid,bhcjd->bhcij', qc, kc) * decay[None, :, None]
    inner_out = jnp.einsum('bhcij,bhcje->bhcie', inner, vc)
    incr = jnp.einsum('bhcjd,bhcje->cbhde', kc * zeta[None, :, None, :, None], vc)
    chunk_decay = jnp.exp(log_gamma * L).astype(incr.dtype)[None, :, None, None]

    def step(R, inc):
        return chunk_decay * R + inc, R

    _, R_prev = lax.scan(step, jnp.zeros((B, H, dk, dv), incr.dtype), incr)
    cross = jnp.einsum('bhcid,cbhde->bhcie', qc * xi[None, :, None, :, None], R_prev)
    return (inner_out + cross).reshape(B, H, S, dv)


def _retention_branch(r_q, r_k, r_v, r_g, positions, gn_g, w_o):
    B, S, _ = r_q.shape
    q = _rope(r_q.reshape(B, S, RET_HEADS, RET_DK), positions)
    k = _rope(r_k.reshape(B, S, RET_HEADS, RET_DK), positions) * (RET_DK ** -0.5)
    v = r_v.reshape(B, S, RET_HEADS, RET_DV)
    log_gamma = jnp.log(1.0 - 2.0 ** (-5.0 - jnp.arange(RET_HEADS, dtype=jnp.float32)))
    o = _retention_chunkwise(q.transpose(0, 2, 1, 3), k.transpose(0, 2, 1, 3),
                             v.transpose(0, 2, 1, 3), log_gamma)
    o = _rmsnorm(o.transpose(0, 2, 1, 3), gn_g)
    o = o.reshape(B, S, RET_HEADS * RET_DV) * jax.nn.silu(r_g)
    return o @ w_o


def _memory_branch(x_q, mem, mem_g, w_mem_kv, qn_g, kn_g, w_o):
    B, S, _ = x_q.shape
    M = mem.shape[1]
    q = _rmsnorm(x_q.reshape(B, S, X_HEADS, X_DH), qn_g)
    kv = (_rmsnorm(mem, mem_g) @ w_mem_kv).reshape(B, M, X_HEADS, 2 * X_DH)
    k = _rmsnorm(kv[..., :X_DH], kn_g)
    v = kv[..., X_DH:]
    s = jnp.einsum('bshd,bmhd->bhsm', q, k).astype(jnp.float32) * (X_DH ** -0.5)
    p = jax.nn.softmax(s, axis=-1).astype(v.dtype)
    o = jnp.einsum('bhsm,bmhd->bshd', p, v).reshape(B, S, X_HEADS * X_DH)
    return o @ w_o


def _peer_ffn(xn, w_q, sub_keys, u_tab, v_tab):
    B, S, D = xn.shape
    T = B * S
    xt = xn.reshape(T, D)
    q = (xt @ w_q).reshape(T, PEER_HEADS, 2, PEER_DHALF)
    s = jnp.einsum('thcd,hcnd->thcn', q, sub_keys)
    sv, si = lax.top_k(s, PEER_TOPK)
    cand = (sv[:, :, 0, :, None] + sv[:, :, 1, None, :]).reshape(T, PEER_HEADS, PEER_TOPK * PEER_TOPK)
    cand_idx = (si[:, :, 0, :, None] * PEER_N_KEYS + si[:, :, 1, None, :]).reshape(
        T, PEER_HEADS, PEER_TOPK * PEER_TOPK)
    best, pos = lax.top_k(cand, PEER_TOPK)
    experts = jnp.take_along_axis(cand_idx, pos, axis=-1)
    gate = jax.nn.softmax(best.astype(jnp.float32), axis=-1).astype(xn.dtype)
    nb = T // PEER_TOKEN_BLOCK

    def one_block(args):
        xb, eb, gb = args
        u = u_tab[eb]
        a = jax.nn.gelu(jnp.einsum('thkd,td->thk', u, xb))
        vv = v_tab[eb]
        return jnp.einsum('thk,thkd->td', gb * a, vv)

    out = lax.map(one_block, (xt.reshape(nb, PEER_TOKEN_BLOCK, D),
                              experts.reshape(nb, PEER_TOKEN_BLOCK, PEER_HEADS, PEER_TOPK),
                              gate.reshape(nb, PEER_TOKEN_BLOCK, PEER_HEADS, PEER_TOPK)))
    return out.reshape(B, S, D)


def kernel(x, mem, positions, attn_norm_g, w_in, mla_q_norm_g, w_uq, mla_kv_norm_g, w_ukv,
           mla_qn_g, mla_kn_g, w_o_mla, ret_gn_g, w_o_ret, mem_norm_g, w_mem_kv, x_qn_g, x_kn_g,
           w_o_cross, w_out, ffn_norm_g, peer_w_q, peer_keys, peer_u, peer_v):
    B, S, D = x.shape
    T = B * S
    layer = 0
    h = x
    offs = [0]
    for sz in IN_SPLITS:
        offs.append(offs[-1] + sz)
    w_pieces = [w_in[layer][:, offs[i]:offs[i + 1]].astype(jnp.bfloat16) for i in range(len(IN_SPLITS))]
    outs = _in_proj(h.reshape(T, D), attn_norm_g[layer], w_pieces)
    c_q, c_kv, k_r, r_q, r_k, r_v, r_g, x_q, gates = [o.reshape(B, S, -1) for o in outs]
    y_mla = _mla_branch(c_q, c_kv, k_r, positions, mla_q_norm_g[layer], w_uq[layer],
                        mla_kv_norm_g[layer], w_ukv[layer], mla_qn_g[layer], mla_kn_g[layer],
                        w_o_mla[layer])
    y_ret = _retention_branch(r_q, r_k, r_v, r_g, positions, ret_gn_g[layer], w_o_ret[layer])
    y_mem = _memory_branch(x_q, mem, mem_norm_g[layer], w_mem_kv[layer], x_qn_g[layer],
                           x_kn_g[layer], w_o_cross[layer])
    g = jax.nn.sigmoid(gates.reshape(B, S, N_BRANCH, D_MODEL))
    merged = g[:, :, 0] * y_mla + g[:, :, 1] * y_ret + g[:, :, 2] * y_mem
    h = h + merged @ w_out[layer]
    h = h + _peer_ffn(_rmsnorm(h, ffn_norm_g[layer]), peer_w_q[layer], peer_keys[layer],
                      peer_u[layer], peer_v[layer])
    return h
```

```python
import functools
import math

import jax
import jax.numpy as jnp
from jax import lax
from jax.experimental import pallas as pl
from jax.experimental.pallas import tpu as pltpu

D_MODEL = 1024
MEM_LEN = 256
MLA_HEADS = 8
MLA_Q_RANK = 384
MLA_KV_RANK = 128
MLA_NOPE = 64
MLA_ROPE = 32
MLA_V = 64
Q_BLOCK = 128
RET_HEADS = 4
RET_DK = 64
RET_DV = 128
RET_CHUNK = 128
X_HEADS = 4
X_DH = 128
N_BRANCH = 3
PEER_HEADS = 8
PEER_N_KEYS = 128
PEER_TOPK = 16
PEER_DQ = 256
PEER_DHALF = PEER_DQ // 2
PEER_TOKEN_BLOCK = 128
ROPE_BASE = 10000.0
EPS = 1e-6
IN_SPLITS = (MLA_Q_RANK, MLA_KV_RANK, MLA_ROPE,
             RET_HEADS * RET_DK, RET_HEADS * RET_DK, RET_HEADS * RET_DV, RET_HEADS * RET_DV,
             X_HEADS * X_DH, N_BRANCH * D_MODEL)

VMEM_LIMIT = 48 * 1024 * 1024
PEER_VMEM_LIMIT = 56 * 1024 * 1024


def _rmsnorm(x, g):
    xf = x.astype(jnp.float32)
    y = xf * lax.rsqrt(jnp.mean(xf * xf, axis=-1, keepdims=True) + EPS)
    return (y * g.astype(jnp.float32)).astype(x.dtype)


def _in_proj_kernel(x_ref, g_ref, *refs):
    n_w = len(refs) // 2
    w_refs, o_refs = refs[:n_w], refs[n_w:]
    x = x_ref[...]
    n = x * lax.rsqrt(jnp.mean(x * x, axis=-1, keepdims=True) + EPS) * g_ref[...]
    nb = n.astype(jnp.bfloat16)
    for w_ref, o_ref in zip(w_refs, o_refs):
        o_ref[...] = jnp.dot(nb, w_ref[...], preferred_element_type=jnp.float32).astype(o_ref.dtype)


def _in_proj(x2d, g, w_pieces, tm=512):
    t, d = x2d.shape
    in_specs = [pl.BlockSpec((tm, d), lambda i: (i, 0)), pl.BlockSpec((1, d), lambda i: (0, 0))]
    in_specs += [pl.BlockSpec(w.shape, lambda i: (0, 0)) for w in w_pieces]
    out_specs = [pl.BlockSpec((tm, w.shape[1]), lambda i: (i, 0)) for w in w_pieces]
    out_shape = [jax.ShapeDtypeStruct((t, w.shape[1]), jnp.float32) for w in w_pieces]
    return pl.pallas_call(
        _in_proj_kernel,
        grid=(t // tm,),
        in_specs=in_specs,
        out_specs=out_specs,
        out_shape=out_shape,
        compiler_params=pltpu.CompilerParams(dimension_semantics=("parallel",),
                                             vmem_limit_bytes=VMEM_LIMIT),
        name="in_proj",
    )(x2d, g.reshape(1, d), *w_pieces)


def _rope(x, positions):
    d = x.shape[-1]
    inv_freq = ROPE_BASE ** (-jnp.arange(0, d, 2, dtype=jnp.float32) / d)
    ang = positions.astype(jnp.float32)[..., None] * inv_freq
    cos = jnp.cos(ang)[:, :, None, :].astype(x.dtype)
    sin = jnp.sin(ang)[:, :, None, :].astype(x.dtype)
    x1, x2 = x[..., : d // 2], x[..., d // 2:]
    return jnp.concatenate([x1 * cos - x2 * sin, x1 * sin + x2 * cos], axis=-1)


def _causal_attention_blocked(q, k, v, scale):
    B, H, S, dqk = q.shape
    nb = S // Q_BLOCK
    qb = q.reshape(B, H, nb, Q_BLOCK, dqk).transpose(2, 0, 1, 3, 4)
    kpos = jnp.arange(S)

    def one_block(args):
        q_blk, bi = args
        qpos = bi * Q_BLOCK + jnp.arange(Q_BLOCK)
        s = jnp.einsum('bhqd,bhkd->bhqk', q_blk, k).astype(jnp.float32) * scale
        s = jnp.where(kpos[None, :] <= qpos[:, None], s, jnp.float32(-1e30))
        p = jax.nn.softmax(s, axis=-1).astype(v.dtype)
        return jnp.einsum('bhqk,bhkd->bhqd', p, v)

    o = lax.map(one_block, (qb, jnp.arange(nb)))
    return o.transpose(1, 2, 0, 3, 4).reshape(B, H, S, v.shape[-1])


def _mla_branch(c_q, c_kv, k_r, positions, q_norm_g, w_uq, kv_norm_g, w_ukv, qn_g, kn_g, w_o):
    B, S, _ = c_q.shape
    q = (_rmsnorm(c_q, q_norm_g) @ w_uq).reshape(B, S, MLA_HEADS, MLA_NOPE + MLA_ROPE)
    kv = (_rmsnorm(c_kv, kv_norm_g) @ w_ukv).reshape(B, S, MLA_HEADS, MLA_NOPE + MLA_V)
    k_nope, v = kv[..., :MLA_NOPE], kv[..., MLA_NOPE:]
    k_rope = jnp.broadcast_to(k_r[:, :, None, :], (B, S, MLA_HEADS, MLA_ROPE))
    k = jnp.concatenate([k_nope, k_rope], axis=-1)
    q = _rmsnorm(q, qn_g)
    k = _rmsnorm(k, kn_g)
    q = jnp.concatenate([q[..., :MLA_NOPE], _rope(q[..., MLA_NOPE:], positions)], axis=-1)
    k = jnp.concatenate([k[..., :MLA_NOPE], _rope(k[..., MLA_NOPE:], positions)], axis=-1)
    scale = (MLA_NOPE + MLA_ROPE) ** -0.5
    o = _causal_attention_blocked(q.transpose(0, 2, 1, 3), k.transpose(0, 2, 1, 3),
                                  v.transpose(0, 2, 1, 3), scale)
    o = o.transpose(0, 2, 1, 3).reshape(B, S, MLA_HEADS * MLA_V)
    return o @ w_o


def _retention_chunkwise(q, k, v, log_gamma):
    B, H, S, dk = q.shape
    dv = v.shape[-1]
    L = RET_CHUNK
    C = S // L
    dt = q.dtype
    qc = q.reshape(B, H, C, L, dk)
    kc = k.reshape(B, H, C, L, dk)
    vc = v.reshape(B, H, C, L, dv)
    idx = jnp.arange(L, dtype=jnp.float32)
    diff = idx[:, None] - idx[None, :]
    lg = log_gamma[:, None, None]
    decay = jnp.where(diff >= 0, jnp.exp(lg * jnp.maximum(diff, 0.0)), 0.0).astype(dt)
    xi = jnp.exp(log_gamma[:, None] * (idx + 1.0)).astype(dt)
    zeta = jnp.exp(log_gamma[:, None] * (L - 1.0 - idx)).astype(dt)
    inner = jnp.einsum('bhcid,bhcjd->bhcij', qc, kc) * decay[None, :, None]
    inner_out = jnp.einsum('bhcij,bhcje->bhcie', inner, vc)
    incr = jnp.einsum('bhcjd,bhcje->cbhde', kc * zeta[None, :, None, :, None], vc)
    chunk_decay = jnp.exp(log_gamma * L).astype(incr.dtype)[None, :, None, None]

    def step(R, inc):
        return chunk_decay * R + inc, R

    _, R_prev = lax.scan(step, jnp.zeros((B, H, dk, dv), incr.dtype), incr)
    cross = jnp.einsum('bhcid,cbhde->bhcie', qc * xi[None, :, None, :, None], R_prev)
    return (inner_out + cross).reshape(B, H, S, dv)


def _retention_branch(r_q, r_k, r_v, r_g, positions, gn_g, w_o):
    B, S, _ = r_q.shape
    q = _rope(r_q.reshape(B, S, RET_HEADS, RET_DK), positions)
    k = _rope(r_k.reshape(B, S, RET_HEADS, RET_DK), positions) * (RET_DK ** -0.5)
    v = r_v.reshape(B, S, RET_HEADS, RET_DV)
    log_gamma = jnp.log(1.0 - 2.0 ** (-5.0 - jnp.arange(RET_HEADS, dtype=jnp.float32)))
    o = _retention_chunkwise(q.transpose(0, 2, 1, 3), k.transpose(0, 2, 1, 3),
                             v.transpose(0, 2, 1, 3), log_gamma)
    o = _rmsnorm(o.transpose(0, 2, 1, 3), gn_g)
    o = o.reshape(B, S, RET_HEADS * RET_DV) * jax.nn.silu(r_g)
    return o @ w_o


def _memory_branch(x_q, mem, mem_g, w_mem_kv, qn_g, kn_g, w_o):
    B, S, _ = x_q.shape
    M = mem.shape[1]
    q = _rmsnorm(x_q.reshape(B, S, X_HEADS, X_DH), qn_g)
    kv = (_rmsnorm(mem, mem_g) @ w_mem_kv).reshape(B, M, X_HEADS, 2 * X_DH)
    k = _rmsnorm(kv[..., :X_DH], kn_g)
    v = kv[..., X_DH:]
    s = jnp.einsum('bshd,bmhd->bhsm', q, k).astype(jnp.float32) * (X_DH ** -0.5)
    p = jax.nn.softmax(s, axis=-1).astype(v.dtype)
    o = jnp.einsum('bhsm,bmhd->bshd', p, v).reshape(B, S, X_HEADS * X_DH)
    return o @ w_o


PEER_SLOTS = PEER_HEADS * PEER_TOPK
ROW_TILE = D_MODEL // 128
PAIR_ROWS = 2 * ROW_TILE


def _pack_expert_table(tab):
    n, d = tab.shape
    t = tab.astype(jnp.bfloat16).reshape(2, n // 2, ROW_TILE, 128)
    return t.transpose(1, 2, 0, 3).reshape(n // 2, PAIR_ROWS, 128)


def _peer_u_kernel(idx_ref, xdup_ref, hi_ref, gate_ref, ksum_ref, tab_ref, w_ref, prod_ref):
    tb = xdup_ref.shape[0]

    def token_body(t, carry):
        xd = xdup_ref[t]
        for k in range(PEER_SLOTS):
            prod_ref[t, :, k * 128:(k + 1) * 128] = tab_ref[idx_ref[t, k]] * xd
        return carry

    lax.fori_loop(0, tb, token_body, 0)
    prod = prod_ref[...].reshape(tb * PAIR_ROWS, PEER_SLOTS * 128)
    r = jnp.dot(prod, ksum_ref[...], preferred_element_type=jnp.float32)
    r = r.reshape(tb, PAIR_ROWS, PEER_SLOTS)
    odd = (lax.broadcasted_iota(jnp.int32, r.shape, 1) & 1) == 1
    a_hi = jnp.sum(jnp.where(odd, r, 0.0), axis=1)
    a_lo = jnp.sum(jnp.where(odd, 0.0, r), axis=1)
    a = jnp.where(hi_ref[...] == 1, a_hi, a_lo)
    w_ref[...] = gate_ref[...] * jax.nn.gelu(a)


def _peer_u(idx, xdup, hi, gate, tab, tb):
    t = idx.shape[0]
    ksum = jnp.repeat(jnp.eye(PEER_SLOTS, dtype=jnp.bfloat16), 128, axis=0)
    return pl.pallas_call(
        _peer_u_kernel,
        grid=(t // tb,),
        in_specs=[
            pl.BlockSpec((tb, PEER_SLOTS), lambda i: (i, 0), memory_space=pltpu.SMEM),
            pl.BlockSpec((tb, PAIR_ROWS, 128), lambda i: (i, 0, 0)),
            pl.BlockSpec((tb, PEER_SLOTS), lambda i: (i, 0)),
            pl.BlockSpec((tb, PEER_SLOTS), lambda i: (i, 0)),
            pl.BlockSpec(ksum.shape, lambda i: (0, 0), pipeline_mode=pl.Buffered(1)),
            pl.BlockSpec(tab.shape, lambda i: (0, 0, 0), pipeline_mode=pl.Buffered(1)),
        ],
        out_specs=pl.BlockSpec((tb, PEER_SLOTS), lambda i: (i, 0)),
        out_shape=jax.ShapeDtypeStruct((t, PEER_SLOTS), jnp.float32),
        scratch_shapes=[pltpu.VMEM((tb, PAIR_ROWS, PEER_SLOTS * 128), jnp.bfloat16)],
        compiler_params=pltpu.CompilerParams(dimension_semantics=("arbitrary",),
                                             vmem_limit_bytes=PEER_VMEM_LIMIT),
        name="peer_u",
    )(idx, xdup, hi, gate, ksum, tab)


def _peer_v_kernel(idx_ref, hi_ref, w_ref, h_ref, expand_ref, tab_ref, o_ref, lhs_ref, g_ref):
    tb = hi_ref.shape[0]
    w_x = jnp.dot(w_ref[...].astype(jnp.bfloat16), expand_ref[...], preferred_element_type=jnp.float32)
    hi_x = jnp.dot(hi_ref[...].astype(jnp.bfloat16), expand_ref[...], preferred_element_type=jnp.float32)
    shape = (tb, PAIR_ROWS, PEER_SLOTS * PAIR_ROWS)
    row_in_tile = lax.broadcasted_iota(jnp.int32, shape, 2) & (PAIR_ROWS - 1)
    chunk = lax.broadcasted_iota(jnp.int32, shape, 1)
    want = 2 * chunk + hi_x.astype(jnp.int32)[:, None, :]
    lhs_ref[...] = jnp.where(row_in_tile == want, w_x[:, None, :], 0.0).astype(jnp.bfloat16)

    def token_body(t, carry):
        for k in range(PEER_SLOTS):
            g_ref[k * PAIR_ROWS:(k + 1) * PAIR_ROWS, :] = tab_ref[idx_ref[t, k]]
        o = jnp.dot(lhs_ref[t], g_ref[...], preferred_element_type=jnp.float32)
        o_ref[t] = h_ref[t] + o[:ROW_TILE]
        return carry

    lax.fori_loop(0, tb, token_body, 0)


def _peer_v(idx, hi, w, h3, tab, tb):
    t = idx.shape[0]
    expand = jnp.repeat(jnp.eye(PEER_SLOTS, dtype=jnp.bfloat16), PAIR_ROWS, axis=1)
    return pl.pallas_call(
        _peer_v_kernel,
        grid=(t // tb,),
        in_specs=[
            pl.BlockSpec((tb, PEER_SLOTS), lambda i: (i, 0), memory_space=pltpu.SMEM),
            pl.BlockSpec((tb, PEER_SLOTS), lambda i: (i, 0)),
            pl.BlockSpec((tb, PEER_SLOTS), lambda i: (i, 0)),
            pl.BlockSpec((tb, ROW_TILE, 128), lambda i: (i, 0, 0)),
            pl.BlockSpec(expand.shape, lambda i: (0, 0), pipeline_mode=pl.Buffered(1)),
            pl.BlockSpec(tab.shape, lambda i: (0, 0, 0), pipeline_mode=pl.Buffered(1)),
        ],
        out_specs=pl.BlockSpec((tb, ROW_TILE, 128), lambda i: (i, 0, 0)),
        out_shape=jax.ShapeDtypeStruct((t, ROW_TILE, 128), jnp.float32),
        scratch_shapes=[pltpu.VMEM((tb, PAIR_ROWS, PEER_SLOTS * PAIR_ROWS), jnp.bfloat16),
                        pltpu.VMEM((PEER_SLOTS * PAIR_ROWS, 128), jnp.bfloat16)],
        compiler_params=pltpu.CompilerParams(dimension_semantics=("arbitrary",),
                                             vmem_limit_bytes=PEER_VMEM_LIMIT),
        name="peer_v",
    )(idx, hi, w, h3, expand, tab)


def _peer_experts(h2d, xn2d, experts, gate, u_tab, v_tab, tb_u=16, tb_v=64):
    t, d = xn2d.shape
    half = u_tab.shape[0] // 2
    hi = (experts >= half).astype(jnp.int32)
    idx = experts - hi * half
    xdup = jnp.repeat(xn2d.astype(jnp.bfloat16).reshape(t, ROW_TILE, 128), 2, axis=1)
    w = _peer_u(idx, xdup, hi, gate, _pack_expert_table(u_tab), tb_u)
    out = _peer_v(idx, hi, w, h2d.reshape(t, ROW_TILE, 128), _pack_expert_table(v_tab), tb_v)
    return out.reshape(t, d)


def _peer_select(xn, w_q, sub_keys):
    T = xn.shape[0]
    q = (xn @ w_q).reshape(T, PEER_HEADS, 2, PEER_DHALF)
    s = jnp.einsum('thcd,hcnd->thcn', q, sub_keys)
    sv, si = lax.top_k(s, PEER_TOPK)
    cand = (sv[:, :, 0, :, None] + sv[:, :, 1, None, :]).reshape(T, PEER_HEADS, PEER_TOPK * PEER_TOPK)
    cand_idx = (si[:, :, 0, :, None] * PEER_N_KEYS + si[:, :, 1, None, :]).reshape(
        T, PEER_HEADS, PEER_TOPK * PEER_TOPK)
    best, pos = lax.top_k(cand, PEER_TOPK)
    experts = jnp.take_along_axis(cand_idx, pos, axis=-1)
    gate = jax.nn.softmax(best.astype(jnp.float32), axis=-1)
    return experts.reshape(T, PEER_SLOTS), gate.reshape(T, PEER_SLOTS)


def kernel(x, mem, positions, attn_norm_g, w_in, mla_q_norm_g, w_uq, mla_kv_norm_g, w_ukv,
           mla_qn_g, mla_kn_g, w_o_mla, ret_gn_g, w_o_ret, mem_norm_g, w_mem_kv, x_qn_g, x_kn_g,
           w_o_cross, w_out, ffn_norm_g, peer_w_q, peer_keys, peer_u, peer_v):
    B, S, D = x.shape
    T = B * S
    layer = 0
    h = x
    offs = [0]
    for sz in IN_SPLITS:
        offs.append(offs[-1] + sz)
    w_pieces = [w_in[layer][:, offs[i]:offs[i + 1]].astype(jnp.bfloat16) for i in range(len(IN_SPLITS))]
    outs = _in_proj(h.reshape(T, D), attn_norm_g[layer], w_pieces)
    c_q, c_kv, k_r, r_q, r_k, r_v, r_g, x_q, gates = [o.reshape(B, S, -1) for o in outs]
    y_mla = _mla_branch(c_q, c_kv, k_r, positions, mla_q_norm_g[layer], w_uq[layer],
                        mla_kv_norm_g[layer], w_ukv[layer], mla_qn_g[layer], mla_kn_g[layer],
                        w_o_mla[layer])
    y_ret = _retention_branch(r_q, r_k, r_v, r_g, positions, ret_gn_g[layer], w_o_ret[layer])
    y_mem = _memory_branch(x_q, mem, mem_norm_g[layer], w_mem_kv[layer], x_qn_g[layer],
                           x_kn_g[layer], w_o_cross[layer])
    g = jax.nn.sigmoid(gates.reshape(B, S, N_BRANCH, D_MODEL))
    merged = g[:, :, 0] * y_mla + g[:, :, 1] * y_ret + g[:, :, 2] * y_mem
    h = h + merged @ w_out[layer]
    h2d = h.reshape(T, D)
    xn2d = _rmsnorm(h2d, ffn_norm_g[layer])
    experts, gate = _peer_select(xn2d, peer_w_q[layer], peer_keys[layer])
    out = _peer_experts(h2d, xn2d, experts, gate, peer_u[layer], peer_v[layer])
    return out.reshape(B, S, D)
```

```python
import functools

import jax
import jax.numpy as jnp
from jax import lax
from jax.experimental import pallas as pl
from jax.experimental.pallas import tpu as pltpu

D_MODEL = 1024
MEM_LEN = 256
MLA_HEADS = 8
MLA_Q_RANK = 384
MLA_KV_RANK = 128
MLA_NOPE = 64
MLA_ROPE = 32
MLA_V = 64
RET_HEADS = 4
RET_DK = 64
RET_DV = 128
RET_CHUNK = 128
X_HEADS = 4
X_DH = 128
N_BRANCH = 3
PEER_HEADS = 8
PEER_N_KEYS = 128
PEER_TOPK = 16
PEER_DQ = 256
PEER_DHALF = PEER_DQ // 2
ROPE_BASE = 10000.0
EPS = 1e-6
IN_SPLITS = (MLA_Q_RANK, MLA_KV_RANK, MLA_ROPE,
             RET_HEADS * RET_DK, RET_HEADS * RET_DK, RET_HEADS * RET_DV, RET_HEADS * RET_DV,
             X_HEADS * X_DH, N_BRANCH * D_MODEL)

LANES = 128
VMEM_LIMIT = 48 * 1024 * 1024
PEER_VMEM_LIMIT = 56 * 1024 * 1024
BF16 = jnp.bfloat16
F32 = jnp.float32


def _norm(x, g):
    return x * lax.rsqrt(jnp.mean(x * x, axis=-1, keepdims=True) + EPS) * g


def _row_specs(arrays, tm):
    return [pl.BlockSpec((tm, a.shape[1]), lambda i: (i, 0)) for a in arrays]


def _const_specs(arrays):
    return [pl.BlockSpec(a.shape, lambda i, n=a.ndim: (0,) * n) for a in arrays]


def _in_proj_kernel(x_ref, g_ref, *refs):
    n_w = len(refs) // 2
    w_refs, o_refs = refs[:n_w], refs[n_w:]
    nb = _norm(x_ref[...], g_ref[...]).astype(BF16)
    for w_ref, o_ref in zip(w_refs, o_refs):
        o_ref[...] = jnp.dot(nb, w_ref[...], preferred_element_type=F32).astype(o_ref.dtype)


def _in_proj(x2d, g, w_pieces, out_dtypes, tm=512):
    t, d = x2d.shape
    consts = [g.reshape(1, d)] + list(w_pieces)
    return pl.pallas_call(
        _in_proj_kernel,
        grid=(t // tm,),
        in_specs=_row_specs([x2d], tm) + _const_specs(consts),
        out_specs=[pl.BlockSpec((tm, w.shape[1]), lambda i: (i, 0)) for w in w_pieces],
        out_shape=[jax.ShapeDtypeStruct((t, w.shape[1]), dt) for w, dt in zip(w_pieces, out_dtypes)],
        compiler_params=pltpu.CompilerParams(dimension_semantics=("parallel",),
                                             vmem_limit_bytes=VMEM_LIMIT),
        name="in_proj",
    )(x2d, *consts)


def _rope_patterns(positions, n_rope, lead, width):
    inv_freq = ROPE_BASE ** (-jnp.arange(0, n_rope, 2, dtype=F32) / n_rope)
    ang = positions.astype(F32).reshape(-1, 1) * inv_freq
    cos, sin = jnp.cos(ang), jnp.sin(ang)
    t = ang.shape[0]
    tail = width - lead - n_rope
    cosp = jnp.concatenate([jnp.ones((t, lead)), cos, cos, jnp.zeros((t, tail))], axis=1)
    sinp = jnp.concatenate([jnp.zeros((t, lead)), -sin, sin, jnp.zeros((t, tail))], axis=1)
    rep = LANES // width
    return jnp.tile(cosp, (1, rep)), jnp.tile(sinp, (1, rep))


def _rotate_half(x, cosp, sinp, n_rope, lead, width):
    half = n_rope // 2
    lane = lax.broadcasted_iota(jnp.int32, x.shape, 1) & (width - 1)
    partner = jnp.where(lane < lead + half, pltpu.roll(x, LANES - half, 1), pltpu.roll(x, half, 1))
    return x * cosp + partner * sinp


MLA_QK = MLA_NOPE + MLA_ROPE


def _pad_head_cols(w, per_head):
    r = w.shape[0]
    w = w.reshape(r, MLA_HEADS, per_head)
    return jnp.pad(w, ((0, 0), (0, 0), (0, LANES - per_head))).reshape(r, MLA_HEADS * LANES)


def _mla_prep_kernel(cq_ref, ckv_ref, kr_ref, cos_ref, sin_ref, gq_ref, gkv_ref, wuq_ref, wuk_ref,
                     wuv_ref, qn_ref, kn_ref, q_ref, k_ref, v_ref):
    cqn = _norm(cq_ref[...], gq_ref[...]).astype(BF16)
    ckvn = _norm(ckv_ref[...], gkv_ref[...]).astype(BF16)
    qp = jnp.dot(cqn, wuq_ref[...], preferred_element_type=F32)
    kp = jnp.dot(ckvn, wuk_ref[...], preferred_element_type=F32)
    v_ref[...] = jnp.dot(ckvn, wuv_ref[...], preferred_element_type=F32).astype(v_ref.dtype)
    tm = qp.shape[0]
    kr = jnp.concatenate([jnp.zeros((tm, MLA_NOPE), F32), kr_ref[...],
                          jnp.zeros((tm, LANES - MLA_QK), F32)], axis=1)
    cosp, sinp = cos_ref[...], sin_ref[...]

    def qk_norm_rope(xh, g, scale):
        ss = jnp.sum(xh * xh, axis=-1, keepdims=True) * (1.0 / MLA_QK)
        xn = xh * lax.rsqrt(ss + EPS) * g
        return _rotate_half(xn, cosp, sinp, MLA_ROPE, MLA_NOPE, LANES) * scale

    for hd in range(MLA_HEADS):
        sl = slice(hd * LANES, (hd + 1) * LANES)
        q_ref[:, sl] = qk_norm_rope(qp[:, sl], qn_ref[...], MLA_QK ** -0.5).astype(q_ref.dtype)
        k_ref[:, sl] = qk_norm_rope(kp[:, sl] + kr, kn_ref[...], 1.0).astype(k_ref.dtype)


def _mla_prep(c_q, c_kv, k_r, cosp, sinp, q_norm_g, w_uq, kv_norm_g, w_ukv, qn_g, kn_g, tm=512):
    t = c_q.shape[0]
    w_ukv3 = w_ukv.reshape(MLA_KV_RANK, MLA_HEADS, MLA_NOPE + MLA_V)
    w_uk = _pad_head_cols(w_ukv3[:, :, :MLA_NOPE].reshape(MLA_KV_RANK, -1), MLA_NOPE).astype(BF16)
    w_uv = w_ukv3[:, :, MLA_NOPE:].reshape(MLA_KV_RANK, MLA_HEADS * MLA_V).astype(BF16)
    w_uqp = _pad_head_cols(w_uq, MLA_QK).astype(BF16)
    pad_g = lambda g: jnp.pad(g, (0, LANES - MLA_QK)).reshape(1, LANES)
    consts = [q_norm_g.reshape(1, -1), kv_norm_g.reshape(1, -1), w_uqp, w_uk, w_uv, pad_g(qn_g), pad_g(kn_g)]
    rows = [c_q, c_kv, k_r, cosp, sinp]
    wide = MLA_HEADS * LANES
    return pl.pallas_call(
        _mla_prep_kernel,
        grid=(t // tm,),
        in_specs=_row_specs(rows, tm) + _const_specs(consts),
        out_specs=[pl.BlockSpec((tm, wide), lambda i: (i, 0)),
                   pl.BlockSpec((tm, wide), lambda i: (i, 0)),
                   pl.BlockSpec((tm, MLA_HEADS * MLA_V), lambda i: (i, 0))],
        out_shape=[jax.ShapeDtypeStruct((t, wide), BF16), jax.ShapeDtypeStruct((t, wide), BF16),
                   jax.ShapeDtypeStruct((t, MLA_HEADS * MLA_V), BF16)],
        compiler_params=pltpu.CompilerParams(dimension_semantics=("parallel",),
                                             vmem_limit_bytes=VMEM_LIMIT),
        name="mla_prep",
    )(*rows, *consts)


def _mla_attn_kernel(q_ref, k_ref, v_ref, o_ref, *, tile):
    qi = pl.program_id(2)
    row = lax.broadcasted_iota(jnp.int32, (tile, tile), 0)
    col = lax.broadcasted_iota(jnp.int32, (tile, tile), 1)
    outs = []
    for a in range(2):
        lanes = slice(a * LANES, (a + 1) * LANES)
        q = q_ref[0, :, lanes]

        def step(j, carry, masked):
            m, l, acc = carry
            start = pl.multiple_of(j * tile, tile)
            kj = k_ref[0, pl.ds(start, tile), lanes]
            s = lax.dot_general(q, kj, (((1,), (1,)), ((), ())), preferred_element_type=F32)
            if masked:
                s = jnp.where(col <= row, s, -1e30)
            m_new = jnp.maximum(m, jnp.max(s, axis=-1, keepdims=True))
            alpha = jnp.exp(m - m_new)
            p = jnp.exp(s - m_new)
            l = alpha * l + jnp.sum(p, axis=-1, keepdims=True)
            pv = jnp.dot(p.astype(BF16), v_ref[0, pl.ds(start, tile), :], preferred_element_type=F32)
            return m_new, l, alpha * acc + pv

        init = (jnp.full((tile, 1), -1e30, F32), jnp.zeros((tile, 1), F32),
                jnp.zeros((tile, 2 * MLA_V), F32))
        carry = lax.fori_loop(0, qi, functools.partial(step, masked=False), init)
        m, l, acc = step(qi, carry, True)
        outs.append(acc / l)
    lane = lax.broadcasted_iota(jnp.int32, outs[0].shape, 1)
    o_ref[0] = jnp.where(lane < MLA_V, outs[0], outs[1]).astype(o_ref.dtype)


def _mla_attn(q, k, v, tile=512):
    b, s, _ = q.shape
    pair = 2 * LANES
    return pl.pallas_call(
        functools.partial(_mla_attn_kernel, tile=tile),
        grid=(b, MLA_HEADS // 2, s // tile),
        in_specs=[pl.BlockSpec((1, tile, pair), lambda bi, hp, qi: (bi, qi, hp)),
                  pl.BlockSpec((1, s, pair), lambda bi, hp, qi: (bi, 0, hp)),
                  pl.BlockSpec((1, s, 2 * MLA_V), lambda bi, hp, qi: (bi, 0, hp))],
        out_specs=pl.BlockSpec((1, tile, 2 * MLA_V), lambda bi, hp, qi: (bi, qi, hp)),
        out_shape=jax.ShapeDtypeStruct((b, s, MLA_HEADS * MLA_V), BF16),
        compiler_params=pltpu.CompilerParams(dimension_semantics=("parallel", "parallel", "arbitrary"),
                                             vmem_limit_bytes=VMEM_LIMIT),
        name="mla_attn",
    )(q, k, v)


def _mla_branch(c_q, c_kv, k_r, positions, q_norm_g, w_uq, kv_norm_g, w_ukv, qn_g, kn_g, b, s):
    cosp, sinp = _rope_patterns(positions, MLA_ROPE, MLA_NOPE, LANES)
    q, k, v = _mla_prep(c_q, c_kv, k_r, cosp, sinp, q_norm_g, w_uq, kv_norm_g, w_ukv, qn_g, kn_g)
    o = _mla_attn(q.reshape(b, s, -1), k.reshape(b, s, -1), v.reshape(b, s, -1))
    return o.reshape(b * s, -1)


def _retention_kernel(q_ref, k_ref, v_ref, g_ref, cos_ref, sin_ref, decay_ref, xi_ref, zeta_ref,
                      cd_ref, gn_ref, o_ref):
    L = RET_CHUNK
    n_chunks = q_ref.shape[1] // L
    lane = lax.broadcasted_iota(jnp.int32, (L, LANES), 1)

    def chunk_body(c, states):
        rows = pl.ds(pl.multiple_of(c * L, L), L)
        cosp, sinp = cos_ref[0, rows, :], sin_ref[0, rows, :]
        qc = _rotate_half(q_ref[0, rows, :], cosp, sinp, RET_DK, 0, RET_DK)
        kc = _rotate_half(k_ref[0, rows, :], cosp, sinp, RET_DK, 0, RET_DK) * (RET_DK ** -0.5)
        qx = qc * xi_ref[0]
        kz = kc * zeta_ref[0]
        new_states = []
        for a in range(2):
            mine = (lane >= a * RET_DK) & (lane < (a + 1) * RET_DK)
            qa = jnp.where(mine, qc, 0.0).astype(BF16)
            ka = jnp.where(mine, kc, 0.0).astype(BF16)
            va = v_ref[0, rows, a * RET_DV:(a + 1) * RET_DV]
            inner = lax.dot_general(qa, ka, (((1,), (1,)), ((), ())),
                                    preferred_element_type=F32) * decay_ref[a]
            cross = jnp.dot(jnp.where(mine, qx, 0.0).astype(BF16), states[a].astype(BF16),
                            preferred_element_type=F32)
            o = jnp.dot(inner.astype(BF16), va, preferred_element_type=F32) + cross
            kzt = jnp.where(mine, kz, 0.0).T.astype(BF16)
            incr = jnp.dot(kzt, va, preferred_element_type=F32)
            new_states.append(cd_ref[a, 0:1, :] * states[a] + incr)
            o = _norm(o, gn_ref[a])
            gate = g_ref[0, rows, a * RET_DV:(a + 1) * RET_DV].astype(F32)
            o_ref[0, rows, a * RET_DV:(a + 1) * RET_DV] = (o * gate * jax.nn.sigmoid(gate)).astype(o_ref.dtype)
        return tuple(new_states)

    zero = jnp.zeros((LANES, RET_DV), F32)
    lax.fori_loop(0, n_chunks, chunk_body, (zero, zero))


def _retention_branch(r_q, r_k, r_v, r_g, positions, gn_g):
    b, s, _ = r_q.shape
    L = RET_CHUNK
    cosp, sinp = _rope_patterns(positions, RET_DK, 0, RET_DK)
    cosp, sinp = cosp.reshape(b, s, LANES), sinp.reshape(b, s, LANES)
    log_gamma = jnp.log(1.0 - 2.0 ** (-5.0 - jnp.arange(RET_HEADS, dtype=F32)))
    idx = jnp.arange(L, dtype=F32)
    diff = idx[:, None] - idx[None, :]
    decay = jnp.where(diff >= 0, jnp.exp(log_gamma[:, None, None] * jnp.maximum(diff, 0.0)), 0.0)
    xi = jnp.exp(log_gamma[:, None] * (idx + 1.0))
    zeta = jnp.exp(log_gamma[:, None] * (L - 1.0 - idx))
    per_pair = lambda a: jnp.repeat(a.reshape(RET_HEADS // 2, 2, L), RET_DK, axis=1).transpose(0, 2, 1)
    cd = jnp.broadcast_to(jnp.exp(log_gamma * L)[:, None, None], (RET_HEADS, 8, RET_DV))
    gn = gn_g.reshape(RET_HEADS, 1, RET_DV)
    n_pairs = RET_HEADS // 2
    seq = lambda width: pl.BlockSpec((1, s, width), lambda bi, hp: (bi, 0, hp))
    return pl.pallas_call(
        _retention_kernel,
        grid=(b, n_pairs),
        in_specs=[seq(LANES), seq(LANES), seq(2 * RET_DV), seq(2 * RET_DV),
                  pl.BlockSpec((1, s, LANES), lambda bi, hp: (bi, 0, 0)),
                  pl.BlockSpec((1, s, LANES), lambda bi, hp: (bi, 0, 0)),
                  pl.BlockSpec((2, L, L), lambda bi, hp: (hp, 0, 0)),
                  pl.BlockSpec((1, L, LANES), lambda bi, hp: (hp, 0, 0)),
                  pl.BlockSpec((1, L, LANES), lambda bi, hp: (hp, 0, 0)),
                  pl.BlockSpec((2, 8, RET_DV), lambda bi, hp: (hp, 0, 0)),
                  pl.BlockSpec((2, 1, RET_DV), lambda bi, hp: (hp, 0, 0))],
        out_specs=seq(2 * RET_DV),
        out_shape=jax.ShapeDtypeStruct((b, s, RET_HEADS * RET_DV), BF16),
        compiler_params=pltpu.CompilerParams(dimension_semantics=("parallel", "parallel"),
                                             vmem_limit_bytes=VMEM_LIMIT),
        name="retention",
    )(r_q, r_k, r_v, r_g, cosp, sinp, decay, per_pair(xi), per_pair(zeta), cd, gn)


def _mem_kv_kernel(mem_ref, g_ref, w_ref, kn_ref, k_ref, v_ref):
    kv = jnp.dot(_norm(mem_ref[0], g_ref[...]).astype(BF16), w_ref[...], preferred_element_type=F32)
    width = X_HEADS * X_DH
    for hd in range(X_HEADS):
        sl = slice(hd * X_DH, (hd + 1) * X_DH)
        k_ref[0, :, sl] = _norm(kv[:, sl], kn_ref[...]).astype(k_ref.dtype)
    v_ref[0] = kv[:, width:].astype(v_ref.dtype)


def _mem_kv(mem, mem_g, w_mem_kv, kn_g):
    b, m, d = mem.shape
    width = X_HEADS * X_DH
    w4 = w_mem_kv.reshape(d, X_HEADS, 2, X_DH)
    w = jnp.concatenate([w4[:, :, 0].reshape(d, width), w4[:, :, 1].reshape(d, width)], axis=1).astype(BF16)
    consts = [mem_g.reshape(1, d), w, kn_g.reshape(1, X_DH)]
    blk = pl.BlockSpec((1, m, width), lambda i: (i, 0, 0))
    return pl.pallas_call(
        _mem_kv_kernel,
        grid=(b,),
        in_specs=[pl.BlockSpec((1, m, d), lambda i: (i, 0, 0))] + _const_specs(consts),
        out_specs=[blk, blk],
        out_shape=[jax.ShapeDtypeStruct((b, m, width), BF16)] * 2,
        compiler_params=pltpu.CompilerParams(dimension_semantics=("parallel",),
                                             vmem_limit_bytes=VMEM_LIMIT),
        name="mem_kv",
    )(mem, *consts)


def _mem_attn_kernel(q_ref, k_ref, v_ref, qn_ref, o_ref):
    for hd in range(X_HEADS):
        sl = slice(hd * X_DH, (hd + 1) * X_DH)
        q = (_norm(q_ref[0, :, sl].astype(F32), qn_ref[...]) * (X_DH ** -0.5)).astype(BF16)
        s = lax.dot_general(q, k_ref[0, :, sl], (((1,), (1,)), ((), ())), preferred_element_type=F32)
        p = jnp.exp(s - jnp.max(s, axis=-1, keepdims=True))
        o = jnp.dot(p.astype(BF16), v_ref[0, :, sl], preferred_element_type=F32)
        o_ref[0, :, sl] = (o / jnp.sum(p, axis=-1, keepdims=True)).astype(o_ref.dtype)


def _mem_attn(x_q, k, v, qn_g, tm=512):
    b, s, width = x_q.shape
    m = k.shape[1]
    kv_spec = pl.BlockSpec((1, m, width), lambda bi, i: (bi, 0, 0))
    return pl.pallas_call(
        _mem_attn_kernel,
        grid=(b, s // tm),
        in_specs=[pl.BlockSpec((1, tm, width), lambda bi, i: (bi, i, 0)), kv_spec, kv_spec,
                  pl.BlockSpec((1, X_DH), lambda bi, i: (0, 0))],
        out_specs=pl.BlockSpec((1, tm, width), lambda bi, i: (bi, i, 0)),
        out_shape=jax.ShapeDtypeStruct((b, s, width), BF16),
        compiler_params=pltpu.CompilerParams(dimension_semantics=("parallel", "parallel"),
                                             vmem_limit_bytes=VMEM_LIMIT),
        name="mem_attn",
    )(x_q, k, v, qn_g.reshape(1, X_DH))


def _merge_kernel(x_ref, gates_ref, oa_ref, ob_ref, oc_ref, wa_ref, wb_ref, wc_ref, wout_ref, h_ref):
    merged = None
    for i, (o_ref, w_ref) in enumerate(((oa_ref, wa_ref), (ob_ref, wb_ref), (oc_ref, wc_ref))):
        y = jnp.dot(o_ref[...], w_ref[...], preferred_element_type=F32)
        gate = jax.nn.sigmoid(gates_ref[:, i * D_MODEL:(i + 1) * D_MODEL].astype(F32))
        merged = gate * y if merged is None else merged + gate * y
    h_ref[...] = x_ref[...] + jnp.dot(merged.astype(BF16), wout_ref[...], preferred_element_type=F32)


def _merge(x2d, gates, o_mla, o_ret, o_mem, w_o_mla, w_o_ret, w_o_cross, w_out, tm=512):
    t, d = x2d.shape
    rows = [x2d, gates, o_mla, o_ret, o_mem]
    consts = [w.astype(BF16) for w in (w_o_mla, w_o_ret, w_o_cross, w_out)]
    return pl.pallas_call(
        _merge_kernel,
        grid=(t // tm,),
        in_specs=_row_specs(rows, tm) + _const_specs(consts),
        out_specs=pl.BlockSpec((tm, d), lambda i: (i, 0)),
        out_shape=jax.ShapeDtypeStruct((t, d), F32),
        compiler_params=pltpu.CompilerParams(dimension_semantics=("parallel",),
                                             vmem_limit_bytes=VMEM_LIMIT),
        name="merge",
    )(*rows, *consts)


PEER_SLOTS = PEER_HEADS * PEER_TOPK
PAIR_CANDS = [(i, j) for i in range(PEER_TOPK) for j in range(PEER_TOPK)
              if (i + 1) * (j + 1) <= PEER_TOPK]
N_CANDS = len(PAIR_CANDS)
CAND_ROWS = -(-N_CANDS // 8) * 8
NEG_INF = float("-inf")


def _top_rounds(s, n_rounds):
    rows = lax.broadcasted_iota(jnp.int32, s.shape, 0)
    vals, idxs = [], []
    for _ in range(n_rounds):
        m = jnp.max(s, axis=0, keepdims=True)
        idx = jnp.min(jnp.where(s == m, rows, s.shape[0]), axis=0, keepdims=True)
        s = jnp.where(rows == idx, NEG_INF, s)
        vals.append(m)
        idxs.append(idx)
    return vals, idxs


def _peer_select_kernel(h_ref, g_ref, wq_ref, keys_ref, xn_ref, exp_ref, gate_ref, q_ref):
    xnb = _norm(h_ref[...], g_ref[...]).astype(BF16)
    xn_ref[...] = xnb
    q_ref[...] = jnp.dot(xnb, wq_ref[...], preferred_element_type=F32).astype(BF16)
    tb = xnb.shape[0]

    def head_body(hd, carry):
        sv, si = [], []
        for c in range(2):
            col = pl.multiple_of(hd * PEER_DQ + c * PEER_DHALF, PEER_DHALF)
            qh = q_ref[:, pl.ds(col, PEER_DHALF)]
            st = lax.dot_general(keys_ref[hd, c], qh, (((1,), (1,)), ((), ())),
                                 preferred_element_type=F32)
            v, i = _top_rounds(st, PEER_TOPK)
            sv.append(v)
            si.append(i)
        pad = CAND_ROWS - N_CANDS
        cand = jnp.concatenate([sv[0][i] + sv[1][j] for i, j in PAIR_CANDS]
                               + [jnp.full((pad, tb), NEG_INF, F32)], axis=0)
        cand_e = jnp.concatenate([si[0][i] * PEER_N_KEYS + si[1][j] for i, j in PAIR_CANDS]
                                 + [jnp.zeros((pad, tb), jnp.int32)], axis=0)
        rows = lax.broadcasted_iota(jnp.int32, cand.shape, 0)
        best, experts = [], []
        for _ in range(PEER_TOPK):
            m = jnp.max(cand, axis=0, keepdims=True)
            pos = jnp.min(jnp.where(cand == m, rows, CAND_ROWS), axis=0, keepdims=True)
            hit = rows == pos
            experts.append(jnp.sum(jnp.where(hit, cand_e, 0), axis=0, keepdims=True))
            cand = jnp.where(hit, NEG_INF, cand)
            best.append(m)
        p = [jnp.exp(b - best[0]) for b in best]
        denom = p[0]
        for pk in p[1:]:
            denom = denom + pk
        row0 = pl.multiple_of(hd * PEER_TOPK, PEER_TOPK)
        exp_ref[pl.ds(row0, PEER_TOPK), :] = jnp.concatenate(experts, axis=0)
        gate_ref[pl.ds(row0, PEER_TOPK), :] = jnp.concatenate([pk / denom for pk in p], axis=0)
        return carry

    lax.fori_loop(0, PEER_HEADS, head_body, 0)


def _peer_select(h2d, g, w_q, sub_keys, tb=128):
    t, d = h2d.shape
    consts = [g.reshape(1, d), w_q.astype(BF16), sub_keys.astype(BF16)]
    return pl.pallas_call(
        _peer_select_kernel,
        grid=(t // tb,),
        in_specs=_row_specs([h2d], tb) + _const_specs(consts),
        out_specs=[
            pl.BlockSpec((tb, d), lambda i: (i, 0)),
            pl.BlockSpec((PEER_SLOTS, tb), lambda i: (0, i)),
            pl.BlockSpec((PEER_SLOTS, tb), lambda i: (0, i)),
        ],
        out_shape=[
            jax.ShapeDtypeStruct((t, d), BF16),
            jax.ShapeDtypeStruct((PEER_SLOTS, t), jnp.int32),
            jax.ShapeDtypeStruct((PEER_SLOTS, t), F32),
        ],
        scratch_shapes=[pltpu.VMEM((tb, PEER_HEADS * PEER_DQ), BF16)],
        compiler_params=pltpu.CompilerParams(dimension_semantics=("parallel",),
                                             vmem_limit_bytes=VMEM_LIMIT),
        name="peer_select",
    )(h2d, *consts)


ROW_TILE = D_MODEL // LANES
PAIR_ROWS = 2 * ROW_TILE


def _pack_expert_table(tab):
    n, d = tab.shape
    t = tab.astype(BF16).reshape(2, n // 2, ROW_TILE, LANES)
    return t.transpose(1, 2, 0, 3).reshape(n // 2, PAIR_ROWS, LANES)


def _peer_u_kernel(idx_ref, xdup_ref, hi_ref, gate_ref, ksum_ref, tab_ref, w_ref, prod_ref):
    tb = xdup_ref.shape[0]

    def token_body(t, carry):
        xd = xdup_ref[t]
        for k in range(PEER_SLOTS):
            prod_ref[t, :, k * LANES:(k + 1) * LANES] = tab_ref[idx_ref[t, k]] * xd
        return carry

    lax.fori_loop(0, tb, token_body, 0)
    prod = prod_ref[...].reshape(tb * PAIR_ROWS, PEER_SLOTS * LANES)
    r = jnp.dot(prod, ksum_ref[...], preferred_element_type=F32)
    r = r.reshape(tb, PAIR_ROWS, PEER_SLOTS)
    odd = (lax.broadcasted_iota(jnp.int32, r.shape, 1) & 1) == 1
    a_hi = jnp.sum(jnp.where(odd, r, 0.0), axis=1)
    a_lo = jnp.sum(jnp.where(odd, 0.0, r), axis=1)
    a = jnp.where(hi_ref[...] == 1, a_hi, a_lo)
    w_ref[...] = gate_ref[...] * jax.nn.gelu(a)


def _peer_u(idx, xdup, hi, gate, tab, tb):
    t = idx.shape[0]
    ksum = jnp.repeat(jnp.eye(PEER_SLOTS, dtype=BF16), LANES, axis=0)
    return pl.pallas_call(
        _peer_u_kernel,
        grid=(t // tb,),
        in_specs=[
            pl.BlockSpec((tb, PEER_SLOTS), lambda i: (i, 0), memory_space=pltpu.SMEM),
            pl.BlockSpec((tb, PAIR_ROWS, LANES), lambda i: (i, 0, 0)),
            pl.BlockSpec((tb, PEER_SLOTS), lambda i: (i, 0)),
            pl.BlockSpec((tb, PEER_SLOTS), lambda i: (i, 0)),
            pl.BlockSpec(ksum.shape, lambda i: (0, 0), pipeline_mode=pl.Buffered(1)),
            pl.BlockSpec(tab.shape, lambda i: (0, 0, 0), pipeline_mode=pl.Buffered(1)),
        ],
        out_specs=pl.BlockSpec((tb, PEER_SLOTS), lambda i: (i, 0)),
        out_shape=jax.ShapeDtypeStruct((t, PEER_SLOTS), F32),
        scratch_shapes=[pltpu.VMEM((tb, PAIR_ROWS, PEER_SLOTS * LANES), BF16)],
        compiler_params=pltpu.CompilerParams(dimension_semantics=("arbitrary",),
                                             vmem_limit_bytes=PEER_VMEM_LIMIT),
        name="peer_u",
    )(idx, xdup, hi, gate, ksum, tab)


def _peer_v_kernel(idx_ref, hi_ref, w_ref, h_ref, expand_ref, tab_ref, o_ref, lhs_ref, g_ref):
    tb = hi_ref.shape[0]
    w_x = jnp.dot(w_ref[...].astype(BF16), expand_ref[...], preferred_element_type=F32)
    hi_x = jnp.dot(hi_ref[...].astype(BF16), expand_ref[...], preferred_element_type=F32)
    shape = (tb, PAIR_ROWS, PEER_SLOTS * PAIR_ROWS)
    row_in_tile = lax.broadcasted_iota(jnp.int32, shape, 2) & (PAIR_ROWS - 1)
    chunk = lax.broadcasted_iota(jnp.int32, shape, 1)
    want = 2 * chunk + hi_x.astype(jnp.int32)[:, None, :]
    lhs_ref[...] = jnp.where(row_in_tile == want, w_x[:, None, :], 0.0).astype(BF16)

    def token_body(t, carry):
        for k in range(PEER_SLOTS):
            g_ref[k * PAIR_ROWS:(k + 1) * PAIR_ROWS, :] = tab_ref[idx_ref[t, k]]
        o = jnp.dot(lhs_ref[t], g_ref[...], preferred_element_type=F32)
        o_ref[t] = h_ref[t] + o[:ROW_TILE]
        return carry

    lax.fori_loop(0, tb, token_body, 0)


def _peer_v(idx, hi, w, h3, tab, tb):
    t = idx.shape[0]
    expand = jnp.repeat(jnp.eye(PEER_SLOTS, dtype=BF16), PAIR_ROWS, axis=1)
    return pl.pallas_call(
        _peer_v_kernel,
        grid=(t // tb,),
        in_specs=[
            pl.BlockSpec((tb, PEER_SLOTS), lambda i: (i, 0), memory_space=pltpu.SMEM),
            pl.BlockSpec((tb, PEER_SLOTS), lambda i: (i, 0)),
            pl.BlockSpec((tb, PEER_SLOTS), lambda i: (i, 0)),
            pl.BlockSpec((tb, ROW_TILE, LANES), lambda i: (i, 0, 0)),
            pl.BlockSpec(expand.shape, lambda i: (0, 0), pipeline_mode=pl.Buffered(1)),
            pl.BlockSpec(tab.shape, lambda i: (0, 0, 0), pipeline_mode=pl.Buffered(1)),
        ],
        out_specs=pl.BlockSpec((tb, ROW_TILE, LANES), lambda i: (i, 0, 0)),
        out_shape=jax.ShapeDtypeStruct((t, ROW_TILE, LANES), F32),
        scratch_shapes=[pltpu.VMEM((tb, PAIR_ROWS, PEER_SLOTS * PAIR_ROWS), BF16),
                        pltpu.VMEM((PEER_SLOTS * PAIR_ROWS, LANES), BF16)],
        compiler_params=pltpu.CompilerParams(dimension_semantics=("arbitrary",),
                                             vmem_limit_bytes=PEER_VMEM_LIMIT),
        name="peer_v",
    )(idx, hi, w, h3, expand, tab)


def _peer_experts(h2d, xn2d, experts, gate, u_tab, v_tab, tb_u=16, tb_v=64):
    t, d = xn2d.shape
    half = u_tab.shape[0] // 2
    hi = (experts >= half).astype(jnp.int32)
    idx = experts - hi * half
    xdup = jnp.repeat(xn2d.astype(BF16).reshape(t, ROW_TILE, LANES), 2, axis=1)
    w = _peer_u(idx, xdup, hi, gate, _pack_expert_table(u_tab), tb_u)
    out = _peer_v(idx, hi, w, h2d.reshape(t, ROW_TILE, LANES), _pack_expert_table(v_tab), tb_v)
    return out.reshape(t, d)


def kernel(x, mem, positions, attn_norm_g, w_in, mla_q_norm_g, w_uq, mla_kv_norm_g, w_ukv,
           mla_qn_g, mla_kn_g, w_o_mla, ret_gn_g, w_o_ret, mem_norm_g, w_mem_kv, x_qn_g, x_kn_g,
           w_o_cross, w_out, ffn_norm_g, peer_w_q, peer_keys, peer_u, peer_v):
    B, S, D = x.shape
    T = B * S
    h2d = x.reshape(T, D)
    offs = [0]
    for sz in IN_SPLITS:
        offs.append(offs[-1] + sz)
    for layer in range(w_in.shape[0]):
        w_pieces = [w_in[layer][:, offs[i]:offs[i + 1]].astype(BF16) for i in range(len(IN_SPLITS))]
        out_dtypes = [F32, F32, F32, F32, F32, BF16, BF16, BF16, BF16]
        c_q, c_kv, k_r, r_q, r_k, r_v, r_g, x_q, gates = _in_proj(h2d, attn_norm_g[layer], w_pieces, out_dtypes)
        o_mla = _mla_branch(c_q, c_kv, k_r, positions, mla_q_norm_g[layer], w_uq[layer],
                            mla_kv_norm_g[layer], w_ukv[layer], mla_qn_g[layer], mla_kn_g[layer], B, S)
        seq = lambda a: a.reshape(B, S, -1)
        o_ret = _retention_branch(seq(r_q), seq(r_k), seq(r_v), seq(r_g), positions, ret_gn_g[layer])
        mk, mv = _mem_kv(mem, mem_norm_g[layer], w_mem_kv[layer], x_kn_g[layer])
        o_mem = _mem_attn(seq(x_q), mk, mv, x_qn_g[layer])
        h2d = _merge(h2d, gates, o_mla, o_ret.reshape(T, -1), o_mem.reshape(T, -1),
                     w_o_mla[layer], w_o_ret[layer], w_o_cross[layer], w_out[layer])
        xn2d, experts_t, gate_t = _peer_select(h2d, ffn_norm_g[layer], peer_w_q[layer], peer_keys[layer])
        h2d = _peer_experts(h2d, xn2d, experts_t.T, gate_t.T, peer_u[layer], peer_v[layer])
    return h2d.reshape(B, S, D)
```

```python
import functools

import jax
import jax.numpy as jnp
from jax import lax
from jax.experimental import pallas as pl
from jax.experimental.pallas import tpu as pltpu

D_MODEL = 1024
MEM_LEN = 256
MLA_HEADS = 8
MLA_Q_RANK = 384
MLA_KV_RANK = 128
MLA_NOPE = 64
MLA_ROPE = 32
MLA_V = 64
RET_HEADS = 4
RET_DK = 64
RET_DV = 128
RET_CHUNK = 128
X_HEADS = 4
X_DH = 128
N_BRANCH = 3
PEER_HEADS = 8
PEER_N_KEYS = 128
PEER_TOPK = 16
PEER_DQ = 256
PEER_DHALF = PEER_DQ // 2
ROPE_BASE = 10000.0
EPS = 1e-6
IN_SPLITS = (MLA_Q_RANK, MLA_KV_RANK, MLA_ROPE,
             RET_HEADS * RET_DK, RET_HEADS * RET_DK, RET_HEADS * RET_DV, RET_HEADS * RET_DV,
             X_HEADS * X_DH, N_BRANCH * D_MODEL)

LANES = 128
VMEM_LIMIT = 48 * 1024 * 1024
PEER_VMEM_LIMIT = 56 * 1024 * 1024
BF16 = jnp.bfloat16
F32 = jnp.float32


def _norm(x, g):
    return x * lax.rsqrt(jnp.mean(x * x, axis=-1, keepdims=True) + EPS) * g


def _row_specs(arrays, tm):
    return [pl.BlockSpec((tm, a.shape[1]), lambda i: (i, 0)) for a in arrays]


def _const_specs(arrays):
    return [pl.BlockSpec(a.shape, lambda i, n=a.ndim: (0,) * n) for a in arrays]


def _in_proj_kernel(x_ref, g_ref, *refs):
    n_w = len(refs) // 2
    w_refs, o_refs = refs[:n_w], refs[n_w:]
    nb = _norm(x_ref[...], g_ref[...]).astype(BF16)
    for w_ref, o_ref in zip(w_refs, o_refs):
        o_ref[...] = jnp.dot(nb, w_ref[...], preferred_element_type=F32).astype(o_ref.dtype)


def _in_proj(x2d, g, w_pieces, out_dtypes, tm=512):
    t, d = x2d.shape
    consts = [g.reshape(1, d)] + list(w_pieces)
    return pl.pallas_call(
        _in_proj_kernel,
        grid=(t // tm,),
        in_specs=_row_specs([x2d], tm) + _const_specs(consts),
        out_specs=[pl.BlockSpec((tm, w.shape[1]), lambda i: (i, 0)) for w in w_pieces],
        out_shape=[jax.ShapeDtypeStruct((t, w.shape[1]), dt) for w, dt in zip(w_pieces, out_dtypes)],
        compiler_params=pltpu.CompilerParams(dimension_semantics=("parallel",),
                                             vmem_limit_bytes=VMEM_LIMIT),
        name="in_proj",
    )(x2d, *consts)


def _rope_patterns(positions, n_rope, lead, width):
    inv_freq = ROPE_BASE ** (-jnp.arange(0, n_rope, 2, dtype=F32) / n_rope)
    ang = positions.astype(F32).reshape(-1, 1) * inv_freq
    cos, sin = jnp.cos(ang), jnp.sin(ang)
    t = ang.shape[0]
    tail = width - lead - n_rope
    cosp = jnp.concatenate([jnp.ones((t, lead)), cos, cos, jnp.zeros((t, tail))], axis=1)
    sinp = jnp.concatenate([jnp.zeros((t, lead)), -sin, sin, jnp.zeros((t, tail))], axis=1)
    rep = LANES // width
    return jnp.tile(cosp, (1, rep)), jnp.tile(sinp, (1, rep))


def _rotate_half(x, cosp, sinp, n_rope, lead, width):
    half = n_rope // 2
    lane = lax.broadcasted_iota(jnp.int32, x.shape, 1) & (width - 1)
    partner = jnp.where(lane < lead + half, pltpu.roll(x, LANES - half, 1), pltpu.roll(x, half, 1))
    return x * cosp + partner * sinp


MLA_QK = MLA_NOPE + MLA_ROPE


def _pad_head_cols(w, per_head):
    r = w.shape[0]
    w = w.reshape(r, MLA_HEADS, per_head)
    return jnp.pad(w, ((0, 0), (0, 0), (0, LANES - per_head))).reshape(r, MLA_HEADS * LANES)


def _mla_prep_kernel(cq_ref, ckv_ref, kr_ref, cos_ref, sin_ref, gq_ref, gkv_ref, wuq_ref, wuk_ref,
                     wuv_ref, qn_ref, kn_ref, q_ref, k_ref, v_ref):
    cqn = _norm(cq_ref[...], gq_ref[...]).astype(BF16)
    ckvn = _norm(ckv_ref[...], gkv_ref[...]).astype(BF16)
    qp = jnp.dot(cqn, wuq_ref[...], preferred_element_type=F32)
    kp = jnp.dot(ckvn, wuk_ref[...], preferred_element_type=F32)
    v_ref[...] = jnp.dot(ckvn, wuv_ref[...], preferred_element_type=F32).astype(v_ref.dtype)
    tm = qp.shape[0]
    kr = jnp.concatenate([jnp.zeros((tm, MLA_NOPE), F32), kr_ref[...],
                          jnp.zeros((tm, LANES - MLA_QK), F32)], axis=1)
    cosp, sinp = cos_ref[...], sin_ref[...]

    def qk_norm_rope(xh, g, scale):
        ss = jnp.sum(xh * xh, axis=-1, keepdims=True) * (1.0 / MLA_QK)
        xn = xh * lax.rsqrt(ss + EPS) * g
        return _rotate_half(xn, cosp, sinp, MLA_ROPE, MLA_NOPE, LANES) * scale

    for hd in range(MLA_HEADS):
        sl = slice(hd * LANES, (hd + 1) * LANES)
        q_ref[:, sl] = qk_norm_rope(qp[:, sl], qn_ref[...], MLA_QK ** -0.5).astype(q_ref.dtype)
        k_ref[:, sl] = qk_norm_rope(kp[:, sl] + kr, kn_ref[...], 1.0).astype(k_ref.dtype)


def _mla_prep(c_q, c_kv, k_r, cosp, sinp, q_norm_g, w_uq, kv_norm_g, w_ukv, qn_g, kn_g, tm=512):
    t = c_q.shape[0]
    w_ukv3 = w_ukv.reshape(MLA_KV_RANK, MLA_HEADS, MLA_NOPE + MLA_V)
    w_uk = _pad_head_cols(w_ukv3[:, :, :MLA_NOPE].reshape(MLA_KV_RANK, -1), MLA_NOPE).astype(BF16)
    w_uv = w_ukv3[:, :, MLA_NOPE:].reshape(MLA_KV_RANK, MLA_HEADS * MLA_V).astype(BF16)
    w_uqp = _pad_head_cols(w_uq, MLA_QK).astype(BF16)
    pad_g = lambda g: jnp.pad(g, (0, LANES - MLA_QK)).reshape(1, LANES)
    consts = [q_norm_g.reshape(1, -1), kv_norm_g.reshape(1, -1), w_uqp, w_uk, w_uv, pad_g(qn_g), pad_g(kn_g)]
    rows = [c_q, c_kv, k_r, cosp, sinp]
    wide = MLA_HEADS * LANES
    return pl.pallas_call(
        _mla_prep_kernel,
        grid=(t // tm,),
        in_specs=_row_specs(rows, tm) + _const_specs(consts),
        out_specs=[pl.BlockSpec((tm, wide), lambda i: (i, 0)),
                   pl.BlockSpec((tm, wide), lambda i: (i, 0)),
                   pl.BlockSpec((tm, MLA_HEADS * MLA_V), lambda i: (i, 0))],
        out_shape=[jax.ShapeDtypeStruct((t, wide), BF16), jax.ShapeDtypeStruct((t, wide), BF16),
                   jax.ShapeDtypeStruct((t, MLA_HEADS * MLA_V), BF16)],
        compiler_params=pltpu.CompilerParams(dimension_semantics=("parallel",),
                                             vmem_limit_bytes=VMEM_LIMIT),
        name="mla_prep",
    )(*rows, *consts)


def _mla_attn_kernel(q_ref, k_ref, v_ref, o_ref, *, tile):
    qi = pl.program_id(2)
    row = lax.broadcasted_iota(jnp.int32, (tile, tile), 0)
    col = lax.broadcasted_iota(jnp.int32, (tile, tile), 1)
    outs = []
    for a in range(2):
        lanes = slice(a * LANES, (a + 1) * LANES)
        q = q_ref[0, :, lanes]

        def step(j, carry, masked):
            m, l, acc = carry
            start = pl.multiple_of(j * tile, tile)
            kj = k_ref[0, pl.ds(start, tile), lanes]
            s = lax.dot_general(q, kj, (((1,), (1,)), ((), ())), preferred_element_type=F32)
            if masked:
                s = jnp.where(col <= row, s, -1e30)
            m_new = jnp.maximum(m, jnp.max(s, axis=-1, keepdims=True))
            alpha = jnp.exp(m - m_new)
            p = jnp.exp(s - m_new)
            l = alpha * l + jnp.sum(p, axis=-1, keepdims=True)
            pv = jnp.dot(p.astype(BF16), v_ref[0, pl.ds(start, tile), :], preferred_element_type=F32)
            return m_new, l, alpha * acc + pv

        init = (jnp.full((tile, 1), -1e30, F32), jnp.zeros((tile, 1), F32),
                jnp.zeros((tile, 2 * MLA_V), F32))
        carry = lax.fori_loop(0, qi, functools.partial(step, masked=False), init)
        m, l, acc = step(qi, carry, True)
        outs.append(acc / l)
    lane = lax.broadcasted_iota(jnp.int32, outs[0].shape, 1)
    o_ref[0] = jnp.where(lane < MLA_V, outs[0], outs[1]).astype(o_ref.dtype)


def _mla_attn(q, k, v, tile=512):
    b, s, _ = q.shape
    pair = 2 * LANES
    return pl.pallas_call(
        functools.partial(_mla_attn_kernel, tile=tile),
        grid=(b, MLA_HEADS // 2, s // tile),
        in_specs=[pl.BlockSpec((1, tile, pair), lambda bi, hp, qi: (bi, qi, hp)),
                  pl.BlockSpec((1, s, pair), lambda bi, hp, qi: (bi, 0, hp)),
                  pl.BlockSpec((1, s, 2 * MLA_V), lambda bi, hp, qi: (bi, 0, hp))],
        out_specs=pl.BlockSpec((1, tile, 2 * MLA_V), lambda bi, hp, qi: (bi, qi, hp)),
        out_shape=jax.ShapeDtypeStruct((b, s, MLA_HEADS * MLA_V), BF16),
        compiler_params=pltpu.CompilerParams(dimension_semantics=("parallel", "parallel", "arbitrary"),
                                             vmem_limit_bytes=VMEM_LIMIT),
        name="mla_attn",
    )(q, k, v)


def _mla_branch(c_q, c_kv, k_r, positions, q_norm_g, w_uq, kv_norm_g, w_ukv, qn_g, kn_g, b, s):
    cosp, sinp = _rope_patterns(positions, MLA_ROPE, MLA_NOPE, LANES)
    q, k, v = _mla_prep(c_q, c_kv, k_r, cosp, sinp, q_norm_g, w_uq, kv_norm_g, w_ukv, qn_g, kn_g)
    o = _mla_attn(q.reshape(b, s, -1), k.reshape(b, s, -1), v.reshape(b, s, -1))
    return o.reshape(b * s, -1)


def _retention_kernel(q_ref, k_ref, v_ref, g_ref, cos_ref, sin_ref, decay_ref, xi_ref, zeta_ref,
                      cd_ref, gn_ref, o_ref):
    L = RET_CHUNK
    n_chunks = q_ref.shape[1] // L
    lane = lax.broadcasted_iota(jnp.int32, (L, LANES), 1)

    def chunk_body(c, states):
        rows = pl.ds(pl.multiple_of(c * L, L), L)
        cosp, sinp = cos_ref[0, rows, :], sin_ref[0, rows, :]
        qc = _rotate_half(q_ref[0, rows, :], cosp, sinp, RET_DK, 0, RET_DK)
        kc = _rotate_half(k_ref[0, rows, :], cosp, sinp, RET_DK, 0, RET_DK) * (RET_DK ** -0.5)
        qx = qc * xi_ref[0]
        kz = kc * zeta_ref[0]
        new_states = []
        for a in range(2):
            mine = (lane >= a * RET_DK) & (lane < (a + 1) * RET_DK)
            qa = jnp.where(mine, qc, 0.0).astype(BF16)
            ka = jnp.where(mine, kc, 0.0).astype(BF16)
            va = v_ref[0, rows, a * RET_DV:(a + 1) * RET_DV]
            inner = lax.dot_general(qa, ka, (((1,), (1,)), ((), ())),
                                    preferred_element_type=F32) * decay_ref[a]
            cross = jnp.dot(jnp.where(mine, qx, 0.0).astype(BF16), states[a].astype(BF16),
                            preferred_element_type=F32)
            o = jnp.dot(inner.astype(BF16), va, preferred_element_type=F32) + cross
            kzt = jnp.where(mine, kz, 0.0).T.astype(BF16)
            incr = jnp.dot(kzt, va, preferred_element_type=F32)
            new_states.append(cd_ref[a, 0:1, :] * states[a] + incr)
            o = _norm(o, gn_ref[a])
            gate = g_ref[0, rows, a * RET_DV:(a + 1) * RET_DV].astype(F32)
            o_ref[0, rows, a * RET_DV:(a + 1) * RET_DV] = (o * gate * jax.nn.sigmoid(gate)).astype(o_ref.dtype)
        return tuple(new_states)

    zero = jnp.zeros((LANES, RET_DV), F32)
    lax.fori_loop(0, n_chunks, chunk_body, (zero, zero))


def _retention_branch(r_q, r_k, r_v, r_g, positions, gn_g):
    b, s, _ = r_q.shape
    L = RET_CHUNK
    cosp, sinp = _rope_patterns(positions, RET_DK, 0, RET_DK)
    cosp, sinp = cosp.reshape(b, s, LANES), sinp.reshape(b, s, LANES)
    log_gamma = jnp.log(1.0 - 2.0 ** (-5.0 - jnp.arange(RET_HEADS, dtype=F32)))
    idx = jnp.arange(L, dtype=F32)
    diff = idx[:, None] - idx[None, :]
    decay = jnp.where(diff >= 0, jnp.exp(log_gamma[:, None, None] * jnp.maximum(diff, 0.0)), 0.0)
    xi = jnp.exp(log_gamma[:, None] * (idx + 1.0))
    zeta = jnp.exp(log_gamma[:, None] * (L - 1.0 - idx))
    per_pair = lambda a: jnp.repeat(a.reshape(RET_HEADS // 2, 2, L), RET_DK, axis=1).transpose(0, 2, 1)
    cd = jnp.broadcast_to(jnp.exp(log_gamma * L)[:, None, None], (RET_HEADS, 8, RET_DV))
    gn = gn_g.reshape(RET_HEADS, 1, RET_DV)
    n_pairs = RET_HEADS // 2
    seq = lambda width: pl.BlockSpec((1, s, width), lambda bi, hp: (bi, 0, hp))
    return pl.pallas_call(
        _retention_kernel,
        grid=(b, n_pairs),
        in_specs=[seq(LANES), seq(LANES), seq(2 * RET_DV), seq(2 * RET_DV),
                  pl.BlockSpec((1, s, LANES), lambda bi, hp: (bi, 0, 0)),
                  pl.BlockSpec((1, s, LANES), lambda bi, hp: (bi, 0, 0)),
                  pl.BlockSpec((2, L, L), lambda bi, hp: (hp, 0, 0)),
                  pl.BlockSpec((1, L, LANES), lambda bi, hp: (hp, 0, 0)),
                  pl.BlockSpec((1, L, LANES), lambda bi, hp: (hp, 0, 0)),
                  pl.BlockSpec((2, 8, RET_DV), lambda bi, hp: (hp, 0, 0)),
                  pl.BlockSpec((2, 1, RET_DV), lambda bi, hp: (hp, 0, 0))],
        out_specs=seq(2 * RET_DV),
        out_shape=jax.ShapeDtypeStruct((b, s, RET_HEADS * RET_DV), BF16),
        compiler_params=pltpu.CompilerParams(dimension_semantics=("parallel", "parallel"),
                                             vmem_limit_bytes=VMEM_LIMIT),
        name="retention",
    )(r_q, r_k, r_v, r_g, cosp, sinp, decay, per_pair(xi), per_pair(zeta), cd, gn)


def _mem_kv_kernel(mem_ref, g_ref, w_ref, kn_ref, k_ref, v_ref):
    kv = jnp.dot(_norm(mem_ref[0], g_ref[...]).astype(BF16), w_ref[...], preferred_element_type=F32)
    width = X_HEADS * X_DH
    for hd in range(X_HEADS):
        sl = slice(hd * X_DH, (hd + 1) * X_DH)
        k_ref[0, :, sl] = _norm(kv[:, sl], kn_ref[...]).astype(k_ref.dtype)
    v_ref[0] = kv[:, width:].astype(v_ref.dtype)


def _mem_kv(mem, mem_g, w_mem_kv, kn_g):
    b, m, d = mem.shape
    width = X_HEADS * X_DH
    w4 = w_mem_kv.reshape(d, X_HEADS, 2, X_DH)
    w = jnp.concatenate([w4[:, :, 0].reshape(d, width), w4[:, :, 1].reshape(d, width)], axis=1).astype(BF16)
    consts = [mem_g.reshape(1, d), w, kn_g.reshape(1, X_DH)]
    blk = pl.BlockSpec((1, m, width), lambda i: (i, 0, 0))
    return pl.pallas_call(
        _mem_kv_kernel,
        grid=(b,),
        in_specs=[pl.BlockSpec((1, m, d), lambda i: (i, 0, 0))] + _const_specs(consts),
        out_specs=[blk, blk],
        out_shape=[jax.ShapeDtypeStruct((b, m, width), BF16)] * 2,
        compiler_params=pltpu.CompilerParams(dimension_semantics=("parallel",),
                                             vmem_limit_bytes=VMEM_LIMIT),
        name="mem_kv",
    )(mem, *consts)


def _mem_attn_kernel(q_ref, k_ref, v_ref, qn_ref, o_ref):
    for hd in range(X_HEADS):
        sl = slice(hd * X_DH, (hd + 1) * X_DH)
        q = (_norm(q_ref[0, :, sl].astype(F32), qn_ref[...]) * (X_DH ** -0.5)).astype(BF16)
        s = lax.dot_general(q, k_ref[0, :, sl], (((1,), (1,)), ((), ())), preferred_element_type=F32)
        p = jnp.exp(s - jnp.max(s, axis=-1, keepdims=True))
        o = jnp.dot(p.astype(BF16), v_ref[0, :, sl], preferred_element_type=F32)
        o_ref[0, :, sl] = (o / jnp.sum(p, axis=-1, keepdims=True)).astype(o_ref.dtype)


def _mem_attn(x_q, k, v, qn_g, tm=512):
    b, s, width = x_q.shape
    m = k.shape[1]
    kv_spec = pl.BlockSpec((1, m, width), lambda bi, i: (bi, 0, 0))
    return pl.pallas_call(
        _mem_attn_kernel,
        grid=(b, s // tm),
        in_specs=[pl.BlockSpec((1, tm, width), lambda bi, i: (bi, i, 0)), kv_spec, kv_spec,
                  pl.BlockSpec((1, X_DH), lambda bi, i: (0, 0))],
        out_specs=pl.BlockSpec((1, tm, width), lambda bi, i: (bi, i, 0)),
        out_shape=jax.ShapeDtypeStruct((b, s, width), BF16),
        compiler_params=pltpu.CompilerParams(dimension_semantics=("parallel", "parallel"),
                                             vmem_limit_bytes=VMEM_LIMIT),
        name="mem_attn",
    )(x_q, k, v, qn_g.reshape(1, X_DH))


def _merge_kernel(x_ref, gates_ref, oa_ref, ob_ref, oc_ref, wa_ref, wb_ref, wc_ref, wout_ref, h_ref):
    merged = None
    for i, (o_ref, w_ref) in enumerate(((oa_ref, wa_ref), (ob_ref, wb_ref), (oc_ref, wc_ref))):
        y = jnp.dot(o_ref[...], w_ref[...], preferred_element_type=F32)
        gate = jax.nn.sigmoid(gates_ref[:, i * D_MODEL:(i + 1) * D_MODEL].astype(F32))
        merged = gate * y if merged is None else merged + gate * y
    h_ref[...] = x_ref[...] + jnp.dot(merged.astype(BF16), wout_ref[...], preferred_element_type=F32)


def _merge(x2d, gates, o_mla, o_ret, o_mem, w_o_mla, w_o_ret, w_o_cross, w_out, tm=512):
    t, d = x2d.shape
    rows = [x2d, gates, o_mla, o_ret, o_mem]
    consts = [w.astype(BF16) for w in (w_o_mla, w_o_ret, w_o_cross, w_out)]
    return pl.pallas_call(
        _merge_kernel,
        grid=(t // tm,),
        in_specs=_row_specs(rows, tm) + _const_specs(consts),
        out_specs=pl.BlockSpec((tm, d), lambda i: (i, 0)),
        out_shape=jax.ShapeDtypeStruct((t, d), F32),
        compiler_params=pltpu.CompilerParams(dimension_semantics=("parallel",),
                                             vmem_limit_bytes=VMEM_LIMIT),
        name="merge",
    )(*rows, *consts)


PEER_SLOTS = PEER_HEADS * PEER_TOPK
PAIR_CANDS = [(i, j) for i in range(PEER_TOPK) for j in range(PEER_TOPK)
              if (i + 1) * (j + 1) <= PEER_TOPK]
N_CANDS = len(PAIR_CANDS)
CAND_ROWS = -(-N_CANDS // 8) * 8
NEG_INF = float("-inf")


def _top_rounds(s, n_rounds):
    rows = lax.broadcasted_iota(jnp.int32, s.shape, 0)
    vals, idxs = [], []
    for _ in range(n_rounds):
        m = jnp.max(s, axis=0, keepdims=True)
        idx = jnp.min(jnp.where(s == m, rows, s.shape[0]), axis=0, keepdims=True)
        s = jnp.where(rows == idx, NEG_INF, s)
        vals.append(m)
        idxs.append(idx)
    return vals, idxs


def _peer_select_kernel(h_ref, g_ref, wq_ref, keys_ref, xn_ref, exp_ref, gate_ref, q_ref):
    xnb = _norm(h_ref[...], g_ref[...]).astype(BF16)
    xn_ref[...] = xnb
    q_ref[...] = jnp.dot(xnb, wq_ref[...], preferred_element_type=F32).astype(BF16)
    tb = xnb.shape[0]

    def head_body(hd, carry):
        sv, si = [], []
        for c in range(2):
            col = pl.multiple_of(hd * PEER_DQ + c * PEER_DHALF, PEER_DHALF)
            qh = q_ref[:, pl.ds(col, PEER_DHALF)]
            st = lax.dot_general(keys_ref[hd, c], qh, (((1,), (1,)), ((), ())),
                                 preferred_element_type=F32)
            v, i = _top_rounds(st, PEER_TOPK)
            sv.append(v)
            si.append(i)
        pad = CAND_ROWS - N_CANDS
        cand = jnp.concatenate([sv[0][i] + sv[1][j] for i, j in PAIR_CANDS]
                               + [jnp.full((pad, tb), NEG_INF, F32)], axis=0)
        cand_e = jnp.concatenate([si[0][i] * PEER_N_KEYS + si[1][j] for i, j in PAIR_CANDS]
                                 + [jnp.zeros((pad, tb), jnp.int32)], axis=0)
        rows = lax.broadcasted_iota(jnp.int32, cand.shape, 0)
        best, experts = [], []
        for _ in range(PEER_TOPK):
            m = jnp.max(cand, axis=0, keepdims=True)
            pos = jnp.min(jnp.where(cand == m, rows, CAND_ROWS), axis=0, keepdims=True)
            hit = rows == pos
            experts.append(jnp.sum(jnp.where(hit, cand_e, 0), axis=0, keepdims=True))
            cand = jnp.where(hit, NEG_INF, cand)
            best.append(m)
        p = [jnp.exp(b - best[0]) for b in best]
        denom = p[0]
        for pk in p[1:]:
            denom = denom + pk
        row0 = pl.multiple_of(hd * PEER_TOPK, PEER_TOPK)
        exp_ref[pl.ds(row0, PEER_TOPK), :] = jnp.concatenate(experts, axis=0)
        gate_ref[pl.ds(row0, PEER_TOPK), :] = jnp.concatenate([pk / denom for pk in p], axis=0)
        return carry

    lax.fori_loop(0, PEER_HEADS, head_body, 0)


def _peer_select(h2d, g, w_q, sub_keys, tb=128):
    t, d = h2d.shape
    consts = [g.reshape(1, d), w_q.astype(BF16), sub_keys.astype(BF16)]
    return pl.pallas_call(
        _peer_select_kernel,
        grid=(t // tb,),
        in_specs=_row_specs([h2d], tb) + _const_specs(consts),
        out_specs=[
            pl.BlockSpec((tb, d), lambda i: (i, 0)),
            pl.BlockSpec((PEER_SLOTS, tb), lambda i: (0, i)),
            pl.BlockSpec((PEER_SLOTS, tb), lambda i: (0, i)),
        ],
        out_shape=[
            jax.ShapeDtypeStruct((t, d), BF16),
            jax.ShapeDtypeStruct((PEER_SLOTS, t), jnp.int32),
            jax.ShapeDtypeStruct((PEER_SLOTS, t), F32),
        ],
        scratch_shapes=[pltpu.VMEM((tb, PEER_HEADS * PEER_DQ), BF16)],
        compiler_params=pltpu.CompilerParams(dimension_semantics=("parallel",),
                                             vmem_limit_bytes=VMEM_LIMIT),
        name="peer_select",
    )(h2d, *consts)


ROW_TILE = D_MODEL // LANES
PAIR_ROWS = 2 * ROW_TILE


def _pack_expert_table(tab):
    n, d = tab.shape
    t = tab.astype(BF16).reshape(2, n // 2, ROW_TILE, LANES)
    bits = lax.bitcast_convert_type(t, jnp.uint16).astype(jnp.uint32)
    return (bits[0] | (bits[1] << 16)).reshape(n // 2 * ROW_TILE, LANES)


TOKENS_PER_ITER = 8


def _gather_tiles(idx_ref, tab_ref, g_ref, t):
    for k in range(PEER_SLOTS):
        start = pl.multiple_of(idx_ref[t, k], ROW_TILE)
        g_ref[k * ROW_TILE:(k + 1) * ROW_TILE, :] = tab_ref[pl.ds(start, ROW_TILE), :]


def _peer_u_kernel(idx_ref, x_ref, hi_ref, gate_ref, fold_ref, tab_ref, w_ref, sel_ref, *g_refs):
    tb = x_ref.shape[0]
    shape = (ROW_TILE, PEER_SLOTS * PAIR_ROWS)
    row_in_tile = lax.broadcasted_iota(jnp.int32, shape, 1) & (PAIR_ROWS - 1)
    chunk = lax.broadcasted_iota(jnp.int32, shape, 0)
    keep = (row_in_tile >> 1) == chunk

    def token_pair(i, carry):
        for j, g_ref in enumerate(g_refs):
            t = i * TOKENS_PER_ITER + j
            _gather_tiles(idx_ref, tab_ref, g_ref, t)
            out = lax.dot_general(x_ref[t], pltpu.bitcast(g_ref[...], BF16), (((1,), (1,)), ((), ())),
                                  preferred_element_type=F32)
            sel_ref[pl.ds(pl.multiple_of(t * ROW_TILE, ROW_TILE), ROW_TILE), :] = (
                jnp.where(keep, out[:ROW_TILE], 0.0))
        return carry

    lax.fori_loop(0, tb // TOKENS_PER_ITER, token_pair, 0)
    sel = sel_ref[...]
    sel_hi = sel.astype(BF16)
    sel_lo = (sel - sel_hi.astype(F32)).astype(BF16)
    r = (jnp.dot(sel_hi, fold_ref[...], preferred_element_type=F32)
         + jnp.dot(sel_lo, fold_ref[...], preferred_element_type=F32))
    r = jnp.sum(r.reshape(tb, ROW_TILE, 2 * PEER_SLOTS), axis=1)
    a = jnp.where(hi_ref[...] == 1, r[:, PEER_SLOTS:], r[:, :PEER_SLOTS])
    w_ref[...] = gate_ref[...] * jax.nn.gelu(a)


def _peer_u(idx, x3, hi, gate, tab, tb):
    t = idx.shape[0]
    eye = jnp.eye(PEER_SLOTS, dtype=BF16)
    parity = (jnp.arange(PAIR_ROWS) & 1).astype(BF16)
    fold = jnp.concatenate([jnp.kron(eye, (1 - parity)[:, None]), jnp.kron(eye, parity[:, None])], axis=1)
    g_shape = pltpu.VMEM((PEER_SLOTS * ROW_TILE, LANES), jnp.uint32)
    return pl.pallas_call(
        _peer_u_kernel,
        grid=(t // tb,),
        in_specs=[
            pl.BlockSpec((tb, PEER_SLOTS), lambda i: (i, 0), memory_space=pltpu.SMEM),
            pl.BlockSpec((tb, PAIR_ROWS, LANES), lambda i: (i, 0, 0)),
            pl.BlockSpec((tb, PEER_SLOTS), lambda i: (i, 0)),
            pl.BlockSpec((tb, PEER_SLOTS), lambda i: (i, 0)),
            pl.BlockSpec(fold.shape, lambda i: (0, 0), pipeline_mode=pl.Buffered(1)),
            pl.BlockSpec(tab.shape, lambda i: (0, 0), pipeline_mode=pl.Buffered(1)),
        ],
        out_specs=pl.BlockSpec((tb, PEER_SLOTS), lambda i: (i, 0)),
        out_shape=jax.ShapeDtypeStruct((t, PEER_SLOTS), F32),
        scratch_shapes=[pltpu.VMEM((tb * ROW_TILE, PEER_SLOTS * PAIR_ROWS), F32)]
        + [g_shape] * TOKENS_PER_ITER,
        compiler_params=pltpu.CompilerParams(dimension_semantics=("arbitrary",),
                                             vmem_limit_bytes=PEER_VMEM_LIMIT),
        name="peer_u",
    )(idx, x3, hi, gate, fold, tab)


def _peer_v_kernel(idx_ref, hi_ref, w_ref, h_ref, expand_ref, tab_ref, o_ref, lhs_ref, *g_refs):
    tb = hi_ref.shape[0]
    w_x = jnp.dot(w_ref[...].astype(BF16), expand_ref[...], preferred_element_type=F32)
    hi_x = jnp.dot(hi_ref[...].astype(BF16), expand_ref[...], preferred_element_type=F32)
    shape = (tb, PAIR_ROWS, PEER_SLOTS * PAIR_ROWS)
    row_in_tile = lax.broadcasted_iota(jnp.int32, shape, 2) & (PAIR_ROWS - 1)
    chunk = lax.broadcasted_iota(jnp.int32, shape, 1)
    want = 2 * chunk + hi_x.astype(jnp.int32)[:, None, :]
    lhs_ref[...] = jnp.where(row_in_tile == want, w_x[:, None, :], 0.0).astype(BF16)

    def token_pair(i, carry):
        for j, g_ref in enumerate(g_refs):
            t = i * TOKENS_PER_ITER + j
            _gather_tiles(idx_ref, tab_ref, g_ref, t)
            o = jnp.dot(lhs_ref[t], pltpu.bitcast(g_ref[...], BF16), preferred_element_type=F32)
            o_ref[t] = h_ref[t] + o[:ROW_TILE]
        return carry

    lax.fori_loop(0, tb // TOKENS_PER_ITER, token_pair, 0)


def _peer_v(idx, hi, w, h3, tab, tb):
    t = idx.shape[0]
    expand = jnp.repeat(jnp.eye(PEER_SLOTS, dtype=BF16), PAIR_ROWS, axis=1)
    return pl.pallas_call(
        _peer_v_kernel,
        grid=(t // tb,),
        in_specs=[
            pl.BlockSpec((tb, PEER_SLOTS), lambda i: (i, 0), memory_space=pltpu.SMEM),
            pl.BlockSpec((tb, PEER_SLOTS), lambda i: (i, 0)),
            pl.BlockSpec((tb, PEER_SLOTS), lambda i: (i, 0)),
            pl.BlockSpec((tb, ROW_TILE, LANES), lambda i: (i, 0, 0)),
            pl.BlockSpec(expand.shape, lambda i: (0, 0), pipeline_mode=pl.Buffered(1)),
            pl.BlockSpec(tab.shape, lambda i: (0, 0), pipeline_mode=pl.Buffered(1)),
        ],
        out_specs=pl.BlockSpec((tb, ROW_TILE, LANES), lambda i: (i, 0, 0)),
        out_shape=jax.ShapeDtypeStruct((t, ROW_TILE, LANES), F32),
        scratch_shapes=[pltpu.VMEM((tb, PAIR_ROWS, PEER_SLOTS * PAIR_ROWS), BF16)]
        + [pltpu.VMEM((PEER_SLOTS * ROW_TILE, LANES), jnp.uint32)] * TOKENS_PER_ITER,
        compiler_params=pltpu.CompilerParams(dimension_semantics=("arbitrary",),
                                             vmem_limit_bytes=PEER_VMEM_LIMIT),
        name="peer_v",
    )(idx, hi, w, h3, expand, tab)


def _peer_experts(h2d, xn2d, experts, gate, u_tab, v_tab, tb_u=64, tb_v=64):
    t, d = xn2d.shape
    half = u_tab.shape[0] // 2
    hi = (experts >= half).astype(jnp.int32)
    idx = (experts - hi * half) * ROW_TILE
    x3 = jnp.pad(xn2d.astype(BF16).reshape(t, ROW_TILE, LANES), ((0, 0), (0, ROW_TILE), (0, 0)))
    w = _peer_u(idx, x3, hi, gate, _pack_expert_table(u_tab), tb_u)
    out = _peer_v(idx, hi, w, h2d.reshape(t, ROW_TILE, LANES), _pack_expert_table(v_tab), tb_v)
    return out.reshape(t, d)


def kernel(x, mem, positions, attn_norm_g, w_in, mla_q_norm_g, w_uq, mla_kv_norm_g, w_ukv,
           mla_qn_g, mla_kn_g, w_o_mla, ret_gn_g, w_o_ret, mem_norm_g, w_mem_kv, x_qn_g, x_kn_g,
           w_o_cross, w_out, ffn_norm_g, peer_w_q, peer_keys, peer_u, peer_v):
    B, S, D = x.shape
    T = B * S
    h2d = x.reshape(T, D)
    offs = [0]
    for sz in IN_SPLITS:
        offs.append(offs[-1] + sz)
    for layer in range(w_in.shape[0]):
        w_pieces = [w_in[layer][:, offs[i]:offs[i + 1]].astype(BF16) for i in range(len(IN_SPLITS))]
        out_dtypes = [F32, F32, F32, F32, F32, BF16, BF16, BF16, BF16]
        c_q, c_kv, k_r, r_q, r_k, r_v, r_g, x_q, gates = _in_proj(h2d, attn_norm_g[layer], w_pieces, out_dtypes)
        o_mla = _mla_branch(c_q, c_kv, k_r, positions, mla_q_norm_g[layer], w_uq[layer],
                            mla_kv_norm_g[layer], w_ukv[layer], mla_qn_g[layer], mla_kn_g[layer], B, S)
        seq = lambda a: a.reshape(B, S, -1)
        o_ret = _retention_branch(seq(r_q), seq(r_k), seq(r_v), seq(r_g), positions, ret_gn_g[layer])
        mk, mv = _mem_kv(mem, mem_norm_g[layer], w_mem_kv[layer], x_kn_g[layer])
        o_mem = _mem_attn(seq(x_q), mk, mv, x_qn_g[layer])
        h2d = _merge(h2d, gates, o_mla, o_ret.reshape(T, -1), o_mem.reshape(T, -1),
                     w_o_mla[layer], w_o_ret[layer], w_o_cross[layer], w_out[layer])
        xn2d, experts_t, gate_t = _peer_select(h2d, ffn_norm_g[layer], peer_w_q[layer], peer_keys[layer])
        h2d = _peer_experts(h2d, xn2d, experts_t.T, gate_t.T, peer_u[layer], peer_v[layer])
    return h2d.reshape(B, S, D)
```

```python
import functools

import jax
import jax.numpy as jnp
from jax import lax
from jax.experimental import pallas as pl
from jax.experimental.pallas import tpu as pltpu

D_MODEL = 1024
MEM_LEN = 256
MLA_HEADS = 8
MLA_Q_RANK = 384
MLA_KV_RANK = 128
MLA_NOPE = 64
MLA_ROPE = 32
MLA_V = 64
RET_HEADS = 4
RET_DK = 64
RET_DV = 128
RET_CHUNK = 128
X_HEADS = 4
X_DH = 128
N_BRANCH = 3
PEER_HEADS = 8
PEER_N_KEYS = 128
PEER_TOPK = 16
PEER_DQ = 256
PEER_DHALF = PEER_DQ // 2
ROPE_BASE = 10000.0
EPS = 1e-6
IN_SPLITS = (MLA_Q_RANK, MLA_KV_RANK, MLA_ROPE,
             RET_HEADS * RET_DK, RET_HEADS * RET_DK, RET_HEADS * RET_DV, RET_HEADS * RET_DV,
             X_HEADS * X_DH, N_BRANCH * D_MODEL)

LANES = 128
VMEM_LIMIT = 48 * 1024 * 1024
PEER_VMEM_LIMIT = 56 * 1024 * 1024
BF16 = jnp.bfloat16
F32 = jnp.float32


def _norm(x, g):
    return x * lax.rsqrt(jnp.mean(x * x, axis=-1, keepdims=True) + EPS) * g


def _row_specs(arrays, tm):
    return [pl.BlockSpec((tm, a.shape[1]), lambda i: (i, 0)) for a in arrays]


def _const_specs(arrays):
    return [pl.BlockSpec(a.shape, lambda i, n=a.ndim: (0,) * n) for a in arrays]


def _in_proj_kernel(x_ref, g_ref, *refs):
    n_w = len(refs) // 2
    w_refs, o_refs = refs[:n_w], refs[n_w:]
    nb = _norm(x_ref[...], g_ref[...]).astype(BF16)
    for w_ref, o_ref in zip(w_refs, o_refs):
        o_ref[...] = jnp.dot(nb, w_ref[...], preferred_element_type=F32).astype(o_ref.dtype)


def _in_proj(x2d, g, w_pieces, out_dtypes, tm=512):
    t, d = x2d.shape
    consts = [g.reshape(1, d)] + list(w_pieces)
    return pl.pallas_call(
        _in_proj_kernel,
        grid=(t // tm,),
        in_specs=_row_specs([x2d], tm) + _const_specs(consts),
        out_specs=[pl.BlockSpec((tm, w.shape[1]), lambda i: (i, 0)) for w in w_pieces],
        out_shape=[jax.ShapeDtypeStruct((t, w.shape[1]), dt) for w, dt in zip(w_pieces, out_dtypes)],
        compiler_params=pltpu.CompilerParams(dimension_semantics=("parallel",),
                                             vmem_limit_bytes=VMEM_LIMIT),
        name="in_proj",
    )(x2d, *consts)


def _rope_patterns(positions, n_rope, lead, width):
    inv_freq = ROPE_BASE ** (-jnp.arange(0, n_rope, 2, dtype=F32) / n_rope)
    ang = positions.astype(F32).reshape(-1, 1) * inv_freq
    cos, sin = jnp.cos(ang), jnp.sin(ang)
    t = ang.shape[0]
    tail = width - lead - n_rope
    cosp = jnp.concatenate([jnp.ones((t, lead)), cos, cos, jnp.zeros((t, tail))], axis=1)
    sinp = jnp.concatenate([jnp.zeros((t, lead)), -sin, sin, jnp.zeros((t, tail))], axis=1)
    rep = LANES // width
    return jnp.tile(cosp, (1, rep)), jnp.tile(sinp, (1, rep))


def _rotate_half(x, cosp, sinp, n_rope, lead, width):
    half = n_rope // 2
    lane = lax.broadcasted_iota(jnp.int32, x.shape, 1) & (width - 1)
    partner = jnp.where(lane < lead + half, pltpu.roll(x, LANES - half, 1), pltpu.roll(x, half, 1))
    return x * cosp + partner * sinp


MLA_QK = MLA_NOPE + MLA_ROPE


def _pad_head_cols(w, per_head):
    r = w.shape[0]
    w = w.reshape(r, MLA_HEADS, per_head)
    return jnp.pad(w, ((0, 0), (0, 0), (0, LANES - per_head))).reshape(r, MLA_HEADS * LANES)


def _mla_prep_kernel(cq_ref, ckv_ref, kr_ref, cos_ref, sin_ref, gq_ref, gkv_ref, wuq_ref, wuk_ref,
                     wuv_ref, qn_ref, kn_ref, q_ref, k_ref, v_ref):
    cqn = _norm(cq_ref[...], gq_ref[...]).astype(BF16)
    ckvn = _norm(ckv_ref[...], gkv_ref[...]).astype(BF16)
    qp = jnp.dot(cqn, wuq_ref[...], preferred_element_type=F32)
    kp = jnp.dot(ckvn, wuk_ref[...], preferred_element_type=F32)
    v_ref[...] = jnp.dot(ckvn, wuv_ref[...], preferred_element_type=F32).astype(v_ref.dtype)
    tm = qp.shape[0]
    kr = jnp.concatenate([jnp.zeros((tm, MLA_NOPE), F32), kr_ref[...],
                          jnp.zeros((tm, LANES - MLA_QK), F32)], axis=1)
    cosp, sinp = cos_ref[...], sin_ref[...]

    def qk_norm_rope(xh, g, scale):
        ss = jnp.sum(xh * xh, axis=-1, keepdims=True) * (1.0 / MLA_QK)
        xn = xh * lax.rsqrt(ss + EPS) * g
        return _rotate_half(xn, cosp, sinp, MLA_ROPE, MLA_NOPE, LANES) * scale

    for hd in range(MLA_HEADS):
        sl = slice(hd * LANES, (hd + 1) * LANES)
        q_ref[:, sl] = qk_norm_rope(qp[:, sl], qn_ref[...], MLA_QK ** -0.5).astype(q_ref.dtype)
        k_ref[:, sl] = qk_norm_rope(kp[:, sl] + kr, kn_ref[...], 1.0).astype(k_ref.dtype)


def _mla_prep(c_q, c_kv, k_r, cosp, sinp, q_norm_g, w_uq, kv_norm_g, w_ukv, qn_g, kn_g, tm=512):
    t = c_q.shape[0]
    w_ukv3 = w_ukv.reshape(MLA_KV_RANK, MLA_HEADS, MLA_NOPE + MLA_V)
    w_uk = _pad_head_cols(w_ukv3[:, :, :MLA_NOPE].reshape(MLA_KV_RANK, -1), MLA_NOPE).astype(BF16)
    w_uv = w_ukv3[:, :, MLA_NOPE:].reshape(MLA_KV_RANK, MLA_HEADS * MLA_V).astype(BF16)
    w_uqp = _pad_head_cols(w_uq, MLA_QK).astype(BF16)
    pad_g = lambda g: jnp.pad(g, (0, LANES - MLA_QK)).reshape(1, LANES)
    consts = [q_norm_g.reshape(1, -1), kv_norm_g.reshape(1, -1), w_uqp, w_uk, w_uv, pad_g(qn_g), pad_g(kn_g)]
    rows = [c_q, c_kv, k_r, cosp, sinp]
    wide = MLA_HEADS * LANES
    return pl.pallas_call(
        _mla_prep_kernel,
        grid=(t // tm,),
        in_specs=_row_specs(rows, tm) + _const_specs(consts),
        out_specs=[pl.BlockSpec((tm, wide), lambda i: (i, 0)),
                   pl.BlockSpec((tm, wide), lambda i: (i, 0)),
                   pl.BlockSpec((tm, MLA_HEADS * MLA_V), lambda i: (i, 0))],
        out_shape=[jax.ShapeDtypeStruct((t, wide), BF16), jax.ShapeDtypeStruct((t, wide), BF16),
                   jax.ShapeDtypeStruct((t, MLA_HEADS * MLA_V), BF16)],
        compiler_params=pltpu.CompilerParams(dimension_semantics=("parallel",),
                                             vmem_limit_bytes=VMEM_LIMIT),
        name="mla_prep",
    )(*rows, *consts)


def _mla_attn_kernel(q_ref, k_ref, v_ref, o_ref, *, tile):
    qi = pl.program_id(2)
    row = lax.broadcasted_iota(jnp.int32, (tile, tile), 0)
    col = lax.broadcasted_iota(jnp.int32, (tile, tile), 1)
    head_lanes = [slice(a * LANES, (a + 1) * LANES) for a in range(2)]
    qs = [q_ref[0, :, lanes] for lanes in head_lanes]

    def step(j, carry, masked):
        start = pl.multiple_of(j * tile, tile)
        vj = v_ref[0, pl.ds(start, tile), :]
        new = []
        for a, lanes in enumerate(head_lanes):
            m, l, acc = carry[a]
            kj = k_ref[0, pl.ds(start, tile), lanes]
            s = lax.dot_general(qs[a], kj, (((1,), (1,)), ((), ())), preferred_element_type=F32)
            if masked:
                s = jnp.where(col <= row, s, -1e30)
            m_new = jnp.maximum(m, jnp.max(s, axis=-1, keepdims=True))
            alpha = jnp.exp(m - m_new)
            p = jnp.exp(s - m_new)
            l = alpha * l + jnp.sum(p, axis=-1, keepdims=True)
            pv = jnp.dot(p.astype(BF16), vj, preferred_element_type=F32)
            new.append((m_new, l, alpha * acc + pv))
        return tuple(new)

    init = (jnp.full((tile, 1), -1e30, F32), jnp.zeros((tile, 1), F32), jnp.zeros((tile, 2 * MLA_V), F32))
    carry = lax.fori_loop(0, qi, functools.partial(step, masked=False), (init, init))
    (_, l0, acc0), (_, l1, acc1) = step(qi, carry, True)
    lane = lax.broadcasted_iota(jnp.int32, acc0.shape, 1)
    o_ref[0] = jnp.where(lane < MLA_V, acc0 / l0, acc1 / l1).astype(o_ref.dtype)


def _mla_attn(q, k, v, tile=512):
    b, s, _ = q.shape
    pair = 2 * LANES
    return pl.pallas_call(
        functools.partial(_mla_attn_kernel, tile=tile),
        grid=(b, MLA_HEADS // 2, s // tile),
        in_specs=[pl.BlockSpec((1, tile, pair), lambda bi, hp, qi: (bi, qi, hp)),
                  pl.BlockSpec((1, s, pair), lambda bi, hp, qi: (bi, 0, hp)),
                  pl.BlockSpec((1, s, 2 * MLA_V), lambda bi, hp, qi: (bi, 0, hp))],
        out_specs=pl.BlockSpec((1, tile, 2 * MLA_V), lambda bi, hp, qi: (bi, qi, hp)),
        out_shape=jax.ShapeDtypeStruct((b, s, MLA_HEADS * MLA_V), BF16),
        compiler_params=pltpu.CompilerParams(dimension_semantics=("parallel", "parallel", "arbitrary"),
                                             vmem_limit_bytes=VMEM_LIMIT),
        name="mla_attn",
    )(q, k, v)


def _mla_branch(c_q, c_kv, k_r, positions, q_norm_g, w_uq, kv_norm_g, w_ukv, qn_g, kn_g, b, s):
    cosp, sinp = _rope_patterns(positions, MLA_ROPE, MLA_NOPE, LANES)
    q, k, v = _mla_prep(c_q, c_kv, k_r, cosp, sinp, q_norm_g, w_uq, kv_norm_g, w_ukv, qn_g, kn_g)
    o = _mla_attn(q.reshape(b, s, -1), k.reshape(b, s, -1), v.reshape(b, s, -1))
    return o.reshape(b * s, -1)


def _retention_kernel(q_ref, k_ref, v_ref, g_ref, cos_ref, sin_ref, decay_ref, xi_ref, zeta_ref,
                      cd_ref, gn_ref, o_ref):
    L = RET_CHUNK
    n_chunks = q_ref.shape[1] // L
    lane = lax.broadcasted_iota(jnp.int32, (L, LANES), 1)

    def chunk_body(c, states):
        rows = pl.ds(pl.multiple_of(c * L, L), L)
        cosp, sinp = cos_ref[0, rows, :], sin_ref[0, rows, :]
        qc = _rotate_half(q_ref[0, rows, :], cosp, sinp, RET_DK, 0, RET_DK)
        kc = _rotate_half(k_ref[0, rows, :], cosp, sinp, RET_DK, 0, RET_DK) * (RET_DK ** -0.5)
        qx = qc * xi_ref[0]
        kz = kc * zeta_ref[0]
        new_states = []
        for a in range(2):
            mine = (lane >= a * RET_DK) & (lane < (a + 1) * RET_DK)
            qa = jnp.where(mine, qc, 0.0).astype(BF16)
            ka = jnp.where(mine, kc, 0.0).astype(BF16)
            va = v_ref[0, rows, a * RET_DV:(a + 1) * RET_DV]
            inner = lax.dot_general(qa, ka, (((1,), (1,)), ((), ())),
                                    preferred_element_type=F32) * decay_ref[a]
            cross = jnp.dot(jnp.where(mine, qx, 0.0).astype(BF16), states[a].astype(BF16),
                            preferred_element_type=F32)
            o = jnp.dot(inner.astype(BF16), va, preferred_element_type=F32) + cross
            kzt = jnp.where(mine, kz, 0.0).T.astype(BF16)
            incr = jnp.dot(kzt, va, preferred_element_type=F32)
            new_states.append(cd_ref[a, 0:1, :] * states[a] + incr)
            o = _norm(o, gn_ref[a])
            gate = g_ref[0, rows, a * RET_DV:(a + 1) * RET_DV].astype(F32)
            o_ref[0, rows, a * RET_DV:(a + 1) * RET_DV] = (o * gate * jax.nn.sigmoid(gate)).astype(o_ref.dtype)
        return tuple(new_states)

    zero = jnp.zeros((LANES, RET_DV), F32)
    lax.fori_loop(0, n_chunks, chunk_body, (zero, zero))


def _retention_branch(r_q, r_k, r_v, r_g, positions, gn_g):
    b, s, _ = r_q.shape
    L = RET_CHUNK
    cosp, sinp = _rope_patterns(positions, RET_DK, 0, RET_DK)
    cosp, sinp = cosp.reshape(b, s, LANES), sinp.reshape(b, s, LANES)
    log_gamma = jnp.log(1.0 - 2.0 ** (-5.0 - jnp.arange(RET_HEADS, dtype=F32)))
    idx = jnp.arange(L, dtype=F32)
    diff = idx[:, None] - idx[None, :]
    decay = jnp.where(diff >= 0, jnp.exp(log_gamma[:, None, None] * jnp.maximum(diff, 0.0)), 0.0)
    xi = jnp.exp(log_gamma[:, None] * (idx + 1.0))
    zeta = jnp.exp(log_gamma[:, None] * (L - 1.0 - idx))
    per_pair = lambda a: jnp.repeat(a.reshape(RET_HEADS // 2, 2, L), RET_DK, axis=1).transpose(0, 2, 1)
    cd = jnp.broadcast_to(jnp.exp(log_gamma * L)[:, None, None], (RET_HEADS, 8, RET_DV))
    gn = gn_g.reshape(RET_HEADS, 1, RET_DV)
    n_pairs = RET_HEADS // 2
    seq = lambda width: pl.BlockSpec((1, s, width), lambda bi, hp: (bi, 0, hp))
    return pl.pallas_call(
        _retention_kernel,
        grid=(b, n_pairs),
        in_specs=[seq(LANES), seq(LANES), seq(2 * RET_DV), seq(2 * RET_DV),
                  pl.BlockSpec((1, s, LANES), lambda bi, hp: (bi, 0, 0)),
                  pl.BlockSpec((1, s, LANES), lambda bi, hp: (bi, 0, 0)),
                  pl.BlockSpec((2, L, L), lambda bi, hp: (hp, 0, 0)),
                  pl.BlockSpec((1, L, LANES), lambda bi, hp: (hp, 0, 0)),
                  pl.BlockSpec((1, L, LANES), lambda bi, hp: (hp, 0, 0)),
                  pl.BlockSpec((2, 8, RET_DV), lambda bi, hp: (hp, 0, 0)),
                  pl.BlockSpec((2, 1, RET_DV), lambda bi, hp: (hp, 0, 0))],
        out_specs=seq(2 * RET_DV),
        out_shape=jax.ShapeDtypeStruct((b, s, RET_HEADS * RET_DV), BF16),
        compiler_params=pltpu.CompilerParams(dimension_semantics=("parallel", "parallel"),
                                             vmem_limit_bytes=VMEM_LIMIT),
        name="retention",
    )(r_q, r_k, r_v, r_g, cosp, sinp, decay, per_pair(xi), per_pair(zeta), cd, gn)


def _mem_kv_kernel(mem_ref, g_ref, w_ref, kn_ref, k_ref, v_ref):
    kv = jnp.dot(_norm(mem_ref[0], g_ref[...]).astype(BF16), w_ref[...], preferred_element_type=F32)
    width = X_HEADS * X_DH
    for hd in range(X_HEADS):
        sl = slice(hd * X_DH, (hd + 1) * X_DH)
        k_ref[0, :, sl] = _norm(kv[:, sl], kn_ref[...]).astype(k_ref.dtype)
    v_ref[0] = kv[:, width:].astype(v_ref.dtype)


def _mem_kv(mem, mem_g, w_mem_kv, kn_g):
    b, m, d = mem.shape
    width = X_HEADS * X_DH
    w4 = w_mem_kv.reshape(d, X_HEADS, 2, X_DH)
    w = jnp.concatenate([w4[:, :, 0].reshape(d, width), w4[:, :, 1].reshape(d, width)], axis=1).astype(BF16)
    consts = [mem_g.reshape(1, d), w, kn_g.reshape(1, X_DH)]
    blk = pl.BlockSpec((1, m, width), lambda i: (i, 0, 0))
    return pl.pallas_call(
        _mem_kv_kernel,
        grid=(b,),
        in_specs=[pl.BlockSpec((1, m, d), lambda i: (i, 0, 0))] + _const_specs(consts),
        out_specs=[blk, blk],
        out_shape=[jax.ShapeDtypeStruct((b, m, width), BF16)] * 2,
        compiler_params=pltpu.CompilerParams(dimension_semantics=("parallel",),
                                             vmem_limit_bytes=VMEM_LIMIT),
        name="mem_kv",
    )(mem, *consts)


def _mem_attn_kernel(q_ref, k_ref, v_ref, qn_ref, o_ref):
    for hd in range(X_HEADS):
        sl = slice(hd * X_DH, (hd + 1) * X_DH)
        q = (_norm(q_ref[0, :, sl].astype(F32), qn_ref[...]) * (X_DH ** -0.5)).astype(BF16)
        s = lax.dot_general(q, k_ref[0, :, sl], (((1,), (1,)), ((), ())), preferred_element_type=F32)
        p = jnp.exp(s - jnp.max(s, axis=-1, keepdims=True))
        o = jnp.dot(p.astype(BF16), v_ref[0, :, sl], preferred_element_type=F32)
        o_ref[0, :, sl] = (o / jnp.sum(p, axis=-1, keepdims=True)).astype(o_ref.dtype)


def _mem_attn(x_q, k, v, qn_g, tm=512):
    b, s, width = x_q.shape
    m = k.shape[1]
    kv_spec = pl.BlockSpec((1, m, width), lambda bi, i: (bi, 0, 0))
    return pl.pallas_call(
        _mem_attn_kernel,
        grid=(b, s // tm),
        in_specs=[pl.BlockSpec((1, tm, width), lambda bi, i: (bi, i, 0)), kv_spec, kv_spec,
                  pl.BlockSpec((1, X_DH), lambda bi, i: (0, 0))],
        out_specs=pl.BlockSpec((1, tm, width), lambda bi, i: (bi, i, 0)),
        out_shape=jax.ShapeDtypeStruct((b, s, width), BF16),
        compiler_params=pltpu.CompilerParams(dimension_semantics=("parallel", "parallel"),
                                             vmem_limit_bytes=VMEM_LIMIT),
        name="mem_attn",
    )(x_q, k, v, qn_g.reshape(1, X_DH))


def _merge_kernel(x_ref, gates_ref, oa_ref, ob_ref, oc_ref, wa_ref, wb_ref, wc_ref, wout_ref, h_ref):
    merged = None
    for i, (o_ref, w_ref) in enumerate(((oa_ref, wa_ref), (ob_ref, wb_ref), (oc_ref, wc_ref))):
        y = jnp.dot(o_ref[...], w_ref[...], preferred_element_type=F32)
        gate = jax.nn.sigmoid(gates_ref[:, i * D_MODEL:(i + 1) * D_MODEL].astype(F32))
        merged = gate * y if merged is None else merged + gate * y
    h_ref[...] = x_ref[...] + jnp.dot(merged.astype(BF16), wout_ref[...], preferred_element_type=F32)


def _merge(x2d, gates, o_mla, o_ret, o_mem, w_o_mla, w_o_ret, w_o_cross, w_out, tm=512):
    t, d = x2d.shape
    rows = [x2d, gates, o_mla, o_ret, o_mem]
    consts = [w.astype(BF16) for w in (w_o_mla, w_o_ret, w_o_cross, w_out)]
    return pl.pallas_call(
        _merge_kernel,
        grid=(t // tm,),
        in_specs=_row_specs(rows, tm) + _const_specs(consts),
        out_specs=pl.BlockSpec((tm, d), lambda i: (i, 0)),
        out_shape=jax.ShapeDtypeStruct((t, d), F32),
        compiler_params=pltpu.CompilerParams(dimension_semantics=("parallel",),
                                             vmem_limit_bytes=VMEM_LIMIT),
        name="merge",
    )(*rows, *consts)


PEER_SLOTS = PEER_HEADS * PEER_TOPK
PAIR_CANDS = [(i, j) for i in range(PEER_TOPK) for j in range(PEER_TOPK)
              if (i + 1) * (j + 1) <= PEER_TOPK]
N_CANDS = len(PAIR_CANDS)
CAND_ROWS = -(-N_CANDS // 8) * 8
NEG_INF = float("-inf")
SELECT_HEADS_PER_ITER = 2


def _top_rounds(s, n_rounds):
    rows = lax.broadcasted_iota(jnp.int32, s.shape, 0)
    vals, idxs = [], []
    for _ in range(n_rounds):
        m = jnp.max(s, axis=0, keepdims=True)
        idx = jnp.min(jnp.where(s == m, rows, s.shape[0]), axis=0, keepdims=True)
        s = jnp.where(rows == idx, NEG_INF, s)
        vals.append(m)
        idxs.append(idx)
    return vals, idxs


def _peer_select_kernel(h_ref, g_ref, wq_ref, keys_ref, xn_ref, exp_ref, gate_ref, q_ref):
    xnb = _norm(h_ref[...], g_ref[...]).astype(BF16)
    xn_ref[...] = xnb
    q_ref[...] = jnp.dot(xnb, wq_ref[...], preferred_element_type=F32).astype(BF16)
    tb = xnb.shape[0]

    def one_head(hd):
        sv, si = [], []
        for c in range(2):
            col = pl.multiple_of(hd * PEER_DQ + c * PEER_DHALF, PEER_DHALF)
            qh = q_ref[:, pl.ds(col, PEER_DHALF)]
            st = lax.dot_general(keys_ref[hd, c], qh, (((1,), (1,)), ((), ())),
                                 preferred_element_type=F32)
            v, i = _top_rounds(st, PEER_TOPK)
            sv.append(v)
            si.append(i)
        pad = CAND_ROWS - N_CANDS
        cand = jnp.concatenate([sv[0][i] + sv[1][j] for i, j in PAIR_CANDS]
                               + [jnp.full((pad, tb), NEG_INF, F32)], axis=0)
        cand_e = jnp.concatenate([si[0][i] * PEER_N_KEYS + si[1][j] for i, j in PAIR_CANDS]
                                 + [jnp.zeros((pad, tb), jnp.int32)], axis=0)
        rows = lax.broadcasted_iota(jnp.int32, cand.shape, 0)
        best, experts = [], []
        for _ in range(PEER_TOPK):
            m = jnp.max(cand, axis=0, keepdims=True)
            pos = jnp.min(jnp.where(cand == m, rows, CAND_ROWS), axis=0, keepdims=True)
            hit = rows == pos
            experts.append(jnp.sum(jnp.where(hit, cand_e, 0), axis=0, keepdims=True))
            cand = jnp.where(hit, NEG_INF, cand)
            best.append(m)
        p = [jnp.exp(b - best[0]) for b in best]
        denom = p[0]
        for pk in p[1:]:
            denom = denom + pk
        row0 = pl.multiple_of(hd * PEER_TOPK, PEER_TOPK)
        exp_ref[pl.ds(row0, PEER_TOPK), :] = jnp.concatenate(experts, axis=0)
        gate_ref[pl.ds(row0, PEER_TOPK), :] = jnp.concatenate([pk / denom for pk in p], axis=0)

    def head_group(i, carry):
        for j in range(SELECT_HEADS_PER_ITER):
            one_head(i * SELECT_HEADS_PER_ITER + j)
        return carry

    lax.fori_loop(0, PEER_HEADS // SELECT_HEADS_PER_ITER, head_group, 0)


def _peer_select(h2d, g, w_q, sub_keys, tb=128):
    t, d = h2d.shape
    consts = [g.reshape(1, d), w_q.astype(BF16), sub_keys.astype(BF16)]
    return pl.pallas_call(
        _peer_select_kernel,
        grid=(t // tb,),
        in_specs=_row_specs([h2d], tb) + _const_specs(consts),
        out_specs=[
            pl.BlockSpec((tb, d), lambda i: (i, 0)),
            pl.BlockSpec((PEER_SLOTS, tb), lambda i: (0, i)),
            pl.BlockSpec((PEER_SLOTS, tb), lambda i: (0, i)),
        ],
        out_shape=[
            jax.ShapeDtypeStruct((t, d), BF16),
            jax.ShapeDtypeStruct((PEER_SLOTS, t), jnp.int32),
            jax.ShapeDtypeStruct((PEER_SLOTS, t), F32),
        ],
        scratch_shapes=[pltpu.VMEM((tb, PEER_HEADS * PEER_DQ), BF16)],
        compiler_params=pltpu.CompilerParams(dimension_semantics=("parallel",),
                                             vmem_limit_bytes=VMEM_LIMIT),
        name="peer_select",
    )(h2d, *consts)


ROW_TILE = D_MODEL // LANES
PAIR_ROWS = 2 * ROW_TILE


def _pack_expert_table(tab):
    n, d = tab.shape
    t = tab.astype(BF16).reshape(2, n // 2, ROW_TILE, LANES)
    bits = lax.bitcast_convert_type(t, jnp.uint16).astype(jnp.uint32)
    return (bits[0] | (bits[1] << 16)).reshape(n // 2 * ROW_TILE, LANES)


TOKENS_PER_ITER = 8


def _gather_tiles(idx_ref, tab_ref, g_ref, t):
    for k in range(PEER_SLOTS):
        start = pl.multiple_of(idx_ref[t, k], ROW_TILE)
        g_ref[k * ROW_TILE:(k + 1) * ROW_TILE, :] = tab_ref[pl.ds(start, ROW_TILE), :]


def _peer_u_kernel(idx_ref, x_ref, hi_ref, gate_ref, fold_ref, tab_ref, w_ref, sel_ref, *g_refs):
    tb = x_ref.shape[0]
    shape = (ROW_TILE, PEER_SLOTS * PAIR_ROWS)
    row_in_tile = lax.broadcasted_iota(jnp.int32, shape, 1) & (PAIR_ROWS - 1)
    chunk = lax.broadcasted_iota(jnp.int32, shape, 0)
    keep = (row_in_tile >> 1) == chunk

    def token_pair(i, carry):
        for j, g_ref in enumerate(g_refs):
            t = i * TOKENS_PER_ITER + j
            _gather_tiles(idx_ref, tab_ref, g_ref, t)
            out = lax.dot_general(x_ref[t], pltpu.bitcast(g_ref[...], BF16), (((1,), (1,)), ((), ())),
                                  preferred_element_type=F32)
            sel_ref[pl.ds(pl.multiple_of(t * ROW_TILE, ROW_TILE), ROW_TILE), :] = (
                jnp.where(keep, out[:ROW_TILE], 0.0))
        return carry

    lax.fori_loop(0, tb // TOKENS_PER_ITER, token_pair, 0)
    sel = sel_ref[...]
    sel_hi = sel.astype(BF16)
    sel_lo = (sel - sel_hi.astype(F32)).astype(BF16)
    r = (jnp.dot(sel_hi, fold_ref[...], preferred_element_type=F32)
         + jnp.dot(sel_lo, fold_ref[...], preferred_element_type=F32))
    r = jnp.sum(r.reshape(tb, ROW_TILE, 2 * PEER_SLOTS), axis=1)
    a = jnp.where(hi_ref[...] == 1, r[:, PEER_SLOTS:], r[:, :PEER_SLOTS])
    w_ref[...] = gate_ref[...] * jax.nn.gelu(a)


def _peer_u(idx, x3, hi, gate, tab, tb):
    t = idx.shape[0]
    eye = jnp.eye(PEER_SLOTS, dtype=BF16)
    parity = (jnp.arange(PAIR_ROWS) & 1).astype(BF16)
    fold = jnp.concatenate([jnp.kron(eye, (1 - parity)[:, None]), jnp.kron(eye, parity[:, None])], axis=1)
    g_shape = pltpu.VMEM((PEER_SLOTS * ROW_TILE, LANES), jnp.uint32)
    return pl.pallas_call(
        _peer_u_kernel,
        grid=(t // tb,),
        in_specs=[
            pl.BlockSpec((tb, PEER_SLOTS), lambda i: (i, 0), memory_space=pltpu.SMEM),
            pl.BlockSpec((tb, PAIR_ROWS, LANES), lambda i: (i, 0, 0)),
            pl.BlockSpec((tb, PEER_SLOTS), lambda i: (i, 0)),
            pl.BlockSpec((tb, PEER_SLOTS), lambda i: (i, 0)),
            pl.BlockSpec(fold.shape, lambda i: (0, 0), pipeline_mode=pl.Buffered(1)),
            pl.BlockSpec(tab.shape, lambda i: (0, 0), pipeline_mode=pl.Buffered(1)),
        ],
        out_specs=pl.BlockSpec((tb, PEER_SLOTS), lambda i: (i, 0)),
        out_shape=jax.ShapeDtypeStruct((t, PEER_SLOTS), F32),
        scratch_shapes=[pltpu.VMEM((tb * ROW_TILE, PEER_SLOTS * PAIR_ROWS), F32)]
        + [g_shape] * TOKENS_PER_ITER,
        compiler_params=pltpu.CompilerParams(dimension_semantics=("arbitrary",),
                                             vmem_limit_bytes=PEER_VMEM_LIMIT),
        name="peer_u",
    )(idx, x3, hi, gate, fold, tab)


def _peer_v_kernel(idx_ref, hi_ref, w_ref, h_ref, route_ref, tab_ref, o_ref, wx_ref, *g_refs):
    tb = hi_ref.shape[0]
    w = w_ref[...]
    w_hi = jnp.where(hi_ref[...] == 1, w, 0.0)
    w2 = jnp.concatenate([w - w_hi, w_hi], axis=1).astype(BF16)
    wx_ref[...] = jnp.dot(w2, route_ref[...], preferred_element_type=F32)
    shape = (PAIR_ROWS, PEER_SLOTS * PAIR_ROWS)
    row_in_tile = lax.broadcasted_iota(jnp.int32, shape, 1) & (PAIR_ROWS - 1)
    chunk = lax.broadcasted_iota(jnp.int32, shape, 0)
    keep = (row_in_tile >> 1) == chunk

    def token_pair(i, carry):
        for j, g_ref in enumerate(g_refs):
            t = i * TOKENS_PER_ITER + j
            _gather_tiles(idx_ref, tab_ref, g_ref, t)
            lhs = jnp.where(keep, wx_ref[pl.ds(t, 1), :], 0.0).astype(BF16)
            o = jnp.dot(lhs, pltpu.bitcast(g_ref[...], BF16), preferred_element_type=F32)
            o_ref[t] = h_ref[t] + o[:ROW_TILE]
        return carry

    lax.fori_loop(0, tb // TOKENS_PER_ITER, token_pair, 0)


def _peer_v(idx, hi, w, h3, tab, tb):
    t = idx.shape[0]
    eye = jnp.eye(PEER_SLOTS, dtype=BF16)
    parity = (jnp.arange(PAIR_ROWS) & 1).astype(BF16)
    route = jnp.concatenate([jnp.kron(eye, (1 - parity)[None, :]), jnp.kron(eye, parity[None, :])], axis=0)
    return pl.pallas_call(
        _peer_v_kernel,
        grid=(t // tb,),
        in_specs=[
            pl.BlockSpec((tb, PEER_SLOTS), lambda i: (i, 0), memory_space=pltpu.SMEM),
            pl.BlockSpec((tb, PEER_SLOTS), lambda i: (i, 0)),
            pl.BlockSpec((tb, PEER_SLOTS), lambda i: (i, 0)),
            pl.BlockSpec((tb, ROW_TILE, LANES), lambda i: (i, 0, 0)),
            pl.BlockSpec(route.shape, lambda i: (0, 0), pipeline_mode=pl.Buffered(1)),
            pl.BlockSpec(tab.shape, lambda i: (0, 0), pipeline_mode=pl.Buffered(1)),
        ],
        out_specs=pl.BlockSpec((tb, ROW_TILE, LANES), lambda i: (i, 0, 0)),
        out_shape=jax.ShapeDtypeStruct((t, ROW_TILE, LANES), F32),
        scratch_shapes=[pltpu.VMEM((tb, PEER_SLOTS * PAIR_ROWS), F32)]
        + [pltpu.VMEM((PEER_SLOTS * ROW_TILE, LANES), jnp.uint32)] * TOKENS_PER_ITER,
        compiler_params=pltpu.CompilerParams(dimension_semantics=("arbitrary",),
                                             vmem_limit_bytes=PEER_VMEM_LIMIT),
        name="peer_v",
    )(idx, hi, w, h3, route, tab)


def _peer_experts(h2d, xn2d, experts, gate, u_tab, v_tab, tb_u=64, tb_v=64):
    t, d = xn2d.shape
    half = u_tab.shape[0] // 2
    hi = (experts >= half).astype(jnp.int32)
    idx = (experts - hi * half) * ROW_TILE
    x3 = jnp.pad(xn2d.astype(BF16).reshape(t, ROW_TILE, LANES), ((0, 0), (0, ROW_TILE), (0, 0)))
    w = _peer_u(idx, x3, hi, gate, _pack_expert_table(u_tab), tb_u)
    out = _peer_v(idx, hi, w, h2d.reshape(t, ROW_TILE, LANES), _pack_expert_table(v_tab), tb_v)
    return out.reshape(t, d)


def kernel(x, mem, positions, attn_norm_g, w_in, mla_q_norm_g, w_uq, mla_kv_norm_g, w_ukv,
           mla_qn_g, mla_kn_g, w_o_mla, ret_gn_g, w_o_ret, mem_norm_g, w_mem_kv, x_qn_g, x_kn_g,
           w_o_cross, w_out, ffn_norm_g, peer_w_q, peer_keys, peer_u, peer_v):
    B, S, D = x.shape
    T = B * S
    h2d = x.reshape(T, D)
    offs = [0]
    for sz in IN_SPLITS:
        offs.append(offs[-1] + sz)
    for layer in range(w_in.shape[0]):
        w_pieces = [w_in[layer][:, offs[i]:offs[i + 1]].astype(BF16) for i in range(len(IN_SPLITS))]
        out_dtypes = [F32, F32, F32, F32, F32, BF16, BF16, BF16, BF16]
        c_q, c_kv, k_r, r_q, r_k, r_v, r_g, x_q, gates = _in_proj(h2d, attn_norm_g[layer], w_pieces, out_dtypes)
        o_mla = _mla_branch(c_q, c_kv, k_r, positions, mla_q_norm_g[layer], w_uq[layer],
                            mla_kv_norm_g[layer], w_ukv[layer], mla_qn_g[layer], mla_kn_g[layer], B, S)
        seq = lambda a: a.reshape(B, S, -1)
        o_ret = _retention_branch(seq(r_q), seq(r_k), seq(r_v), seq(r_g), positions, ret_gn_g[layer])
        mk, mv = _mem_kv(mem, mem_norm_g[layer], w_mem_kv[layer], x_kn_g[layer])
        o_mem = _mem_attn(seq(x_q), mk, mv, x_qn_g[layer])
        h2d = _merge(h2d, gates, o_mla, o_ret.reshape(T, -1), o_mem.reshape(T, -1),
                     w_o_mla[layer], w_o_ret[layer], w_o_cross[layer], w_out[layer])
        xn2d, experts_t, gate_t = _peer_select(h2d, ffn_norm_g[layer], peer_w_q[layer], peer_keys[layer])
        h2d = _peer_experts(h2d, xn2d, experts_t.T, gate_t.T, peer_u[layer], peer_v[layer])
    return h2d.reshape(B, S, D)
```

```python
import functools

import jax
import jax.numpy as jnp
from jax import lax
from jax.experimental import pallas as pl
from jax.experimental.pallas import tpu as pltpu

D_MODEL = 1024
MEM_LEN = 256
MLA_HEADS = 8
MLA_Q_RANK = 384
MLA_KV_RANK = 128
MLA_NOPE = 64
MLA_ROPE = 32
MLA_V = 64
RET_HEADS = 4
RET_DK = 64
RET_DV = 128
RET_CHUNK = 128
X_HEADS = 4
X_DH = 128
N_BRANCH = 3
PEER_HEADS = 8
PEER_N_KEYS = 128
PEER_TOPK = 16
PEER_DQ = 256
PEER_DHALF = PEER_DQ // 2
ROPE_BASE = 10000.0
EPS = 1e-6
IN_SPLITS = (MLA_Q_RANK, MLA_KV_RANK, MLA_ROPE,
             RET_HEADS * RET_DK, RET_HEADS * RET_DK, RET_HEADS * RET_DV, RET_HEADS * RET_DV,
             X_HEADS * X_DH, N_BRANCH * D_MODEL)

LANES = 128
VMEM_LIMIT = 48 * 1024 * 1024
PEER_VMEM_LIMIT = 56 * 1024 * 1024
BF16 = jnp.bfloat16
F32 = jnp.float32


def _norm(x, g):
    return x * lax.rsqrt(jnp.mean(x * x, axis=-1, keepdims=True) + EPS) * g


def _row_specs(arrays, tm):
    return [pl.BlockSpec((tm, a.shape[1]), lambda i: (i, 0)) for a in arrays]


def _const_specs(arrays):
    return [pl.BlockSpec(a.shape, lambda i, n=a.ndim: (0,) * n) for a in arrays]


def _in_proj_kernel(x_ref, g_ref, *refs):
    n_w = len(refs) // 2
    w_refs, o_refs = refs[:n_w], refs[n_w:]
    nb = _norm(x_ref[...], g_ref[...]).astype(BF16)
    for w_ref, o_ref in zip(w_refs, o_refs):
        o_ref[...] = jnp.dot(nb, w_ref[...], preferred_element_type=F32).astype(o_ref.dtype)


def _in_proj(x2d, g, w_pieces, out_dtypes, tm=512):
    t, d = x2d.shape
    consts = [g.reshape(1, d)] + list(w_pieces)
    return pl.pallas_call(
        _in_proj_kernel,
        grid=(t // tm,),
        in_specs=_row_specs([x2d], tm) + _const_specs(consts),
        out_specs=[pl.BlockSpec((tm, w.shape[1]), lambda i: (i, 0)) for w in w_pieces],
        out_shape=[jax.ShapeDtypeStruct((t, w.shape[1]), dt) for w, dt in zip(w_pieces, out_dtypes)],
        compiler_params=pltpu.CompilerParams(dimension_semantics=("parallel",),
                                             vmem_limit_bytes=VMEM_LIMIT),
        name="in_proj",
    )(x2d, *consts)


def _rope_patterns(positions, n_rope, lead, width):
    inv_freq = ROPE_BASE ** (-jnp.arange(0, n_rope, 2, dtype=F32) / n_rope)
    ang = positions.astype(F32).reshape(-1, 1) * inv_freq
    cos, sin = jnp.cos(ang), jnp.sin(ang)
    t = ang.shape[0]
    tail = width - lead - n_rope
    cosp = jnp.concatenate([jnp.ones((t, lead)), cos, cos, jnp.zeros((t, tail))], axis=1)
    sinp = jnp.concatenate([jnp.zeros((t, lead)), -sin, sin, jnp.zeros((t, tail))], axis=1)
    rep = LANES // width
    return jnp.tile(cosp, (1, rep)), jnp.tile(sinp, (1, rep))


def _rotate_half(x, cosp, sinp, n_rope, lead, width):
    half = n_rope // 2
    lane = lax.broadcasted_iota(jnp.int32, x.shape, 1) & (width - 1)
    partner = jnp.where(lane < lead + half, pltpu.roll(x, LANES - half, 1), pltpu.roll(x, half, 1))
    return x * cosp + partner * sinp


MLA_QK = MLA_NOPE + MLA_ROPE


def _pad_head_cols(w, per_head):
    r = w.shape[0]
    w = w.reshape(r, MLA_HEADS, per_head)
    return jnp.pad(w, ((0, 0), (0, 0), (0, LANES - per_head))).reshape(r, MLA_HEADS * LANES)


def _mla_prep_kernel(cq_ref, ckv_ref, kr_ref, cos_ref, sin_ref, gq_ref, gkv_ref, wuq_ref, wuk_ref,
                     wuv_ref, qn_ref, kn_ref, q_ref, k_ref, v_ref):
    cqn = _norm(cq_ref[...], gq_ref[...]).astype(BF16)
    ckvn = _norm(ckv_ref[...], gkv_ref[...]).astype(BF16)
    qp = jnp.dot(cqn, wuq_ref[...], preferred_element_type=F32)
    kp = jnp.dot(ckvn, wuk_ref[...], preferred_element_type=F32)
    v_ref[...] = jnp.dot(ckvn, wuv_ref[...], preferred_element_type=F32).astype(v_ref.dtype)
    tm = qp.shape[0]
    kr = jnp.concatenate([jnp.zeros((tm, MLA_NOPE), F32), kr_ref[...],
                          jnp.zeros((tm, LANES - MLA_QK), F32)], axis=1)
    cosp, sinp = cos_ref[...], sin_ref[...]

    def qk_norm_rope(xh, g, scale):
        ss = jnp.sum(xh * xh, axis=-1, keepdims=True) * (1.0 / MLA_QK)
        xn = xh * lax.rsqrt(ss + EPS) * g
        return _rotate_half(xn, cosp, sinp, MLA_ROPE, MLA_NOPE, LANES) * scale

    for hd in range(MLA_HEADS):
        sl = slice(hd * LANES, (hd + 1) * LANES)
        q_ref[:, sl] = qk_norm_rope(qp[:, sl], qn_ref[...], MLA_QK ** -0.5).astype(q_ref.dtype)
        k_ref[:, sl] = qk_norm_rope(kp[:, sl] + kr, kn_ref[...], 1.0).astype(k_ref.dtype)


def _mla_prep(c_q, c_kv, k_r, cosp, sinp, q_norm_g, w_uq, kv_norm_g, w_ukv, qn_g, kn_g, tm=512):
    t = c_q.shape[0]
    w_ukv3 = w_ukv.reshape(MLA_KV_RANK, MLA_HEADS, MLA_NOPE + MLA_V)
    w_uk = _pad_head_cols(w_ukv3[:, :, :MLA_NOPE].reshape(MLA_KV_RANK, -1), MLA_NOPE).astype(BF16)
    w_uv = w_ukv3[:, :, MLA_NOPE:].reshape(MLA_KV_RANK, MLA_HEADS * MLA_V).astype(BF16)
    w_uqp = _pad_head_cols(w_uq, MLA_QK).astype(BF16)
    pad_g = lambda g: jnp.pad(g, (0, LANES - MLA_QK)).reshape(1, LANES)
    consts = [q_norm_g.reshape(1, -1), kv_norm_g.reshape(1, -1), w_uqp, w_uk, w_uv, pad_g(qn_g), pad_g(kn_g)]
    rows = [c_q, c_kv, k_r, cosp, sinp]
    wide = MLA_HEADS * LANES
    return pl.pallas_call(
        _mla_prep_kernel,
        grid=(t // tm,),
        in_specs=_row_specs(rows, tm) + _const_specs(consts),
        out_specs=[pl.BlockSpec((tm, wide), lambda i: (i, 0)),
                   pl.BlockSpec((tm, wide), lambda i: (i, 0)),
                   pl.BlockSpec((tm, MLA_HEADS * MLA_V), lambda i: (i, 0))],
        out_shape=[jax.ShapeDtypeStruct((t, wide), BF16), jax.ShapeDtypeStruct((t, wide), BF16),
                   jax.ShapeDtypeStruct((t, MLA_HEADS * MLA_V), BF16)],
        compiler_params=pltpu.CompilerParams(dimension_semantics=("parallel",),
                                             vmem_limit_bytes=VMEM_LIMIT),
        name="mla_prep",
    )(*rows, *consts)


def _mla_attn_kernel(q_ref, k_ref, v_ref, o_ref, *, tile):
    qi = pl.program_id(2)
    row = lax.broadcasted_iota(jnp.int32, (tile, tile), 0)
    col = lax.broadcasted_iota(jnp.int32, (tile, tile), 1)
    head_lanes = [slice(a * LANES, (a + 1) * LANES) for a in range(2)]
    qs = [q_ref[0, :, lanes] for lanes in head_lanes]

    def step(j, carry, masked):
        start = pl.multiple_of(j * tile, tile)
        vj = v_ref[0, pl.ds(start, tile), :]
        new = []
        for a, lanes in enumerate(head_lanes):
            m, l, acc = carry[a]
            kj = k_ref[0, pl.ds(start, tile), lanes]
            s = lax.dot_general(qs[a], kj, (((1,), (1,)), ((), ())), preferred_element_type=F32)
            if masked:
                s = jnp.where(col <= row, s, -1e30)
            m_new = jnp.maximum(m, jnp.max(s, axis=-1, keepdims=True))
            alpha = jnp.exp(m - m_new)
            p = jnp.exp(s - m_new)
            l = alpha * l + jnp.sum(p, axis=-1, keepdims=True)
            pv = jnp.dot(p.astype(BF16), vj, preferred_element_type=F32)
            new.append((m_new, l, alpha * acc + pv))
        return tuple(new)

    init = (jnp.full((tile, 1), -1e30, F32), jnp.zeros((tile, 1), F32), jnp.zeros((tile, 2 * MLA_V), F32))
    carry = lax.fori_loop(0, qi, functools.partial(step, masked=False), (init, init))
    (_, l0, acc0), (_, l1, acc1) = step(qi, carry, True)
    lane = lax.broadcasted_iota(jnp.int32, acc0.shape, 1)
    o_ref[0] = jnp.where(lane < MLA_V, acc0 / l0, acc1 / l1).astype(o_ref.dtype)


def _mla_attn(q, k, v, tile=512):
    b, s, _ = q.shape
    pair = 2 * LANES
    return pl.pallas_call(
        functools.partial(_mla_attn_kernel, tile=tile),
        grid=(b, MLA_HEADS // 2, s // tile),
        in_specs=[pl.BlockSpec((1, tile, pair), lambda bi, hp, qi: (bi, qi, hp)),
                  pl.BlockSpec((1, s, pair), lambda bi, hp, qi: (bi, 0, hp)),
                  pl.BlockSpec((1, s, 2 * MLA_V), lambda bi, hp, qi: (bi, 0, hp))],
        out_specs=pl.BlockSpec((1, tile, 2 * MLA_V), lambda bi, hp, qi: (bi, qi, hp)),
        out_shape=jax.ShapeDtypeStruct((b, s, MLA_HEADS * MLA_V), BF16),
        compiler_params=pltpu.CompilerParams(dimension_semantics=("parallel", "parallel", "arbitrary"),
                                             vmem_limit_bytes=VMEM_LIMIT),
        name="mla_attn",
    )(q, k, v)


def _mla_branch(c_q, c_kv, k_r, positions, q_norm_g, w_uq, kv_norm_g, w_ukv, qn_g, kn_g, b, s):
    cosp, sinp = _rope_patterns(positions, MLA_ROPE, MLA_NOPE, LANES)
    q, k, v = _mla_prep(c_q, c_kv, k_r, cosp, sinp, q_norm_g, w_uq, kv_norm_g, w_ukv, qn_g, kn_g)
    o = _mla_attn(q.reshape(b, s, -1), k.reshape(b, s, -1), v.reshape(b, s, -1))
    return o.reshape(b * s, -1)


def _retention_kernel(q_ref, k_ref, v_ref, g_ref, cos_ref, sin_ref, decay_ref, xi_ref, zeta_ref,
                      cd_ref, gn_ref, o_ref):
    L = RET_CHUNK
    n_chunks = q_ref.shape[1] // L
    lane = lax.broadcasted_iota(jnp.int32, (L, LANES), 1)

    def chunk_body(c, states):
        rows = pl.ds(pl.multiple_of(c * L, L), L)
        cosp, sinp = cos_ref[0, rows, :], sin_ref[0, rows, :]
        qc = _rotate_half(q_ref[0, rows, :], cosp, sinp, RET_DK, 0, RET_DK)
        kc = _rotate_half(k_ref[0, rows, :], cosp, sinp, RET_DK, 0, RET_DK) * (RET_DK ** -0.5)
        qx = qc * xi_ref[0]
        kz = kc * zeta_ref[0]
        new_states = []
        for a in range(2):
            mine = (lane >= a * RET_DK) & (lane < (a + 1) * RET_DK)
            qa = jnp.where(mine, qc, 0.0).astype(BF16)
            ka = jnp.where(mine, kc, 0.0).astype(BF16)
            va = v_ref[0, rows, a * RET_DV:(a + 1) * RET_DV]
            inner = lax.dot_general(qa, ka, (((1,), (1,)), ((), ())),
                                    preferred_element_type=F32) * decay_ref[a]
            cross = jnp.dot(jnp.where(mine, qx, 0.0).astype(BF16), states[a].astype(BF16),
                            preferred_element_type=F32)
            o = jnp.dot(inner.astype(BF16), va, preferred_element_type=F32) + cross
            kzt = jnp.where(mine, kz, 0.0).T.astype(BF16)
            incr = jnp.dot(kzt, va, preferred_element_type=F32)
            new_states.append(cd_ref[a, 0:1, :] * states[a] + incr)
            o = _norm(o, gn_ref[a])
            gate = g_ref[0, rows, a * RET_DV:(a + 1) * RET_DV].astype(F32)
            o_ref[0, rows, a * RET_DV:(a + 1) * RET_DV] = (o * gate * jax.nn.sigmoid(gate)).astype(o_ref.dtype)
        return tuple(new_states)

    zero = jnp.zeros((LANES, RET_DV), F32)
    lax.fori_loop(0, n_chunks, chunk_body, (zero, zero))


def _retention_branch(r_q, r_k, r_v, r_g, positions, gn_g):
    b, s, _ = r_q.shape
    L = RET_CHUNK
    cosp, sinp = _rope_patterns(positions, RET_DK, 0, RET_DK)
    cosp, sinp = cosp.reshape(b, s, LANES), sinp.reshape(b, s, LANES)
    log_gamma = jnp.log(1.0 - 2.0 ** (-5.0 - jnp.arange(RET_HEADS, dtype=F32)))
    idx = jnp.arange(L, dtype=F32)
    diff = idx[:, None] - idx[None, :]
    decay = jnp.where(diff >= 0, jnp.exp(log_gamma[:, None, None] * jnp.maximum(diff, 0.0)), 0.0)
    xi = jnp.exp(log_gamma[:, None] * (idx + 1.0))
    zeta = jnp.exp(log_gamma[:, None] * (L - 1.0 - idx))
    per_pair = lambda a: jnp.repeat(a.reshape(RET_HEADS // 2, 2, L), RET_DK, axis=1).transpose(0, 2, 1)
    cd = jnp.broadcast_to(jnp.exp(log_gamma * L)[:, None, None], (RET_HEADS, 8, RET_DV))
    gn = gn_g.reshape(RET_HEADS, 1, RET_DV)
    n_pairs = RET_HEADS // 2
    seq = lambda width: pl.BlockSpec((1, s, width), lambda bi, hp: (bi, 0, hp))
    return pl.pallas_call(
        _retention_kernel,
        grid=(b, n_pairs),
        in_specs=[seq(LANES), seq(LANES), seq(2 * RET_DV), seq(2 * RET_DV),
                  pl.BlockSpec((1, s, LANES), lambda bi, hp: (bi, 0, 0)),
                  pl.BlockSpec((1, s, LANES), lambda bi, hp: (bi, 0, 0)),
                  pl.BlockSpec((2, L, L), lambda bi, hp: (hp, 0, 0)),
                  pl.BlockSpec((1, L, LANES), lambda bi, hp: (hp, 0, 0)),
                  pl.BlockSpec((1, L, LANES), lambda bi, hp: (hp, 0, 0)),
                  pl.BlockSpec((2, 8, RET_DV), lambda bi, hp: (hp, 0, 0)),
                  pl.BlockSpec((2, 1, RET_DV), lambda bi, hp: (hp, 0, 0))],
        out_specs=seq(2 * RET_DV),
        out_shape=jax.ShapeDtypeStruct((b, s, RET_HEADS * RET_DV), BF16),
        compiler_params=pltpu.CompilerParams(dimension_semantics=("parallel", "parallel"),
                                             vmem_limit_bytes=VMEM_LIMIT),
        name="retention",
    )(r_q, r_k, r_v, r_g, cosp, sinp, decay, per_pair(xi), per_pair(zeta), cd, gn)


def _mem_kv_kernel(mem_ref, g_ref, w_ref, kn_ref, k_ref, v_ref):
    kv = jnp.dot(_norm(mem_ref[0], g_ref[...]).astype(BF16), w_ref[...], preferred_element_type=F32)
    width = X_HEADS * X_DH
    for hd in range(X_HEADS):
        sl = slice(hd * X_DH, (hd + 1) * X_DH)
        k_ref[0, :, sl] = _norm(kv[:, sl], kn_ref[...]).astype(k_ref.dtype)
    v_ref[0] = kv[:, width:].astype(v_ref.dtype)


def _mem_kv(mem, mem_g, w_mem_kv, kn_g):
    b, m, d = mem.shape
    width = X_HEADS * X_DH
    w4 = w_mem_kv.reshape(d, X_HEADS, 2, X_DH)
    w = jnp.concatenate([w4[:, :, 0].reshape(d, width), w4[:, :, 1].reshape(d, width)], axis=1).astype(BF16)
    consts = [mem_g.reshape(1, d), w, kn_g.reshape(1, X_DH)]
    blk = pl.BlockSpec((1, m, width), lambda i: (i, 0, 0))
    return pl.pallas_call(
        _mem_kv_kernel,
        grid=(b,),
        in_specs=[pl.BlockSpec((1, m, d), lambda i: (i, 0, 0))] + _const_specs(consts),
        out_specs=[blk, blk],
        out_shape=[jax.ShapeDtypeStruct((b, m, width), BF16)] * 2,
        compiler_params=pltpu.CompilerParams(dimension_semantics=("parallel",),
                                             vmem_limit_bytes=VMEM_LIMIT),
        name="mem_kv",
    )(mem, *consts)


def _mem_attn_kernel(q_ref, k_ref, v_ref, qn_ref, o_ref):
    for hd in range(X_HEADS):
        sl = slice(hd * X_DH, (hd + 1) * X_DH)
        q = (_norm(q_ref[0, :, sl].astype(F32), qn_ref[...]) * (X_DH ** -0.5)).astype(BF16)
        s = lax.dot_general(q, k_ref[0, :, sl], (((1,), (1,)), ((), ())), preferred_element_type=F32)
        p = jnp.exp(s - jnp.max(s, axis=-1, keepdims=True))
        o = jnp.dot(p.astype(BF16), v_ref[0, :, sl], preferred_element_type=F32)
        o_ref[0, :, sl] = (o / jnp.sum(p, axis=-1, keepdims=True)).astype(o_ref.dtype)


def _mem_attn(x_q, k, v, qn_g, tm=512):
    b, s, width = x_q.shape
    m = k.shape[1]
    kv_spec = pl.BlockSpec((1, m, width), lambda bi, i: (bi, 0, 0))
    return pl.pallas_call(
        _mem_attn_kernel,
        grid=(b, s // tm),
        in_specs=[pl.BlockSpec((1, tm, width), lambda bi, i: (bi, i, 0)), kv_spec, kv_spec,
                  pl.BlockSpec((1, X_DH), lambda bi, i: (0, 0))],
        out_specs=pl.BlockSpec((1, tm, width), lambda bi, i: (bi, i, 0)),
        out_shape=jax.ShapeDtypeStruct((b, s, width), BF16),
        compiler_params=pltpu.CompilerParams(dimension_semantics=("parallel", "parallel"),
                                             vmem_limit_bytes=VMEM_LIMIT),
        name="mem_attn",
    )(x_q, k, v, qn_g.reshape(1, X_DH))


def _merge_kernel(x_ref, gates_ref, oa_ref, ob_ref, oc_ref, wa_ref, wb_ref, wc_ref, wout_ref, h_ref):
    merged = None
    for i, (o_ref, w_ref) in enumerate(((oa_ref, wa_ref), (ob_ref, wb_ref), (oc_ref, wc_ref))):
        y = jnp.dot(o_ref[...], w_ref[...], preferred_element_type=F32)
        gate = jax.nn.sigmoid(gates_ref[:, i * D_MODEL:(i + 1) * D_MODEL].astype(F32))
        merged = gate * y if merged is None else merged + gate * y
    h_ref[...] = x_ref[...] + jnp.dot(merged.astype(BF16), wout_ref[...], preferred_element_type=F32)


def _merge(x2d, gates, o_mla, o_ret, o_mem, w_o_mla, w_o_ret, w_o_cross, w_out, tm=512):
    t, d = x2d.shape
    rows = [x2d, gates, o_mla, o_ret, o_mem]
    consts = [w.astype(BF16) for w in (w_o_mla, w_o_ret, w_o_cross, w_out)]
    return pl.pallas_call(
        _merge_kernel,
        grid=(t // tm,),
        in_specs=_row_specs(rows, tm) + _const_specs(consts),
        out_specs=pl.BlockSpec((tm, d), lambda i: (i, 0)),
        out_shape=jax.ShapeDtypeStruct((t, d), F32),
        compiler_params=pltpu.CompilerParams(dimension_semantics=("parallel",),
                                             vmem_limit_bytes=VMEM_LIMIT),
        name="merge",
    )(*rows, *consts)


PEER_SLOTS = PEER_HEADS * PEER_TOPK
PAIR_CANDS = [(i, j) for i in range(PEER_TOPK) for j in range(PEER_TOPK)
              if (i + 1) * (j + 1) <= PEER_TOPK]
N_CANDS = len(PAIR_CANDS)
NEG_INF = float("-inf")


def _top_rounds(s, n_rounds):
    rows = lax.broadcasted_iota(jnp.int32, s.shape, 0)
    vals, idxs = [], []
    for _ in range(n_rounds):
        m = jnp.max(s, axis=0, keepdims=True)
        idx = jnp.min(jnp.where(s == m, rows, s.shape[0]), axis=0, keepdims=True)
        s = jnp.where(rows == idx, NEG_INF, s)
        vals.append(m)
        idxs.append(idx)
    return vals, idxs


def _peer_select_kernel(h_ref, g_ref, wq_ref, keys_ref, xn_ref, exp_ref, gate_ref, q_ref):
    xn = _norm(h_ref[...], g_ref[...])
    xn_ref[...] = xn
    xnb = xn.astype(BF16)
    q_ref[...] = jnp.dot(xnb, wq_ref[...], preferred_element_type=F32).astype(BF16)
    tb = xnb.shape[0]
    tok = (tb // LANES, LANES)

    def head_body(hd, carry):
        sv, si = [], []
        for c in range(2):
            col = pl.multiple_of(hd * PEER_DQ + c * PEER_DHALF, PEER_DHALF)
            qh = q_ref[:, pl.ds(col, PEER_DHALF)]
            st = lax.dot_general(keys_ref[hd, c], qh, (((1,), (1,)), ((), ())),
                                 preferred_element_type=F32)
            v, i = _top_rounds(st.reshape((PEER_N_KEYS,) + tok), PEER_TOPK)
            sv.append(v)
            si.append(i)
        cand = jnp.concatenate([sv[0][i] + sv[1][j] for i, j in PAIR_CANDS], axis=0)
        cand_e = jnp.concatenate([si[0][i] * PEER_N_KEYS + si[1][j] for i, j in PAIR_CANDS], axis=0)
        rows = lax.broadcasted_iota(jnp.int32, cand.shape, 0)
        best, experts = [], []
        for _ in range(PEER_TOPK):
            m = jnp.max(cand, axis=0, keepdims=True)
            pos = jnp.min(jnp.where(cand == m, rows, N_CANDS), axis=0, keepdims=True)
            hit = rows == pos
            experts.append(jnp.sum(jnp.where(hit, cand_e, 0), axis=0, keepdims=True))
            cand = jnp.where(hit, NEG_INF, cand)
            best.append(m)
        p = [jnp.exp(b - best[0]) for b in best]
        denom = p[0]
        for pk in p[1:]:
            denom = denom + pk
        row0 = pl.multiple_of(hd * PEER_TOPK, PEER_TOPK)
        exp_ref[pl.ds(row0, PEER_TOPK)] = jnp.concatenate(experts, axis=0)
        gate_ref[pl.ds(row0, PEER_TOPK)] = jnp.concatenate([pk / denom for pk in p], axis=0)
        return carry

    lax.fori_loop(0, PEER_HEADS, head_body, 0)


def _peer_select(h2d, g, w_q, sub_keys, tb=1024):
    t, d = h2d.shape
    consts = [g.reshape(1, d), w_q.astype(BF16), sub_keys.astype(BF16)]
    slot_spec = pl.BlockSpec((PEER_SLOTS, tb // LANES, LANES), lambda i: (0, i, 0))
    xn, experts, gates = pl.pallas_call(
        _peer_select_kernel,
        grid=(t // tb,),
        in_specs=_row_specs([h2d], tb) + _const_specs(consts),
        out_specs=[pl.BlockSpec((tb, d), lambda i: (i, 0)), slot_spec, slot_spec],
        out_shape=[
            jax.ShapeDtypeStruct((t, d), F32),
            jax.ShapeDtypeStruct((PEER_SLOTS, t // LANES, LANES), jnp.int32),
            jax.ShapeDtypeStruct((PEER_SLOTS, t // LANES, LANES), F32),
        ],
        scratch_shapes=[pltpu.VMEM((tb, PEER_HEADS * PEER_DQ), BF16)],
        compiler_params=pltpu.CompilerParams(dimension_semantics=("parallel",),
                                             vmem_limit_bytes=VMEM_LIMIT),
        name="peer_select",
    )(h2d, *consts)
    return xn, experts.reshape(PEER_SLOTS, t), gates.reshape(PEER_SLOTS, t)


ROW_TILE = D_MODEL // LANES
PAIR_ROWS = 2 * ROW_TILE


def _pack_expert_table(tab):
    n, d = tab.shape
    t = tab.astype(BF16).reshape(2, n // 2, ROW_TILE, LANES)
    bits = lax.bitcast_convert_type(t, jnp.uint16).astype(jnp.uint32)
    return (bits[0] | (bits[1] << 16)).reshape(n // 2 * ROW_TILE, LANES)


TOKENS_PER_ITER = 8


def _gather_tiles(idx_ref, tab_ref, g_ref, t):
    for k in range(PEER_SLOTS):
        start = pl.multiple_of(idx_ref[t, k], ROW_TILE)
        g_ref[k * ROW_TILE:(k + 1) * ROW_TILE, :] = tab_ref[pl.ds(start, ROW_TILE), :]


def _peer_u_kernel(idx_ref, x_ref, hi_ref, gate_ref, fold_ref, tab_ref, w_ref, sel_ref, *g_refs):
    tb = x_ref.shape[0]
    shape = (ROW_TILE, PEER_SLOTS * PAIR_ROWS)
    row_in_tile = lax.broadcasted_iota(jnp.int32, shape, 1) & (PAIR_ROWS - 1)
    chunk = lax.broadcasted_iota(jnp.int32, shape, 0)
    keep = (row_in_tile >> 1) == chunk

    def token_pair(i, carry):
        for j, g_ref in enumerate(g_refs):
            t = i * TOKENS_PER_ITER + j
            _gather_tiles(idx_ref, tab_ref, g_ref, t)
            x8 = x_ref[pl.ds(t, 1), :].reshape(ROW_TILE, LANES)
            x16 = jnp.concatenate([x8, jnp.zeros_like(x8)], axis=0).astype(BF16)
            out = lax.dot_general(x16, pltpu.bitcast(g_ref[...], BF16), (((1,), (1,)), ((), ())),
                                  preferred_element_type=F32)
            sel_ref[pl.ds(pl.multiple_of(t * ROW_TILE, ROW_TILE), ROW_TILE), :] = (
                jnp.where(keep, out[:ROW_TILE], 0.0))
        return carry

    lax.fori_loop(0, tb // TOKENS_PER_ITER, token_pair, 0)
    sel = sel_ref[...]
    sel_hi = sel.astype(BF16)
    sel_lo = (sel - sel_hi.astype(F32)).astype(BF16)
    r = (jnp.dot(sel_hi, fold_ref[...], preferred_element_type=F32)
         + jnp.dot(sel_lo, fold_ref[...], preferred_element_type=F32))
    r = jnp.sum(r.reshape(tb, ROW_TILE, 2 * PEER_SLOTS), axis=1)
    a = jnp.where(hi_ref[...] == 1, r[:, PEER_SLOTS:], r[:, :PEER_SLOTS])
    w_ref[...] = gate_ref[...] * jax.nn.gelu(a)


def _peer_u(idx, xn, hi, gate, tab, tb):
    t = idx.shape[0]
    eye = jnp.eye(PEER_SLOTS, dtype=BF16)
    parity = (jnp.arange(PAIR_ROWS) & 1).astype(BF16)
    fold = jnp.concatenate([jnp.kron(eye, (1 - parity)[:, None]), jnp.kron(eye, parity[:, None])], axis=1)
    g_shape = pltpu.VMEM((PEER_SLOTS * ROW_TILE, LANES), jnp.uint32)
    return pl.pallas_call(
        _peer_u_kernel,
        grid=(t // tb,),
        in_specs=[
            pl.BlockSpec((tb, PEER_SLOTS), lambda i: (i, 0), memory_space=pltpu.SMEM),
            pl.BlockSpec((tb, D_MODEL), lambda i: (i, 0)),
            pl.BlockSpec((tb, PEER_SLOTS), lambda i: (i, 0)),
            pl.BlockSpec((tb, PEER_SLOTS), lambda i: (i, 0)),
            pl.BlockSpec(fold.shape, lambda i: (0, 0), pipeline_mode=pl.Buffered(1)),
            pl.BlockSpec(tab.shape, lambda i: (0, 0), pipeline_mode=pl.Buffered(1)),
        ],
        out_specs=pl.BlockSpec((tb, PEER_SLOTS), lambda i: (i, 0)),
        out_shape=jax.ShapeDtypeStruct((t, PEER_SLOTS), F32),
        scratch_shapes=[pltpu.VMEM((tb * ROW_TILE, PEER_SLOTS * PAIR_ROWS), F32)]
        + [g_shape] * TOKENS_PER_ITER,
        compiler_params=pltpu.CompilerParams(dimension_semantics=("arbitrary",),
                                             vmem_limit_bytes=PEER_VMEM_LIMIT),
        name="peer_u",
    )(idx, xn, hi, gate, fold, tab)


def _peer_v_kernel(idx_ref, hi_ref, w_ref, h_ref, route_ref, tab_ref, o_ref, wx_ref, *g_refs):
    tb = hi_ref.shape[0]
    w = w_ref[...]
    w_hi = jnp.where(hi_ref[...] == 1, w, 0.0)
    w2 = jnp.concatenate([w - w_hi, w_hi], axis=1).astype(BF16)
    wx_ref[...] = jnp.dot(w2, route_ref[...], preferred_element_type=F32)
    shape = (PAIR_ROWS, PEER_SLOTS * PAIR_ROWS)
    row_in_tile = lax.broadcasted_iota(jnp.int32, shape, 1) & (PAIR_ROWS - 1)
    chunk = lax.broadcasted_iota(jnp.int32, shape, 0)
    keep = (row_in_tile >> 1) == chunk

    def token_pair(i, carry):
        for j, g_ref in enumerate(g_refs):
            t = i * TOKENS_PER_ITER + j
            _gather_tiles(idx_ref, tab_ref, g_ref, t)
            lhs = jnp.where(keep, wx_ref[pl.ds(t, 1), :], 0.0).astype(BF16)
            o = jnp.dot(lhs, pltpu.bitcast(g_ref[...], BF16), preferred_element_type=F32)
            o_ref[pl.ds(t, 1), :] = h_ref[pl.ds(t, 1), :] + o[:ROW_TILE].reshape(1, D_MODEL)
        return carry

    lax.fori_loop(0, tb // TOKENS_PER_ITER, token_pair, 0)


def _peer_v(idx, hi, w, h2d, tab, tb):
    t = idx.shape[0]
    eye = jnp.eye(PEER_SLOTS, dtype=BF16)
    parity = (jnp.arange(PAIR_ROWS) & 1).astype(BF16)
    route = jnp.concatenate([jnp.kron(eye, (1 - parity)[None, :]), jnp.kron(eye, parity[None, :])], axis=0)
    return pl.pallas_call(
        _peer_v_kernel,
        grid=(t // tb,),
        in_specs=[
            pl.BlockSpec((tb, PEER_SLOTS), lambda i: (i, 0), memory_space=pltpu.SMEM),
            pl.BlockSpec((tb, PEER_SLOTS), lambda i: (i, 0)),
            pl.BlockSpec((tb, PEER_SLOTS), lambda i: (i, 0)),
            pl.BlockSpec((tb, D_MODEL), lambda i: (i, 0)),
            pl.BlockSpec(route.shape, lambda i: (0, 0), pipeline_mode=pl.Buffered(1)),
            pl.BlockSpec(tab.shape, lambda i: (0, 0), pipeline_mode=pl.Buffered(1)),
        ],
        out_specs=pl.BlockSpec((tb, D_MODEL), lambda i: (i, 0)),
        out_shape=jax.ShapeDtypeStruct((t, D_MODEL), F32),
        scratch_shapes=[pltpu.VMEM((tb, PEER_SLOTS * PAIR_ROWS), F32)]
        + [pltpu.VMEM((PEER_SLOTS * ROW_TILE, LANES), jnp.uint32)] * TOKENS_PER_ITER,
        compiler_params=pltpu.CompilerParams(dimension_semantics=("arbitrary",),
                                             vmem_limit_bytes=PEER_VMEM_LIMIT),
        name="peer_v",
    )(idx, hi, w, h2d, route, tab)


def _peer_experts(h2d, xn2d, experts, gate, u_tab, v_tab, tb_u=64, tb_v=64):
    t, d = xn2d.shape
    half = u_tab.shape[0] // 2
    hi = (experts >= half).astype(jnp.int32)
    idx = (experts - hi * half) * ROW_TILE
    w = _peer_u(idx, xn2d, hi, gate, _pack_expert_table(u_tab), tb_u)
    return _peer_v(idx, hi, w, h2d, _pack_expert_table(v_tab), tb_v)


def kernel(x, mem, positions, attn_norm_g, w_in, mla_q_norm_g, w_uq, mla_kv_norm_g, w_ukv,
           mla_qn_g, mla_kn_g, w_o_mla, ret_gn_g, w_o_ret, mem_norm_g, w_mem_kv, x_qn_g, x_kn_g,
           w_o_cross, w_out, ffn_norm_g, peer_w_q, peer_keys, peer_u, peer_v):
    B, S, D = x.shape
    T = B * S
    h2d = x.reshape(T, D)
    offs = [0]
    for sz in IN_SPLITS:
        offs.append(offs[-1] + sz)
    for layer in range(w_in.shape[0]):
        w_pieces = [w_in[layer][:, offs[i]:offs[i + 1]].astype(BF16) for i in range(len(IN_SPLITS))]
        out_dtypes = [F32, F32, F32, F32, F32, BF16, BF16, BF16, BF16]
        c_q, c_kv, k_r, r_q, r_k, r_v, r_g, x_q, gates = _in_proj(h2d, attn_norm_g[layer], w_pieces, out_dtypes)
        o_mla = _mla_branch(c_q, c_kv, k_r, positions, mla_q_norm_g[layer], w_uq[layer],
                            mla_kv_norm_g[layer], w_ukv[layer], mla_qn_g[layer], mla_kn_g[layer], B, S)
        seq = lambda a: a.reshape(B, S, -1)
        o_ret = _retention_branch(seq(r_q), seq(r_k), seq(r_v), seq(r_g), positions, ret_gn_g[layer])
        mk, mv = _mem_kv(mem, mem_norm_g[layer], w_mem_kv[layer], x_kn_g[layer])
        o_mem = _mem_attn(seq(x_q), mk, mv, x_qn_g[layer])
        h2d = _merge(h2d, gates, o_mla, o_ret.reshape(T, -1), o_mem.reshape(T, -1),
                     w_o_mla[layer], w_o_ret[layer], w_o_cross[layer], w_out[layer])
        xn2d, experts_t, gate_t = _peer_select(h2d, ffn_norm_g[layer], peer_w_q[layer], peer_keys[layer])
        h2d = _peer_experts(h2d, xn2d, experts_t.T, gate_t.T, peer_u[layer], peer_v[layer])
    return h2d.reshape(B, S, D)
```

```python
import functools

import jax
import jax.numpy as jnp
from jax import lax
from jax.experimental import pallas as pl
from jax.experimental.pallas import tpu as pltpu

D_MODEL = 1024
MEM_LEN = 256
MLA_HEADS = 8
MLA_Q_RANK = 384
MLA_KV_RANK = 128
MLA_NOPE = 64
MLA_ROPE = 32
MLA_V = 64
RET_HEADS = 4
RET_DK = 64
RET_DV = 128
RET_CHUNK = 128
X_HEADS = 4
X_DH = 128
N_BRANCH = 3
PEER_HEADS = 8
PEER_N_KEYS = 128
PEER_TOPK = 16
PEER_DQ = 256
PEER_DHALF = PEER_DQ // 2
ROPE_BASE = 10000.0
EPS = 1e-6
IN_SPLITS = (MLA_Q_RANK, MLA_KV_RANK, MLA_ROPE,
             RET_HEADS * RET_DK, RET_HEADS * RET_DK, RET_HEADS * RET_DV, RET_HEADS * RET_DV,
             X_HEADS * X_DH, N_BRANCH * D_MODEL)

LANES = 128
VMEM_LIMIT = 48 * 1024 * 1024
PEER_VMEM_LIMIT = 56 * 1024 * 1024
BF16 = jnp.bfloat16
F32 = jnp.float32


def _norm(x, g):
    return x * lax.rsqrt(jnp.mean(x * x, axis=-1, keepdims=True) + EPS) * g


def _row_specs(arrays, tm):
    return [pl.BlockSpec((tm, a.shape[1]), lambda i: (i, 0)) for a in arrays]


def _const_specs(arrays):
    return [pl.BlockSpec(a.shape, lambda i, n=a.ndim: (0,) * n) for a in arrays]


def _in_proj_kernel(x_ref, g_ref, *refs):
    n_w = len(refs) // 2
    w_refs, o_refs = refs[:n_w], refs[n_w:]
    nb = _norm(x_ref[...], g_ref[...]).astype(BF16)
    for w_ref, o_ref in zip(w_refs, o_refs):
        o_ref[...] = jnp.dot(nb, w_ref[...], preferred_element_type=F32).astype(o_ref.dtype)


def _in_proj(x2d, g, w_pieces, out_dtypes, tm=512):
    t, d = x2d.shape
    consts = [g.reshape(1, d)] + list(w_pieces)
    return pl.pallas_call(
        _in_proj_kernel,
        grid=(t // tm,),
        in_specs=_row_specs([x2d], tm) + _const_specs(consts),
        out_specs=[pl.BlockSpec((tm, w.shape[1]), lambda i: (i, 0)) for w in w_pieces],
        out_shape=[jax.ShapeDtypeStruct((t, w.shape[1]), dt) for w, dt in zip(w_pieces, out_dtypes)],
        compiler_params=pltpu.CompilerParams(dimension_semantics=("parallel",),
                                             vmem_limit_bytes=VMEM_LIMIT),
        name="in_proj",
    )(x2d, *consts)


def _rope_patterns(positions, n_rope, lead, width):
    inv_freq = ROPE_BASE ** (-jnp.arange(0, n_rope, 2, dtype=F32) / n_rope)
    ang = positions.astype(F32).reshape(-1, 1) * inv_freq
    cos, sin = jnp.cos(ang), jnp.sin(ang)
    t = ang.shape[0]
    tail = width - lead - n_rope
    cosp = jnp.concatenate([jnp.ones((t, lead)), cos, cos, jnp.zeros((t, tail))], axis=1)
    sinp = jnp.concatenate([jnp.zeros((t, lead)), -sin, sin, jnp.zeros((t, tail))], axis=1)
    rep = LANES // width
    return jnp.tile(cosp, (1, rep)), jnp.tile(sinp, (1, rep))


def _rotate_half(x, cosp, sinp, n_rope, lead, width):
    half = n_rope // 2
    lane = lax.broadcasted_iota(jnp.int32, x.shape, 1) & (width - 1)
    partner = jnp.where(lane < lead + half, pltpu.roll(x, LANES - half, 1), pltpu.roll(x, half, 1))
    return x * cosp + partner * sinp


MLA_QK = MLA_NOPE + MLA_ROPE


def _pad_head_cols(w, per_head):
    r = w.shape[0]
    w = w.reshape(r, MLA_HEADS, per_head)
    return jnp.pad(w, ((0, 0), (0, 0), (0, LANES - per_head))).reshape(r, MLA_HEADS * LANES)


def _mla_prep_kernel(cq_ref, ckv_ref, kr_ref, cos_ref, sin_ref, gq_ref, gkv_ref, wuq_ref, wuk_ref,
                     wuv_ref, qn_ref, kn_ref, q_ref, k_ref, v_ref):
    cqn = _norm(cq_ref[...], gq_ref[...]).astype(BF16)
    ckvn = _norm(ckv_ref[...], gkv_ref[...]).astype(BF16)
    qp = jnp.dot(cqn, wuq_ref[...], preferred_element_type=F32)
    kp = jnp.dot(ckvn, wuk_ref[...], preferred_element_type=F32)
    v_ref[...] = jnp.dot(ckvn, wuv_ref[...], preferred_element_type=F32).astype(v_ref.dtype)
    tm = qp.shape[0]
    kr = jnp.concatenate([jnp.zeros((tm, MLA_NOPE), F32), kr_ref[...],
                          jnp.zeros((tm, LANES - MLA_QK), F32)], axis=1)
    cosp, sinp = cos_ref[...], sin_ref[...]

    def qk_norm_rope(xh, g, scale):
        ss = jnp.sum(xh * xh, axis=-1, keepdims=True) * (1.0 / MLA_QK)
        xn = xh * lax.rsqrt(ss + EPS) * g
        return _rotate_half(xn, cosp, sinp, MLA_ROPE, MLA_NOPE, LANES) * scale

    for hd in range(MLA_HEADS):
        sl = slice(hd * LANES, (hd + 1) * LANES)
        q_ref[:, sl] = qk_norm_rope(qp[:, sl], qn_ref[...], MLA_QK ** -0.5).astype(q_ref.dtype)
        k_ref[:, sl] = qk_norm_rope(kp[:, sl] + kr, kn_ref[...], 1.0).astype(k_ref.dtype)


def _mla_prep(c_q, c_kv, k_r, cosp, sinp, q_norm_g, w_uq, kv_norm_g, w_ukv, qn_g, kn_g, tm=512):
    t = c_q.shape[0]
    w_ukv3 = w_ukv.reshape(MLA_KV_RANK, MLA_HEADS, MLA_NOPE + MLA_V)
    w_uk = _pad_head_cols(w_ukv3[:, :, :MLA_NOPE].reshape(MLA_KV_RANK, -1), MLA_NOPE).astype(BF16)
    w_uv = w_ukv3[:, :, MLA_NOPE:].reshape(MLA_KV_RANK, MLA_HEADS * MLA_V).astype(BF16)
    w_uqp = _pad_head_cols(w_uq, MLA_QK).astype(BF16)
    pad_g = lambda g: jnp.pad(g, (0, LANES - MLA_QK)).reshape(1, LANES)
    consts = [q_norm_g.reshape(1, -1), kv_norm_g.reshape(1, -1), w_uqp, w_uk, w_uv, pad_g(qn_g), pad_g(kn_g)]
    rows = [c_q, c_kv, k_r, cosp, sinp]
    wide = MLA_HEADS * LANES
    return pl.pallas_call(
        _mla_prep_kernel,
        grid=(t // tm,),
        in_specs=_row_specs(rows, tm) + _const_specs(consts),
        out_specs=[pl.BlockSpec((tm, wide), lambda i: (i, 0)),
                   pl.BlockSpec((tm, wide), lambda i: (i, 0)),
                   pl.BlockSpec((tm, MLA_HEADS * MLA_V), lambda i: (i, 0))],
        out_shape=[jax.ShapeDtypeStruct((t, wide), BF16), jax.ShapeDtypeStruct((t, wide), BF16),
                   jax.ShapeDtypeStruct((t, MLA_HEADS * MLA_V), BF16)],
        compiler_params=pltpu.CompilerParams(dimension_semantics=("parallel",),
                                             vmem_limit_bytes=VMEM_LIMIT),
        name="mla_prep",
    )(*rows, *consts)


def _mla_attn_kernel(qt_ref, k_ref, vt_ref, o_ref, *, tile):
    qi = pl.program_id(2)
    key = lax.broadcasted_iota(jnp.int32, (tile, tile), 0)
    qry = lax.broadcasted_iota(jnp.int32, (tile, tile), 1)
    head_lanes = [slice(a * LANES, (a + 1) * LANES) for a in range(2)]
    qts = [qt_ref[0, lanes, :] for lanes in head_lanes]

    def step(j, carry, masked):
        start = pl.multiple_of(j * tile, tile)
        vtj = vt_ref[0, :, pl.ds(start, tile)]
        new = []
        for a, lanes in enumerate(head_lanes):
            m, l, acc = carry[a]
            kj = k_ref[0, pl.ds(start, tile), lanes]
            st = jnp.dot(kj, qts[a], preferred_element_type=F32)
            if masked:
                st = jnp.where(key <= qry, st, -1e30)
            m_new = jnp.maximum(m, jnp.max(st, axis=0, keepdims=True))
            alpha = jnp.exp(m - m_new)
            p = jnp.exp(st - m_new)
            l = alpha * l + jnp.sum(p, axis=0, keepdims=True)
            pv = jnp.dot(vtj, p.astype(BF16), preferred_element_type=F32)
            new.append((m_new, l, alpha * acc + pv))
        return tuple(new)

    init = (jnp.full((1, tile), -1e30, F32), jnp.zeros((1, tile), F32), jnp.zeros((2 * MLA_V, tile), F32))
    carry = lax.fori_loop(0, qi, functools.partial(step, masked=False), (init, init))
    (_, l0, acc0), (_, l1, acc1) = step(qi, carry, True)
    dv_row = lax.broadcasted_iota(jnp.int32, acc0.shape, 0)
    out_t = jnp.where(dv_row < MLA_V, acc0 / l0, acc1 / l1)
    o_ref[0] = out_t.T.astype(o_ref.dtype)


def _mla_attn(q, k, v, tile=512):
    b, s, _ = q.shape
    pair = 2 * LANES
    qt, vt = q.transpose(0, 2, 1), v.transpose(0, 2, 1)
    return pl.pallas_call(
        functools.partial(_mla_attn_kernel, tile=tile),
        grid=(b, MLA_HEADS // 2, s // tile),
        in_specs=[pl.BlockSpec((1, pair, tile), lambda bi, hp, qi: (bi, hp, qi)),
                  pl.BlockSpec((1, s, pair), lambda bi, hp, qi: (bi, 0, hp)),
                  pl.BlockSpec((1, 2 * MLA_V, s), lambda bi, hp, qi: (bi, hp, 0))],
        out_specs=pl.BlockSpec((1, tile, 2 * MLA_V), lambda bi, hp, qi: (bi, qi, hp)),
        out_shape=jax.ShapeDtypeStruct((b, s, MLA_HEADS * MLA_V), BF16),
        compiler_params=pltpu.CompilerParams(dimension_semantics=("parallel", "parallel", "arbitrary"),
                                             vmem_limit_bytes=VMEM_LIMIT),
        name="mla_attn",
    )(qt, k, vt)


def _mla_branch(c_q, c_kv, k_r, positions, q_norm_g, w_uq, kv_norm_g, w_ukv, qn_g, kn_g, b, s):
    cosp, sinp = _rope_patterns(positions, MLA_ROPE, MLA_NOPE, LANES)
    q, k, v = _mla_prep(c_q, c_kv, k_r, cosp, sinp, q_norm_g, w_uq, kv_norm_g, w_ukv, qn_g, kn_g)
    o = _mla_attn(q.reshape(b, s, -1), k.reshape(b, s, -1), v.reshape(b, s, -1))
    return o.reshape(b * s, -1)


def _retention_kernel(q_ref, k_ref, v_ref, g_ref, cos_ref, sin_ref, decay_ref, xi_ref, zeta_ref,
                      cd_ref, gn_ref, o_ref):
    L = RET_CHUNK
    n_chunks = q_ref.shape[1] // L
    lane = lax.broadcasted_iota(jnp.int32, (L, LANES), 1)

    def chunk_body(c, states):
        rows = pl.ds(pl.multiple_of(c * L, L), L)
        cosp, sinp = cos_ref[0, rows, :], sin_ref[0, rows, :]
        qc = _rotate_half(q_ref[0, rows, :], cosp, sinp, RET_DK, 0, RET_DK)
        kc = _rotate_half(k_ref[0, rows, :], cosp, sinp, RET_DK, 0, RET_DK) * (RET_DK ** -0.5)
        qx = qc * xi_ref[0]
        kz = kc * zeta_ref[0]
        new_states = []
        for a in range(2):
            mine = (lane >= a * RET_DK) & (lane < (a + 1) * RET_DK)
            qa = jnp.where(mine, qc, 0.0).astype(BF16)
            ka = jnp.where(mine, kc, 0.0).astype(BF16)
            va = v_ref[0, rows, a * RET_DV:(a + 1) * RET_DV]
            inner = lax.dot_general(qa, ka, (((1,), (1,)), ((), ())),
                                    preferred_element_type=F32) * decay_ref[a]
            cross = jnp.dot(jnp.where(mine, qx, 0.0).astype(BF16), states[a].astype(BF16),
                            preferred_element_type=F32)
            o = jnp.dot(inner.astype(BF16), va, preferred_element_type=F32) + cross
            kzt = jnp.where(mine, kz, 0.0).T.astype(BF16)
            incr = jnp.dot(kzt, va, preferred_element_type=F32)
            new_states.append(cd_ref[a, 0:1, :] * states[a] + incr)
            o = _norm(o, gn_ref[a])
            gate = g_ref[0, rows, a * RET_DV:(a + 1) * RET_DV].astype(F32)
            o_ref[0, rows, a * RET_DV:(a + 1) * RET_DV] = (o * gate * jax.nn.sigmoid(gate)).astype(o_ref.dtype)
        return tuple(new_states)

    zero = jnp.zeros((LANES, RET_DV), F32)
    lax.fori_loop(0, n_chunks, chunk_body, (zero, zero))


def _retention_branch(r_q, r_k, r_v, r_g, positions, gn_g):
    b, s, _ = r_q.shape
    L = RET_CHUNK
    cosp, sinp = _rope_patterns(positions, RET_DK, 0, RET_DK)
    cosp, sinp = cosp.reshape(b, s, LANES), sinp.reshape(b, s, LANES)
    log_gamma = jnp.log(1.0 - 2.0 ** (-5.0 - jnp.arange(RET_HEADS, dtype=F32)))
    idx = jnp.arange(L, dtype=F32)
    diff = idx[:, None] - idx[None, :]
    decay = jnp.where(diff >= 0, jnp.exp(log_gamma[:, None, None] * jnp.maximum(diff, 0.0)), 0.0)
    xi = jnp.exp(log_gamma[:, None] * (idx + 1.0))
    zeta = jnp.exp(log_gamma[:, None] * (L - 1.0 - idx))
    per_pair = lambda a: jnp.repeat(a.reshape(RET_HEADS // 2, 2, L), RET_DK, axis=1).transpose(0, 2, 1)
    cd = jnp.broadcast_to(jnp.exp(log_gamma * L)[:, None, None], (RET_HEADS, 8, RET_DV))
    gn = gn_g.reshape(RET_HEADS, 1, RET_DV)
    n_pairs = RET_HEADS // 2
    seq = lambda width: pl.BlockSpec((1, s, width), lambda bi, hp: (bi, 0, hp))
    return pl.pallas_call(
        _retention_kernel,
        grid=(b, n_pairs),
        in_specs=[seq(LANES), seq(LANES), seq(2 * RET_DV), seq(2 * RET_DV),
                  pl.BlockSpec((1, s, LANES), lambda bi, hp: (bi, 0, 0)),
                  pl.BlockSpec((1, s, LANES), lambda bi, hp: (bi, 0, 0)),
                  pl.BlockSpec((2, L, L), lambda bi, hp: (hp, 0, 0)),
                  pl.BlockSpec((1, L, LANES), lambda bi, hp: (hp, 0, 0)),
                  pl.BlockSpec((1, L, LANES), lambda bi, hp: (hp, 0, 0)),
                  pl.BlockSpec((2, 8, RET_DV), lambda bi, hp: (hp, 0, 0)),
                  pl.BlockSpec((2, 1, RET_DV), lambda bi, hp: (hp, 0, 0))],
        out_specs=seq(2 * RET_DV),
        out_shape=jax.ShapeDtypeStruct((b, s, RET_HEADS * RET_DV), BF16),
        compiler_params=pltpu.CompilerParams(dimension_semantics=("parallel", "parallel"),
                                             vmem_limit_bytes=VMEM_LIMIT),
        name="retention",
    )(r_q, r_k, r_v, r_g, cosp, sinp, decay, per_pair(xi), per_pair(zeta), cd, gn)


def _mem_kv_kernel(mem_ref, g_ref, w_ref, kn_ref, k_ref, v_ref):
    kv = jnp.dot(_norm(mem_ref[0], g_ref[...]).astype(BF16), w_ref[...], preferred_element_type=F32)
    width = X_HEADS * X_DH
    for hd in range(X_HEADS):
        sl = slice(hd * X_DH, (hd + 1) * X_DH)
        k_ref[0, :, sl] = _norm(kv[:, sl], kn_ref[...]).astype(k_ref.dtype)
    v_ref[0] = kv[:, width:].astype(v_ref.dtype)


def _mem_kv(mem, mem_g, w_mem_kv, kn_g):
    b, m, d = mem.shape
    width = X_HEADS * X_DH
    w4 = w_mem_kv.reshape(d, X_HEADS, 2, X_DH)
    w = jnp.concatenate([w4[:, :, 0].reshape(d, width), w4[:, :, 1].reshape(d, width)], axis=1).astype(BF16)
    consts = [mem_g.reshape(1, d), w, kn_g.reshape(1, X_DH)]
    blk = pl.BlockSpec((1, m, width), lambda i: (i, 0, 0))
    return pl.pallas_call(
        _mem_kv_kernel,
        grid=(b,),
        in_specs=[pl.BlockSpec((1, m, d), lambda i: (i, 0, 0))] + _const_specs(consts),
        out_specs=[blk, blk],
        out_shape=[jax.ShapeDtypeStruct((b, m, width), BF16)] * 2,
        compiler_params=pltpu.CompilerParams(dimension_semantics=("parallel",),
                                             vmem_limit_bytes=VMEM_LIMIT),
        name="mem_kv",
    )(mem, *consts)


def _mem_attn_kernel(q_ref, k_ref, v_ref, qn_ref, o_ref):
    for hd in range(X_HEADS):
        sl = slice(hd * X_DH, (hd + 1) * X_DH)
        q = (_norm(q_ref[0, :, sl].astype(F32), qn_ref[...]) * (X_DH ** -0.5)).astype(BF16)
        s = lax.dot_general(q, k_ref[0, :, sl], (((1,), (1,)), ((), ())), preferred_element_type=F32)
        p = jnp.exp(s - jnp.max(s, axis=-1, keepdims=True))
        o = jnp.dot(p.astype(BF16), v_ref[0, :, sl], preferred_element_type=F32)
        o_ref[0, :, sl] = (o / jnp.sum(p, axis=-1, keepdims=True)).astype(o_ref.dtype)


def _mem_attn(x_q, k, v, qn_g, tm=512):
    b, s, width = x_q.shape
    m = k.shape[1]
    kv_spec = pl.BlockSpec((1, m, width), lambda bi, i: (bi, 0, 0))
    return pl.pallas_call(
        _mem_attn_kernel,
        grid=(b, s // tm),
        in_specs=[pl.BlockSpec((1, tm, width), lambda bi, i: (bi, i, 0)), kv_spec, kv_spec,
                  pl.BlockSpec((1, X_DH), lambda bi, i: (0, 0))],
        out_specs=pl.BlockSpec((1, tm, width), lambda bi, i: (bi, i, 0)),
        out_shape=jax.ShapeDtypeStruct((b, s, width), BF16),
        compiler_params=pltpu.CompilerParams(dimension_semantics=("parallel", "parallel"),
                                             vmem_limit_bytes=VMEM_LIMIT),
        name="mem_attn",
    )(x_q, k, v, qn_g.reshape(1, X_DH))


def _merge_kernel(x_ref, gates_ref, oa_ref, ob_ref, oc_ref, wa_ref, wb_ref, wc_ref, wout_ref, h_ref):
    merged = None
    for i, (o_ref, w_ref) in enumerate(((oa_ref, wa_ref), (ob_ref, wb_ref), (oc_ref, wc_ref))):
        y = jnp.dot(o_ref[...], w_ref[...], preferred_element_type=F32)
        gate = jax.nn.sigmoid(gates_ref[:, i * D_MODEL:(i + 1) * D_MODEL].astype(F32))
        merged = gate * y if merged is None else merged + gate * y
    h_ref[...] = x_ref[...] + jnp.dot(merged.astype(BF16), wout_ref[...], preferred_element_type=F32)


def _merge(x2d, gates, o_mla, o_ret, o_mem, w_o_mla, w_o_ret, w_o_cross, w_out, tm=512):
    t, d = x2d.shape
    rows = [x2d, gates, o_mla, o_ret, o_mem]
    consts = [w.astype(BF16) for w in (w_o_mla, w_o_ret, w_o_cross, w_out)]
    return pl.pallas_call(
        _merge_kernel,
        grid=(t // tm,),
        in_specs=_row_specs(rows, tm) + _const_specs(consts),
        out_specs=pl.BlockSpec((tm, d), lambda i: (i, 0)),
        out_shape=jax.ShapeDtypeStruct((t, d), F32),
        compiler_params=pltpu.CompilerParams(dimension_semantics=("parallel",),
                                             vmem_limit_bytes=VMEM_LIMIT),
        name="merge",
    )(*rows, *consts)


PEER_SLOTS = PEER_HEADS * PEER_TOPK
PAIR_CANDS = [(i, j) for i in range(PEER_TOPK) for j in range(PEER_TOPK)
              if (i + 1) * (j + 1) <= PEER_TOPK]
N_CANDS = len(PAIR_CANDS)
NEG_INF = float("-inf")


def _top_rounds(s, n_rounds):
    rows = lax.broadcasted_iota(jnp.int32, s.shape, 0)
    vals, idxs = [], []
    for _ in range(n_rounds):
        m = jnp.max(s, axis=0, keepdims=True)
        idx = jnp.min(jnp.where(s == m, rows, s.shape[0]), axis=0, keepdims=True)
        s = jnp.where(rows == idx, NEG_INF, s)
        vals.append(m)
        idxs.append(idx)
    return vals, idxs


def _peer_select_kernel(h_ref, g_ref, wq_ref, keys_ref, xn_ref, exp_ref, gate_ref, q_ref):
    xn = _norm(h_ref[...], g_ref[...])
    xn_ref[...] = xn
    xnb = xn.astype(BF16)
    q_ref[...] = jnp.dot(xnb, wq_ref[...], preferred_element_type=F32).astype(BF16)
    tb = xnb.shape[0]
    tok = (tb // LANES, LANES)

    def head_body(hd, carry):
        sv, si = [], []
        for c in range(2):
            col = pl.multiple_of(hd * PEER_DQ + c * PEER_DHALF, PEER_DHALF)
            qh = q_ref[:, pl.ds(col, PEER_DHALF)]
            st = lax.dot_general(keys_ref[hd, c], qh, (((1,), (1,)), ((), ())),
                                 preferred_element_type=F32)
            v, i = _top_rounds(st.reshape((PEER_N_KEYS,) + tok), PEER_TOPK)
            sv.append(v)
            si.append(i)
        cand = jnp.concatenate([sv[0][i] + sv[1][j] for i, j in PAIR_CANDS], axis=0)
        cand_e = jnp.concatenate([si[0][i] * PEER_N_KEYS + si[1][j] for i, j in PAIR_CANDS], axis=0)
        rows = lax.broadcasted_iota(jnp.int32, cand.shape, 0)
        best, experts = [], []
        for _ in range(PEER_TOPK):
            m = jnp.max(cand, axis=0, keepdims=True)
            pos = jnp.min(jnp.where(cand == m, rows, N_CANDS), axis=0, keepdims=True)
            hit = rows == pos
            experts.append(jnp.sum(jnp.where(hit, cand_e, 0), axis=0, keepdims=True))
            cand = jnp.where(hit, NEG_INF, cand)
            best.append(m)
        p = [jnp.exp(b - best[0]) for b in best]
        denom = p[0]
        for pk in p[1:]:
            denom = denom + pk
        row0 = pl.multiple_of(hd * PEER_TOPK, PEER_TOPK)
        exp_ref[pl.ds(row0, PEER_TOPK)] = jnp.concatenate(experts, axis=0)
        gate_ref[pl.ds(row0, PEER_TOPK)] = jnp.concatenate([pk / denom for pk in p], axis=0)
        return carry

    lax.fori_loop(0, PEER_HEADS, head_body, 0)


def _peer_select(h2d, g, w_q, sub_keys, tb=1024):
    t, d = h2d.shape
    consts = [g.reshape(1, d), w_q.astype(BF16), sub_keys.astype(BF16)]
    slot_spec = pl.BlockSpec((PEER_SLOTS, tb // LANES, LANES), lambda i: (0, i, 0))
    xn, experts, gates = pl.pallas_call(
        _peer_select_kernel,
        grid=(t // tb,),
        in_specs=_row_specs([h2d], tb) + _const_specs(consts),
        out_specs=[pl.BlockSpec((tb, d), lambda i: (i, 0)), slot_spec, slot_spec],
        out_shape=[
            jax.ShapeDtypeStruct((t, d), F32),
            jax.ShapeDtypeStruct((PEER_SLOTS, t // LANES, LANES), jnp.int32),
            jax.ShapeDtypeStruct((PEER_SLOTS, t // LANES, LANES), F32),
        ],
        scratch_shapes=[pltpu.VMEM((tb, PEER_HEADS * PEER_DQ), BF16)],
        compiler_params=pltpu.CompilerParams(dimension_semantics=("parallel",),
                                             vmem_limit_bytes=VMEM_LIMIT),
        name="peer_select",
    )(h2d, *consts)
    return xn, experts.reshape(PEER_SLOTS, t), gates.reshape(PEER_SLOTS, t)


ROW_TILE = D_MODEL // LANES
PAIR_ROWS = 2 * ROW_TILE


def _pack_expert_table(tab):
    n, d = tab.shape
    t = tab.astype(BF16).reshape(2, n // 2, ROW_TILE, LANES)
    bits = lax.bitcast_convert_type(t, jnp.uint16).astype(jnp.uint32)
    return (bits[0] | (bits[1] << 16)).reshape(n // 2 * ROW_TILE, LANES)


TOKENS_PER_ITER = 8


def _gather_tiles(idx_ref, tab_ref, g_ref, t):
    for k in range(PEER_SLOTS):
        start = pl.multiple_of(idx_ref[t, k], ROW_TILE)
        g_ref[k * ROW_TILE:(k + 1) * ROW_TILE, :] = tab_ref[pl.ds(start, ROW_TILE), :]


def _peer_u_kernel(idx_ref, x_ref, hi_ref, gate_ref, fold_ref, tab_ref, w_ref, sel_ref, *g_refs):
    tb = x_ref.shape[0]
    shape = (ROW_TILE, PEER_SLOTS * PAIR_ROWS)
    row_in_tile = lax.broadcasted_iota(jnp.int32, shape, 1) & (PAIR_ROWS - 1)
    chunk = lax.broadcasted_iota(jnp.int32, shape, 0)
    keep = (row_in_tile >> 1) == chunk

    def token_pair(i, carry):
        for j, g_ref in enumerate(g_refs):
            t = i * TOKENS_PER_ITER + j
            _gather_tiles(idx_ref, tab_ref, g_ref, t)
            x8 = x_ref[pl.ds(t, 1), :].reshape(ROW_TILE, LANES)
            x16 = jnp.concatenate([x8, jnp.zeros_like(x8)], axis=0).astype(BF16)
            out = lax.dot_general(x16, pltpu.bitcast(g_ref[...], BF16), (((1,), (1,)), ((), ())),
                                  preferred_element_type=F32)
            sel_ref[pl.ds(t, 1), :] = jnp.sum(jnp.where(keep, out[:ROW_TILE], 0.0), axis=0, keepdims=True)
        return carry

    lax.fori_loop(0, tb // TOKENS_PER_ITER, token_pair, 0)
    sel = sel_ref[...]
    sel_hi = sel.astype(BF16)
    sel_lo = (sel - sel_hi.astype(F32)).astype(BF16)
    r = (jnp.dot(sel_hi, fold_ref[...], preferred_element_type=F32)
         + jnp.dot(sel_lo, fold_ref[...], preferred_element_type=F32))
    a = jnp.where(hi_ref[...] == 1, r[:, PEER_SLOTS:], r[:, :PEER_SLOTS])
    w_ref[...] = gate_ref[...] * jax.nn.gelu(a)


def _peer_u(idx, xn, hi, gate, tab, tb):
    t = idx.shape[0]
    eye = jnp.eye(PEER_SLOTS, dtype=BF16)
    parity = (jnp.arange(PAIR_ROWS) & 1).astype(BF16)
    fold = jnp.concatenate([jnp.kron(eye, (1 - parity)[:, None]), jnp.kron(eye, parity[:, None])], axis=1)
    g_shape = pltpu.VMEM((PEER_SLOTS * ROW_TILE, LANES), jnp.uint32)
    return pl.pallas_call(
        _peer_u_kernel,
        grid=(t // tb,),
        in_specs=[
            pl.BlockSpec((tb, PEER_SLOTS), lambda i: (i, 0), memory_space=pltpu.SMEM),
            pl.BlockSpec((tb, D_MODEL), lambda i: (i, 0)),
            pl.BlockSpec((tb, PEER_SLOTS), lambda i: (i, 0)),
            pl.BlockSpec((tb, PEER_SLOTS), lambda i: (i, 0)),
            pl.BlockSpec(fold.shape, lambda i: (0, 0), pipeline_mode=pl.Buffered(1)),
            pl.BlockSpec(tab.shape, lambda i: (0, 0), pipeline_mode=pl.Buffered(1)),
        ],
        out_specs=pl.BlockSpec((tb, PEER_SLOTS), lambda i: (i, 0)),
        out_shape=jax.ShapeDtypeStruct((t, PEER_SLOTS), F32),
        scratch_shapes=[pltpu.VMEM((tb, PEER_SLOTS * PAIR_ROWS), F32)] + [g_shape] * TOKENS_PER_ITER,
        compiler_params=pltpu.CompilerParams(dimension_semantics=("arbitrary",),
                                             vmem_limit_bytes=PEER_VMEM_LIMIT),
        name="peer_u",
    )(idx, xn, hi, gate, fold, tab)


def _peer_v_kernel(idx_ref, hi_ref, w_ref, h_ref, route_ref, tab_ref, o_ref, wx_ref, *g_refs):
    tb = hi_ref.shape[0]
    w = w_ref[...]
    w_hi = jnp.where(hi_ref[...] == 1, w, 0.0)
    w2 = jnp.concatenate([w - w_hi, w_hi], axis=1).astype(BF16)
    wx_ref[...] = jnp.dot(w2, route_ref[...], preferred_element_type=F32)
    shape = (PAIR_ROWS, PEER_SLOTS * PAIR_ROWS)
    row_in_tile = lax.broadcasted_iota(jnp.int32, shape, 1) & (PAIR_ROWS - 1)
    chunk = lax.broadcasted_iota(jnp.int32, shape, 0)
    keep = (row_in_tile >> 1) == chunk

    def token_pair(i, carry):
        for j, g_ref in enumerate(g_refs):
            t = i * TOKENS_PER_ITER + j
            _gather_tiles(idx_ref, tab_ref, g_ref, t)
            lhs = jnp.where(keep, wx_ref[pl.ds(t, 1), :], 0.0).astype(BF16)
            o = jnp.dot(lhs, pltpu.bitcast(g_ref[...], BF16), preferred_element_type=F32)
            o_ref[pl.ds(t, 1), :] = h_ref[pl.ds(t, 1), :] + o[:ROW_TILE].reshape(1, D_MODEL)
        return carry

    lax.fori_loop(0, tb // TOKENS_PER_ITER, token_pair, 0)


def _peer_v(idx, hi, w, h2d, tab, tb):
    t = idx.shape[0]
    eye = jnp.eye(PEER_SLOTS, dtype=BF16)
    parity = (jnp.arange(PAIR_ROWS) & 1).astype(BF16)
    route = jnp.concatenate([jnp.kron(eye, (1 - parity)[None, :]), jnp.kron(eye, parity[None, :])], axis=0)
    return pl.pallas_call(
        _peer_v_kernel,
        grid=(t // tb,),
        in_specs=[
            pl.BlockSpec((tb, PEER_SLOTS), lambda i: (i, 0), memory_space=pltpu.SMEM),
            pl.BlockSpec((tb, PEER_SLOTS), lambda i: (i, 0)),
            pl.BlockSpec((tb, PEER_SLOTS), lambda i: (i, 0)),
            pl.BlockSpec((tb, D_MODEL), lambda i: (i, 0)),
            pl.BlockSpec(route.shape, lambda i: (0, 0), pipeline_mode=pl.Buffered(1)),
            pl.BlockSpec(tab.shape, lambda i: (0, 0), pipeline_mode=pl.Buffered(1)),
        ],
        out_specs=pl.BlockSpec((tb, D_MODEL), lambda i: (i, 0)),
        out_shape=jax.ShapeDtypeStruct((t, D_MODEL), F32),
        scratch_shapes=[pltpu.VMEM((tb, PEER_SLOTS * PAIR_ROWS), F32)]
        + [pltpu.VMEM((PEER_SLOTS * ROW_TILE, LANES), jnp.uint32)] * TOKENS_PER_ITER,
        compiler_params=pltpu.CompilerParams(dimension_semantics=("arbitrary",),
                                             vmem_limit_bytes=PEER_VMEM_LIMIT),
        name="peer_v",
    )(idx, hi, w, h2d, route, tab)


def _peer_experts(h2d, xn2d, experts, gate, u_tab, v_tab, tb_u=128, tb_v=128):
    t, d = xn2d.shape
    half = u_tab.shape[0] // 2
    hi = (experts >= half).astype(jnp.int32)
    idx = (experts - hi * half) * ROW_TILE
    w = _peer_u(idx, xn2d, hi, gate, _pack_expert_table(u_tab), tb_u)
    return _peer_v(idx, hi, w, h2d, _pack_expert_table(v_tab), tb_v)


def kernel(x, mem, positions, attn_norm_g, w_in, mla_q_norm_g, w_uq, mla_kv_norm_g, w_ukv,
           mla_qn_g, mla_kn_g, w_o_mla, ret_gn_g, w_o_ret, mem_norm_g, w_mem_kv, x_qn_g, x_kn_g,
           w_o_cross, w_out, ffn_norm_g, peer_w_q, peer_keys, peer_u, peer_v):
    B, S, D = x.shape
    T = B * S
    h2d = x.reshape(T, D)
    offs = [0]
    for sz in IN_SPLITS:
        offs.append(offs[-1] + sz)
    for layer in range(w_in.shape[0]):
        w_pieces = [w_in[layer][:, offs[i]:offs[i + 1]].astype(BF16) for i in range(len(IN_SPLITS))]
        out_dtypes = [F32, F32, F32, F32, F32, BF16, BF16, BF16, BF16]
        c_q, c_kv, k_r, r_q, r_k, r_v, r_g, x_q, gates = _in_proj(h2d, attn_norm_g[layer], w_pieces, out_dtypes)
        o_mla = _mla_branch(c_q, c_kv, k_r, positions, mla_q_norm_g[layer], w_uq[layer],
                            mla_kv_norm_g[layer], w_ukv[layer], mla_qn_g[layer], mla_kn_g[layer], B, S)
        seq = lambda a: a.reshape(B, S, -1)
        o_ret = _retention_branch(seq(r_q), seq(r_k), seq(r_v), seq(r_g), positions, ret_gn_g[layer])
        mk, mv = _mem_kv(mem, mem_norm_g[layer], w_mem_kv[layer], x_kn_g[layer])
        o_mem = _mem_attn(seq(x_q), mk, mv, x_qn_g[layer])
        h2d = _merge(h2d, gates, o_mla, o_ret.reshape(T, -1), o_mem.reshape(T, -1),
                     w_o_mla[layer], w_o_ret[layer], w_o_cross[layer], w_out[layer])
        xn2d, experts_t, gate_t = _peer_select(h2d, ffn_norm_g[layer], peer_w_q[layer], peer_keys[layer])
        h2d = _peer_experts(h2d, xn2d, experts_t.T, gate_t.T, peer_u[layer], peer_v[layer])
    return h2d.reshape(B, S, D)
```

```python
import functools

import jax
import jax.numpy as jnp
from jax import lax
from jax.experimental import pallas as pl
from jax.experimental.pallas import tpu as pltpu

D_MODEL = 1024
MEM_LEN = 256
MLA_HEADS = 8
MLA_Q_RANK = 384
MLA_KV_RANK = 128
MLA_NOPE = 64
MLA_ROPE = 32
MLA_V = 64
RET_HEADS = 4
RET_DK = 64
RET_DV = 128
RET_CHUNK = 128
X_HEADS = 4
X_DH = 128
N_BRANCH = 3
PEER_HEADS = 8
PEER_N_KEYS = 128
PEER_TOPK = 16
PEER_DQ = 256
PEER_DHALF = PEER_DQ // 2
ROPE_BASE = 10000.0
EPS = 1e-6
IN_SPLITS = (MLA_Q_RANK, MLA_KV_RANK, MLA_ROPE,
             RET_HEADS * RET_DK, RET_HEADS * RET_DK, RET_HEADS * RET_DV, RET_HEADS * RET_DV,
             X_HEADS * X_DH, N_BRANCH * D_MODEL)

LANES = 128
VMEM_LIMIT = 48 * 1024 * 1024
PEER_VMEM_LIMIT = 56 * 1024 * 1024
BF16 = jnp.bfloat16
F32 = jnp.float32


def _norm(x, g):
    return x * lax.rsqrt(jnp.mean(x * x, axis=-1, keepdims=True) + EPS) * g


def _row_specs(arrays, tm):
    return [pl.BlockSpec((tm, a.shape[1]), lambda i: (i, 0)) for a in arrays]


def _const_specs(arrays):
    return [pl.BlockSpec(a.shape, lambda i, n=a.ndim: (0,) * n) for a in arrays]


def _in_proj_kernel(x_ref, g_ref, *refs):
    n_w = len(refs) // 2
    w_refs, o_refs = refs[:n_w], refs[n_w:]
    nb = _norm(x_ref[...], g_ref[...]).astype(BF16)
    for w_ref, o_ref in zip(w_refs, o_refs):
        o_ref[...] = jnp.dot(nb, w_ref[...], preferred_element_type=F32).astype(o_ref.dtype)


def _in_proj(x2d, g, w_pieces, out_dtypes, tm=512):
    t, d = x2d.shape
    consts = [g.reshape(1, d)] + list(w_pieces)
    return pl.pallas_call(
        _in_proj_kernel,
        grid=(t // tm,),
        in_specs=_row_specs([x2d], tm) + _const_specs(consts),
        out_specs=[pl.BlockSpec((tm, w.shape[1]), lambda i: (i, 0)) for w in w_pieces],
        out_shape=[jax.ShapeDtypeStruct((t, w.shape[1]), dt) for w, dt in zip(w_pieces, out_dtypes)],
        compiler_params=pltpu.CompilerParams(dimension_semantics=("parallel",),
                                             vmem_limit_bytes=VMEM_LIMIT),
        name="in_proj",
    )(x2d, *consts)


def _rope_patterns(positions, n_rope, lead, width):
    inv_freq = ROPE_BASE ** (-jnp.arange(0, n_rope, 2, dtype=F32) / n_rope)
    ang = positions.astype(F32).reshape(-1, 1) * inv_freq
    cos, sin = jnp.cos(ang), jnp.sin(ang)
    t = ang.shape[0]
    tail = width - lead - n_rope
    cosp = jnp.concatenate([jnp.ones((t, lead)), cos, cos, jnp.zeros((t, tail))], axis=1)
    sinp = jnp.concatenate([jnp.zeros((t, lead)), -sin, sin, jnp.zeros((t, tail))], axis=1)
    rep = LANES // width
    return jnp.tile(cosp, (1, rep)), jnp.tile(sinp, (1, rep))


def _rotate_half(x, cosp, sinp, n_rope, lead, width):
    half = n_rope // 2
    lane = lax.broadcasted_iota(jnp.int32, x.shape, 1) & (width - 1)
    partner = jnp.where(lane < lead + half, pltpu.roll(x, LANES - half, 1), pltpu.roll(x, half, 1))
    return x * cosp + partner * sinp


MLA_QK = MLA_NOPE + MLA_ROPE


def _pad_head_cols(w, per_head):
    r = w.shape[0]
    w = w.reshape(r, MLA_HEADS, per_head)
    return jnp.pad(w, ((0, 0), (0, 0), (0, LANES - per_head))).reshape(r, MLA_HEADS * LANES)


def _mla_prep_kernel(cq_ref, ckv_ref, kr_ref, cos_ref, sin_ref, gq_ref, gkv_ref, wuq_ref, wuk_ref,
                     wuv_ref, qn_ref, kn_ref, q_ref, k_ref, v_ref):
    cqn = _norm(cq_ref[...], gq_ref[...]).astype(BF16)
    ckvn = _norm(ckv_ref[...], gkv_ref[...]).astype(BF16)
    qp = jnp.dot(cqn, wuq_ref[...], preferred_element_type=F32)
    kp = jnp.dot(ckvn, wuk_ref[...], preferred_element_type=F32)
    v_ref[...] = jnp.dot(ckvn, wuv_ref[...], preferred_element_type=F32).astype(v_ref.dtype)
    tm = qp.shape[0]
    kr = jnp.concatenate([jnp.zeros((tm, MLA_NOPE), F32), kr_ref[...],
                          jnp.zeros((tm, LANES - MLA_QK), F32)], axis=1)
    cosp, sinp = cos_ref[...], sin_ref[...]

    def qk_norm_rope(xh, g, scale):
        ss = jnp.sum(xh * xh, axis=-1, keepdims=True) * (1.0 / MLA_QK)
        xn = xh * lax.rsqrt(ss + EPS) * g
        return _rotate_half(xn, cosp, sinp, MLA_ROPE, MLA_NOPE, LANES) * scale

    for hd in range(MLA_HEADS):
        sl = slice(hd * LANES, (hd + 1) * LANES)
        q_ref[:, sl] = qk_norm_rope(qp[:, sl], qn_ref[...], MLA_QK ** -0.5).astype(q_ref.dtype)
        k_ref[:, sl] = qk_norm_rope(kp[:, sl] + kr, kn_ref[...], 1.0).astype(k_ref.dtype)


def _mla_prep(c_q, c_kv, k_r, cosp, sinp, q_norm_g, w_uq, kv_norm_g, w_ukv, qn_g, kn_g, tm=512):
    t = c_q.shape[0]
    w_ukv3 = w_ukv.reshape(MLA_KV_RANK, MLA_HEADS, MLA_NOPE + MLA_V)
    w_uk = _pad_head_cols(w_ukv3[:, :, :MLA_NOPE].reshape(MLA_KV_RANK, -1), MLA_NOPE).astype(BF16)
    w_uv = w_ukv3[:, :, MLA_NOPE:].reshape(MLA_KV_RANK, MLA_HEADS * MLA_V).astype(BF16)
    w_uqp = _pad_head_cols(w_uq, MLA_QK).astype(BF16)
    pad_g = lambda g: jnp.pad(g, (0, LANES - MLA_QK)).reshape(1, LANES)
    consts = [q_norm_g.reshape(1, -1), kv_norm_g.reshape(1, -1), w_uqp, w_uk, w_uv, pad_g(qn_g), pad_g(kn_g)]
    rows = [c_q, c_kv, k_r, cosp, sinp]
    wide = MLA_HEADS * LANES
    return pl.pallas_call(
        _mla_prep_kernel,
        grid=(t // tm,),
        in_specs=_row_specs(rows, tm) + _const_specs(consts),
        out_specs=[pl.BlockSpec((tm, wide), lambda i: (i, 0)),
                   pl.BlockSpec((tm, wide), lambda i: (i, 0)),
                   pl.BlockSpec((tm, MLA_HEADS * MLA_V), lambda i: (i, 0))],
        out_shape=[jax.ShapeDtypeStruct((t, wide), BF16), jax.ShapeDtypeStruct((t, wide), BF16),
                   jax.ShapeDtypeStruct((t, MLA_HEADS * MLA_V), BF16)],
        compiler_params=pltpu.CompilerParams(dimension_semantics=("parallel",),
                                             vmem_limit_bytes=VMEM_LIMIT),
        name="mla_prep",
    )(*rows, *consts)


def _mla_attn_kernel(qt_ref, k_ref, vt_ref, o_ref, st_a, st_b, m_ref, l_ref, acc_ref, *, tile):
    qi = pl.program_id(2)
    key = lax.broadcasted_iota(jnp.int32, (tile, tile), 0)
    qry = lax.broadcasted_iota(jnp.int32, (tile, tile), 1)
    head_lanes = [slice(a * LANES, (a + 1) * LANES) for a in range(2)]
    qts = [qt_ref[0, lanes, :] for lanes in head_lanes]

    def scores_into(buf, j):
        start = pl.multiple_of(j * tile, tile)
        for a, lanes in enumerate(head_lanes):
            buf[a] = jnp.dot(k_ref[0, pl.ds(start, tile), lanes], qts[a], preferred_element_type=F32)

    def consume(buf, j, masked):
        start = pl.multiple_of(j * tile, tile)
        vtj = vt_ref[0, :, pl.ds(start, tile)]
        for a in range(2):
            st = buf[a]
            if masked:
                st = jnp.where(key <= qry, st, -1e30)
            m = m_ref[a]
            m_new = jnp.maximum(m, jnp.max(st, axis=0, keepdims=True))
            alpha = jnp.exp(m - m_new)
            p = jnp.exp(st - m_new)
            m_ref[a] = m_new
            l_ref[a] = alpha * l_ref[a] + jnp.sum(p, axis=0, keepdims=True)
            pv = jnp.dot(vtj, p.astype(BF16), preferred_element_type=F32)
            acc_ref[a] = alpha * acc_ref[a] + pv

    m_ref[...] = jnp.full(m_ref.shape, -1e30, F32)
    l_ref[...] = jnp.zeros(l_ref.shape, F32)
    acc_ref[...] = jnp.zeros(acc_ref.shape, F32)
    scores_into(st_a, 0)

    def two_tiles(i, carry):
        j = 2 * i
        scores_into(st_b, j + 1)
        consume(st_a, j, False)
        scores_into(st_a, j + 2)
        consume(st_b, j + 1, False)
        return carry

    lax.fori_loop(0, qi // 2, two_tiles, 0)

    @pl.when(qi % 2 == 0)
    def _():
        consume(st_a, qi, True)

    @pl.when(qi % 2 == 1)
    def _():
        scores_into(st_b, qi)
        consume(st_a, qi - 1, False)
        consume(st_b, qi, True)

    dv_row = lax.broadcasted_iota(jnp.int32, acc_ref.shape[1:], 0)
    out_t = jnp.where(dv_row < MLA_V, acc_ref[0] / l_ref[0], acc_ref[1] / l_ref[1])
    o_ref[0] = out_t.T.astype(o_ref.dtype)


def _mla_attn(q, k, v, tile=512):
    b, s, _ = q.shape
    pair = 2 * LANES
    qt, vt = q.transpose(0, 2, 1), v.transpose(0, 2, 1)
    return pl.pallas_call(
        functools.partial(_mla_attn_kernel, tile=tile),
        grid=(b, MLA_HEADS // 2, s // tile),
        in_specs=[pl.BlockSpec((1, pair, tile), lambda bi, hp, qi: (bi, hp, qi)),
                  pl.BlockSpec((1, s, pair), lambda bi, hp, qi: (bi, 0, hp)),
                  pl.BlockSpec((1, 2 * MLA_V, s), lambda bi, hp, qi: (bi, hp, 0))],
        out_specs=pl.BlockSpec((1, tile, 2 * MLA_V), lambda bi, hp, qi: (bi, qi, hp)),
        out_shape=jax.ShapeDtypeStruct((b, s, MLA_HEADS * MLA_V), BF16),
        scratch_shapes=[pltpu.VMEM((2, tile, tile), F32), pltpu.VMEM((2, tile, tile), F32),
                        pltpu.VMEM((2, 1, tile), F32), pltpu.VMEM((2, 1, tile), F32),
                        pltpu.VMEM((2, 2 * MLA_V, tile), F32)],
        compiler_params=pltpu.CompilerParams(dimension_semantics=("parallel", "parallel", "arbitrary"),
                                             vmem_limit_bytes=VMEM_LIMIT),
        name="mla_attn",
    )(qt, k, vt)


def _mla_branch(c_q, c_kv, k_r, positions, q_norm_g, w_uq, kv_norm_g, w_ukv, qn_g, kn_g, b, s):
    cosp, sinp = _rope_patterns(positions, MLA_ROPE, MLA_NOPE, LANES)
    q, k, v = _mla_prep(c_q, c_kv, k_r, cosp, sinp, q_norm_g, w_uq, kv_norm_g, w_ukv, qn_g, kn_g)
    o = _mla_attn(q.reshape(b, s, -1), k.reshape(b, s, -1), v.reshape(b, s, -1))
    return o.reshape(b * s, -1)


def _retention_kernel(q_ref, k_ref, v_ref, g_ref, cos_ref, sin_ref, decay_ref, xi_ref, zeta_ref,
                      cd_ref, gn_ref, o_ref):
    L = RET_CHUNK
    n_chunks = q_ref.shape[1] // L
    lane = lax.broadcasted_iota(jnp.int32, (L, LANES), 1)

    def chunk_body(c, states):
        rows = pl.ds(pl.multiple_of(c * L, L), L)
        cosp, sinp = cos_ref[0, rows, :], sin_ref[0, rows, :]
        qc = _rotate_half(q_ref[0, rows, :], cosp, sinp, RET_DK, 0, RET_DK)
        kc = _rotate_half(k_ref[0, rows, :], cosp, sinp, RET_DK, 0, RET_DK) * (RET_DK ** -0.5)
        qx = qc * xi_ref[0]
        kz = kc * zeta_ref[0]
        new_states = []
        for a in range(2):
            mine = (lane >= a * RET_DK) & (lane < (a + 1) * RET_DK)
            qa = jnp.where(mine, qc, 0.0).astype(BF16)
            ka = jnp.where(mine, kc, 0.0).astype(BF16)
            va = v_ref[0, rows, a * RET_DV:(a + 1) * RET_DV]
            inner = lax.dot_general(qa, ka, (((1,), (1,)), ((), ())),
                                    preferred_element_type=F32) * decay_ref[a]
            cross = jnp.dot(jnp.where(mine, qx, 0.0).astype(BF16), states[a].astype(BF16),
                            preferred_element_type=F32)
            o = jnp.dot(inner.astype(BF16), va, preferred_element_type=F32) + cross
            kzt = jnp.where(mine, kz, 0.0).T.astype(BF16)
            incr = jnp.dot(kzt, va, preferred_element_type=F32)
            new_states.append(cd_ref[a, 0:1, :] * states[a] + incr)
            o = _norm(o, gn_ref[a])
            gate = g_ref[0, rows, a * RET_DV:(a + 1) * RET_DV].astype(F32)
            o_ref[0, rows, a * RET_DV:(a + 1) * RET_DV] = (o * gate * jax.nn.sigmoid(gate)).astype(o_ref.dtype)
        return tuple(new_states)

    zero = jnp.zeros((LANES, RET_DV), F32)
    lax.fori_loop(0, n_chunks, chunk_body, (zero, zero))


def _retention_branch(r_q, r_k, r_v, r_g, positions, gn_g):
    b, s, _ = r_q.shape
    L = RET_CHUNK
    cosp, sinp = _rope_patterns(positions, RET_DK, 0, RET_DK)
    cosp, sinp = cosp.reshape(b, s, LANES), sinp.reshape(b, s, LANES)
    log_gamma = jnp.log(1.0 - 2.0 ** (-5.0 - jnp.arange(RET_HEADS, dtype=F32)))
    idx = jnp.arange(L, dtype=F32)
    diff = idx[:, None] - idx[None, :]
    decay = jnp.where(diff >= 0, jnp.exp(log_gamma[:, None, None] * jnp.maximum(diff, 0.0)), 0.0)
    xi = jnp.exp(log_gamma[:, None] * (idx + 1.0))
    zeta = jnp.exp(log_gamma[:, None] * (L - 1.0 - idx))
    per_pair = lambda a: jnp.repeat(a.reshape(RET_HEADS // 2, 2, L), RET_DK, axis=1).transpose(0, 2, 1)
    cd = jnp.broadcast_to(jnp.exp(log_gamma * L)[:, None, None], (RET_HEADS, 8, RET_DV))
    gn = gn_g.reshape(RET_HEADS, 1, RET_DV)
    n_pairs = RET_HEADS // 2
    seq = lambda width: pl.BlockSpec((1, s, width), lambda bi, hp: (bi, 0, hp))
    return pl.pallas_call(
        _retention_kernel,
        grid=(b, n_pairs),
        in_specs=[seq(LANES), seq(LANES), seq(2 * RET_DV), seq(2 * RET_DV),
                  pl.BlockSpec((1, s, LANES), lambda bi, hp: (bi, 0, 0)),
                  pl.BlockSpec((1, s, LANES), lambda bi, hp: (bi, 0, 0)),
                  pl.BlockSpec((2, L, L), lambda bi, hp: (hp, 0, 0)),
                  pl.BlockSpec((1, L, LANES), lambda bi, hp: (hp, 0, 0)),
                  pl.BlockSpec((1, L, LANES), lambda bi, hp: (hp, 0, 0)),
                  pl.BlockSpec((2, 8, RET_DV), lambda bi, hp: (hp, 0, 0)),
                  pl.BlockSpec((2, 1, RET_DV), lambda bi, hp: (hp, 0, 0))],
        out_specs=seq(2 * RET_DV),
        out_shape=jax.ShapeDtypeStruct((b, s, RET_HEADS * RET_DV), BF16),
        compiler_params=pltpu.CompilerParams(dimension_semantics=("parallel", "parallel"),
                                             vmem_limit_bytes=VMEM_LIMIT),
        name="retention",
    )(r_q, r_k, r_v, r_g, cosp, sinp, decay, per_pair(xi), per_pair(zeta), cd, gn)


def _mem_kv_kernel(mem_ref, g_ref, w_ref, kn_ref, k_ref, v_ref):
    kv = jnp.dot(_norm(mem_ref[0], g_ref[...]).astype(BF16), w_ref[...], preferred_element_type=F32)
    width = X_HEADS * X_DH
    for hd in range(X_HEADS):
        sl = slice(hd * X_DH, (hd + 1) * X_DH)
        k_ref[0, :, sl] = _norm(kv[:, sl], kn_ref[...]).astype(k_ref.dtype)
    v_ref[0] = kv[:, width:].astype(v_ref.dtype)


def _mem_kv(mem, mem_g, w_mem_kv, kn_g):
    b, m, d = mem.shape
    width = X_HEADS * X_DH
    w4 = w_mem_kv.reshape(d, X_HEADS, 2, X_DH)
    w = jnp.concatenate([w4[:, :, 0].reshape(d, width), w4[:, :, 1].reshape(d, width)], axis=1).astype(BF16)
    consts = [mem_g.reshape(1, d), w, kn_g.reshape(1, X_DH)]
    blk = pl.BlockSpec((1, m, width), lambda i: (i, 0, 0))
    return pl.pallas_call(
        _mem_kv_kernel,
        grid=(b,),
        in_specs=[pl.BlockSpec((1, m, d), lambda i: (i, 0, 0))] + _const_specs(consts),
        out_specs=[blk, blk],
        out_shape=[jax.ShapeDtypeStruct((b, m, width), BF16)] * 2,
        compiler_params=pltpu.CompilerParams(dimension_semantics=("parallel",),
                                             vmem_limit_bytes=VMEM_LIMIT),
        name="mem_kv",
    )(mem, *consts)


def _mem_attn_kernel(q_ref, k_ref, v_ref, qn_ref, o_ref):
    for hd in range(X_HEADS):
        sl = slice(hd * X_DH, (hd + 1) * X_DH)
        q = (_norm(q_ref[0, :, sl].astype(F32), qn_ref[...]) * (X_DH ** -0.5)).astype(BF16)
        s = lax.dot_general(q, k_ref[0, :, sl], (((1,), (1,)), ((), ())), preferred_element_type=F32)
        p = jnp.exp(s - jnp.max(s, axis=-1, keepdims=True))
        o = jnp.dot(p.astype(BF16), v_ref[0, :, sl], preferred_element_type=F32)
        o_ref[0, :, sl] = (o / jnp.sum(p, axis=-1, keepdims=True)).astype(o_ref.dtype)


def _mem_attn(x_q, k, v, qn_g, tm=512):
    b, s, width = x_q.shape
    m = k.shape[1]
    kv_spec = pl.BlockSpec((1, m, width), lambda bi, i: (bi, 0, 0))
    return pl.pallas_call(
        _mem_attn_kernel,
        grid=(b, s // tm),
        in_specs=[pl.BlockSpec((1, tm, width), lambda bi, i: (bi, i, 0)), kv_spec, kv_spec,
                  pl.BlockSpec((1, X_DH), lambda bi, i: (0, 0))],
        out_specs=pl.BlockSpec((1, tm, width), lambda bi, i: (bi, i, 0)),
        out_shape=jax.ShapeDtypeStruct((b, s, width), BF16),
        compiler_params=pltpu.CompilerParams(dimension_semantics=("parallel", "parallel"),
                                             vmem_limit_bytes=VMEM_LIMIT),
        name="mem_attn",
    )(x_q, k, v, qn_g.reshape(1, X_DH))


def _merge_kernel(x_ref, gates_ref, oa_ref, ob_ref, oc_ref, wa_ref, wb_ref, wc_ref, wout_ref, h_ref):
    merged = None
    for i, (o_ref, w_ref) in enumerate(((oa_ref, wa_ref), (ob_ref, wb_ref), (oc_ref, wc_ref))):
        y = jnp.dot(o_ref[...], w_ref[...], preferred_element_type=F32)
        gate = jax.nn.sigmoid(gates_ref[:, i * D_MODEL:(i + 1) * D_MODEL].astype(F32))
        merged = gate * y if merged is None else merged + gate * y
    h_ref[...] = x_ref[...] + jnp.dot(merged.astype(BF16), wout_ref[...], preferred_element_type=F32)


def _merge(x2d, gates, o_mla, o_ret, o_mem, w_o_mla, w_o_ret, w_o_cross, w_out, tm=512):
    t, d = x2d.shape
    rows = [x2d, gates, o_mla, o_ret, o_mem]
    consts = [w.astype(BF16) for w in (w_o_mla, w_o_ret, w_o_cross, w_out)]
    return pl.pallas_call(
        _merge_kernel,
        grid=(t // tm,),
        in_specs=_row_specs(rows, tm) + _const_specs(consts),
        out_specs=pl.BlockSpec((tm, d), lambda i: (i, 0)),
        out_shape=jax.ShapeDtypeStruct((t, d), F32),
        compiler_params=pltpu.CompilerParams(dimension_semantics=("parallel",),
                                             vmem_limit_bytes=VMEM_LIMIT),
        name="merge",
    )(*rows, *consts)


PEER_SLOTS = PEER_HEADS * PEER_TOPK
PAIR_CANDS = [(i, j) for i in range(PEER_TOPK) for j in range(PEER_TOPK)
              if (i + 1) * (j + 1) <= PEER_TOPK]
N_CANDS = len(PAIR_CANDS)
NEG_INF = float("-inf")


def _top_rounds(s, n_rounds):
    rows = lax.broadcasted_iota(jnp.int32, s.shape, 0)
    vals, idxs = [], []
    for _ in range(n_rounds):
        m = jnp.max(s, axis=0, keepdims=True)
        idx = jnp.min(jnp.where(s == m, rows, s.shape[0]), axis=0, keepdims=True)
        s = jnp.where(rows == idx, NEG_INF, s)
        vals.append(m)
        idxs.append(idx)
    return vals, idxs


def _peer_select_kernel(h_ref, g_ref, wq_ref, keys_ref, xn_ref, exp_ref, gate_ref, q_ref):
    xn = _norm(h_ref[...], g_ref[...])
    xn_ref[...] = xn
    xnb = xn.astype(BF16)
    q_ref[...] = jnp.dot(xnb, wq_ref[...], preferred_element_type=F32).astype(BF16)
    tb = xnb.shape[0]
    tok = (tb // LANES, LANES)

    def head_body(hd, carry):
        sv, si = [], []
        for c in range(2):
            col = pl.multiple_of(hd * PEER_DQ + c * PEER_DHALF, PEER_DHALF)
            qh = q_ref[:, pl.ds(col, PEER_DHALF)]
            st = lax.dot_general(keys_ref[hd, c], qh, (((1,), (1,)), ((), ())),
                                 preferred_element_type=F32)
            v, i = _top_rounds(st.reshape((PEER_N_KEYS,) + tok), PEER_TOPK)
            sv.append(v)
            si.append(i)
        cand = jnp.concatenate([sv[0][i] + sv[1][j] for i, j in PAIR_CANDS], axis=0)
        cand_e = jnp.concatenate([si[0][i] * PEER_N_KEYS + si[1][j] for i, j in PAIR_CANDS], axis=0)
        rows = lax.broadcasted_iota(jnp.int32, cand.shape, 0)
        best, experts = [], []
        for _ in range(PEER_TOPK):
            m = jnp.max(cand, axis=0, keepdims=True)
            pos = jnp.min(jnp.where(cand == m, rows, N_CANDS), axis=0, keepdims=True)
            hit = rows == pos
            experts.append(jnp.sum(jnp.where(hit, cand_e, 0), axis=0, keepdims=True))
            cand = jnp.where(hit, NEG_INF, cand)
            best.append(m)
        p = [jnp.exp(b - best[0]) for b in best]
        denom = p[0]
        for pk in p[1:]:
            denom = denom + pk
        row0 = pl.multiple_of(hd * PEER_TOPK, PEER_TOPK)
        exp_ref[pl.ds(row0, PEER_TOPK)] = jnp.concatenate(experts, axis=0)
        gate_ref[pl.ds(row0, PEER_TOPK)] = jnp.concatenate([pk / denom for pk in p], axis=0)
        return carry

    lax.fori_loop(0, PEER_HEADS, head_body, 0)


def _peer_select(h2d, g, w_q, sub_keys, tb=1024):
    t, d = h2d.shape
    consts = [g.reshape(1, d), w_q.astype(BF16), sub_keys.astype(BF16)]
    slot_spec = pl.BlockSpec((PEER_SLOTS, tb // LANES, LANES), lambda i: (0, i, 0))
    xn, experts, gates = pl.pallas_call(
        _peer_select_kernel,
        grid=(t // tb,),
        in_specs=_row_specs([h2d], tb) + _const_specs(consts),
        out_specs=[pl.BlockSpec((tb, d), lambda i: (i, 0)), slot_spec, slot_spec],
        out_shape=[
            jax.ShapeDtypeStruct((t, d), F32),
            jax.ShapeDtypeStruct((PEER_SLOTS, t // LANES, LANES), jnp.int32),
            jax.ShapeDtypeStruct((PEER_SLOTS, t // LANES, LANES), F32),
        ],
        scratch_shapes=[pltpu.VMEM((tb, PEER_HEADS * PEER_DQ), BF16)],
        compiler_params=pltpu.CompilerParams(dimension_semantics=("parallel",),
                                             vmem_limit_bytes=VMEM_LIMIT),
        name="peer_select",
    )(h2d, *consts)
    return xn, experts.reshape(PEER_SLOTS, t), gates.reshape(PEER_SLOTS, t)


ROW_TILE = D_MODEL // LANES
PAIR_ROWS = 2 * ROW_TILE


def _pack_expert_table(tab):
    n, d = tab.shape
    t = tab.astype(BF16).reshape(2, n // 2, ROW_TILE, LANES)
    bits = lax.bitcast_convert_type(t, jnp.uint16).astype(jnp.uint32)
    return (bits[0] | (bits[1] << 16)).reshape(n // 2 * ROW_TILE, LANES)


TOKENS_PER_ITER = 8


def _gather_tiles(idx_ref, tab_ref, g_ref, t):
    for k in range(PEER_SLOTS):
        start = pl.multiple_of(idx_ref[t, k], ROW_TILE)
        g_ref[k * ROW_TILE:(k + 1) * ROW_TILE, :] = tab_ref[pl.ds(start, ROW_TILE), :]


def _peer_u_kernel(idx_ref, x_ref, hi_ref, gate_ref, fold_ref, tab_ref, w_ref, sel_ref, *g_refs):
    tb = x_ref.shape[0]
    shape = (ROW_TILE, PEER_SLOTS * PAIR_ROWS)
    row_in_tile = lax.broadcasted_iota(jnp.int32, shape, 1) & (PAIR_ROWS - 1)
    chunk = lax.broadcasted_iota(jnp.int32, shape, 0)
    keep = (row_in_tile >> 1) == chunk

    def token_pair(i, carry):
        for j, g_ref in enumerate(g_refs):
            t = i * TOKENS_PER_ITER + j
            _gather_tiles(idx_ref, tab_ref, g_ref, t)
            x8 = x_ref[pl.ds(t, 1), :].reshape(ROW_TILE, LANES)
            x16 = jnp.concatenate([x8, jnp.zeros_like(x8)], axis=0).astype(BF16)
            out = lax.dot_general(x16, pltpu.bitcast(g_ref[...], BF16), (((1,), (1,)), ((), ())),
                                  preferred_element_type=F32)
            sel_ref[pl.ds(t, 1), :] = jnp.sum(jnp.where(keep, out[:ROW_TILE], 0.0), axis=0, keepdims=True)
        return carry

    lax.fori_loop(0, tb // TOKENS_PER_ITER, token_pair, 0)
    sel = sel_ref[...]
    sel_hi = sel.astype(BF16)
    sel_lo = (sel - sel_hi.astype(F32)).astype(BF16)
    r = (jnp.dot(sel_hi, fold_ref[...], preferred_element_type=F32)
         + jnp.dot(sel_lo, fold_ref[...], preferred_element_type=F32))
    a = jnp.where(hi_ref[...] == 1, r[:, PEER_SLOTS:], r[:, :PEER_SLOTS])
    w_ref[...] = gate_ref[...] * jax.nn.gelu(a)


def _peer_u(idx, xn, hi, gate, tab, tb):
    t = idx.shape[0]
    eye = jnp.eye(PEER_SLOTS, dtype=BF16)
    parity = (jnp.arange(PAIR_ROWS) & 1).astype(BF16)
    fold = jnp.concatenate([jnp.kron(eye, (1 - parity)[:, None]), jnp.kron(eye, parity[:, None])], axis=1)
    g_shape = pltpu.VMEM((PEER_SLOTS * ROW_TILE, LANES), jnp.uint32)
    return pl.pallas_call(
        _peer_u_kernel,
        grid=(t // tb,),
        in_specs=[
            pl.BlockSpec((tb, PEER_SLOTS), lambda i: (i, 0), memory_space=pltpu.SMEM),
            pl.BlockSpec((tb, D_MODEL), lambda i: (i, 0)),
            pl.BlockSpec((tb, PEER_SLOTS), lambda i: (i, 0)),
            pl.BlockSpec((tb, PEER_SLOTS), lambda i: (i, 0)),
            pl.BlockSpec(fold.shape, lambda i: (0, 0), pipeline_mode=pl.Buffered(1)),
            pl.BlockSpec(tab.shape, lambda i: (0, 0), pipeline_mode=pl.Buffered(1)),
        ],
        out_specs=pl.BlockSpec((tb, PEER_SLOTS), lambda i: (i, 0)),
        out_shape=jax.ShapeDtypeStruct((t, PEER_SLOTS), F32),
        scratch_shapes=[pltpu.VMEM((tb, PEER_SLOTS * PAIR_ROWS), F32)] + [g_shape] * TOKENS_PER_ITER,
        compiler_params=pltpu.CompilerParams(dimension_semantics=("arbitrary",),
                                             vmem_limit_bytes=PEER_VMEM_LIMIT),
        name="peer_u",
    )(idx, xn, hi, gate, fold, tab)


def _peer_v_kernel(idx_ref, hi_ref, w_ref, h_ref, route_ref, tab_ref, o_ref, wx_ref, *g_refs):
    tb = hi_ref.shape[0]
    w = w_ref[...]
    w_hi = jnp.where(hi_ref[...] == 1, w, 0.0)
    w2 = jnp.concatenate([w - w_hi, w_hi], axis=1).astype(BF16)
    wx_ref[...] = jnp.dot(w2, route_ref[...], preferred_element_type=F32)
    shape = (PAIR_ROWS, PEER_SLOTS * PAIR_ROWS)
    row_in_tile = lax.broadcasted_iota(jnp.int32, shape, 1) & (PAIR_ROWS - 1)
    chunk = lax.broadcasted_iota(jnp.int32, shape, 0)
    keep = (row_in_tile >> 1) == chunk

    def token_pair(i, carry):
        for j, g_ref in enumerate(g_refs):
            t = i * TOKENS_PER_ITER + j
            _gather_tiles(idx_ref, tab_ref, g_ref, t)
            lhs = jnp.where(keep, wx_ref[pl.ds(t, 1), :], 0.0).astype(BF16)
            o = jnp.dot(lhs, pltpu.bitcast(g_ref[...], BF16), preferred_element_type=F32)
            o_ref[pl.ds(t, 1), :] = h_ref[pl.ds(t, 1), :] + o[:ROW_TILE].reshape(1, D_MODEL)
        return carry

    lax.fori_loop(0, tb // TOKENS_PER_ITER, token_pair, 0)


def _peer_v(idx, hi, w, h2d, tab, tb):
    t = idx.shape[0]
    eye = jnp.eye(PEER_SLOTS, dtype=BF16)
    parity = (jnp.arange(PAIR_ROWS) & 1).astype(BF16)
    route = jnp.concatenate([jnp.kron(eye, (1 - parity)[None, :]), jnp.kron(eye, parity[None, :])], axis=0)
    return pl.pallas_call(
        _peer_v_kernel,
        grid=(t // tb,),
        in_specs=[
            pl.BlockSpec((tb, PEER_SLOTS), lambda i: (i, 0), memory_space=pltpu.SMEM),
            pl.BlockSpec((tb, PEER_SLOTS), lambda i: (i, 0)),
            pl.BlockSpec((tb, PEER_SLOTS), lambda i: (i, 0)),
            pl.BlockSpec((tb, D_MODEL), lambda i: (i, 0)),
            pl.BlockSpec(route.shape, lambda i: (0, 0), pipeline_mode=pl.Buffered(1)),
            pl.BlockSpec(tab.shape, lambda i: (0, 0), pipeline_mode=pl.Buffered(1)),
        ],
        out_specs=pl.BlockSpec((tb, D_MODEL), lambda i: (i, 0)),
        out_shape=jax.ShapeDtypeStruct((t, D_MODEL), F32),
        scratch_shapes=[pltpu.VMEM((tb, PEER_SLOTS * PAIR_ROWS), F32)]
        + [pltpu.VMEM((PEER_SLOTS * ROW_TILE, LANES), jnp.uint32)] * TOKENS_PER_ITER,
        compiler_params=pltpu.CompilerParams(dimension_semantics=("arbitrary",),
                                             vmem_limit_bytes=PEER_VMEM_LIMIT),
        name="peer_v",
    )(idx, hi, w, h2d, route, tab)


def _peer_experts(h2d, xn2d, experts, gate, u_tab, v_tab, tb_u=128, tb_v=128):
    t, d = xn2d.shape
    half = u_tab.shape[0] // 2
    hi = (experts >= half).astype(jnp.int32)
    idx = (experts - hi * half) * ROW_TILE
    w = _peer_u(idx, xn2d, hi, gate, _pack_expert_table(u_tab), tb_u)
    return _peer_v(idx, hi, w, h2d, _pack_expert_table(v_tab), tb_v)


def kernel(x, mem, positions, attn_norm_g, w_in, mla_q_norm_g, w_uq, mla_kv_norm_g, w_ukv,
           mla_qn_g, mla_kn_g, w_o_mla, ret_gn_g, w_o_ret, mem_norm_g, w_mem_kv, x_qn_g, x_kn_g,
           w_o_cross, w_out, ffn_norm_g, peer_w_q, peer_keys, peer_u, peer_v):
    B, S, D = x.shape
    T = B * S
    h2d = x.reshape(T, D)
    offs = [0]
    for sz in IN_SPLITS:
        offs.append(offs[-1] + sz)
    for layer in range(w_in.shape[0]):
        w_pieces = [w_in[layer][:, offs[i]:offs[i + 1]].astype(BF16) for i in range(len(IN_SPLITS))]
        out_dtypes = [F32, F32, F32, F32, F32, BF16, BF16, BF16, BF16]
        c_q, c_kv, k_r, r_q, r_k, r_v, r_g, x_q, gates = _in_proj(h2d, attn_norm_g[layer], w_pieces, out_dtypes)
        o_mla = _mla_branch(c_q, c_kv, k_r, positions, mla_q_norm_g[layer], w_uq[layer],
                            mla_kv_norm_g[layer], w_ukv[layer], mla_qn_g[layer], mla_kn_g[layer], B, S)
        seq = lambda a: a.reshape(B, S, -1)
        o_ret = _retention_branch(seq(r_q), seq(r_k), seq(r_v), seq(r_g), positions, ret_gn_g[layer])
        mk, mv = _mem_kv(mem, mem_norm_g[layer], w_mem_kv[layer], x_kn_g[layer])
        o_mem = _mem_attn(seq(x_q), mk, mv, x_qn_g[layer])
        h2d = _merge(h2d, gates, o_mla, o_ret.reshape(T, -1), o_mem.reshape(T, -1),
                     w_o_mla[layer], w_o_ret[layer], w_o_cross[layer], w_out[layer])
        xn2d, experts_t, gate_t = _peer_select(h2d, ffn_norm_g[layer], peer_w_q[layer], peer_keys[layer])
        h2d = _peer_experts(h2d, xn2d, experts_t.T, gate_t.T, peer_u[layer], peer_v[layer])
    return h2d.reshape(B, S, D)
```

```python
import functools

import jax
import jax.numpy as jnp
from jax import lax
from jax.experimental import pallas as pl
from jax.experimental.pallas import tpu as pltpu

D_MODEL = 1024
MEM_LEN = 256
MLA_HEADS = 8
MLA_Q_RANK = 384
MLA_KV_RANK = 128
MLA_NOPE = 64
MLA_ROPE = 32
MLA_V = 64
RET_HEADS = 4
RET_DK = 64
RET_DV = 128
RET_CHUNK = 128
X_HEADS = 4
X_DH = 128
N_BRANCH = 3
PEER_HEADS = 8
PEER_N_KEYS = 128
PEER_TOPK = 16
PEER_DQ = 256
PEER_DHALF = PEER_DQ // 2
ROPE_BASE = 10000.0
EPS = 1e-6
IN_SPLITS = (MLA_Q_RANK, MLA_KV_RANK, MLA_ROPE,
             RET_HEADS * RET_DK, RET_HEADS * RET_DK, RET_HEADS * RET_DV, RET_HEADS * RET_DV,
             X_HEADS * X_DH, N_BRANCH * D_MODEL)

LANES = 128
VMEM_LIMIT = 48 * 1024 * 1024
PEER_VMEM_LIMIT = 56 * 1024 * 1024
BF16 = jnp.bfloat16
F32 = jnp.float32


def _norm(x, g):
    return x * lax.rsqrt(jnp.mean(x * x, axis=-1, keepdims=True) + EPS) * g


def _row_specs(arrays, tm):
    return [pl.BlockSpec((tm, a.shape[1]), lambda i: (i, 0)) for a in arrays]


def _const_specs(arrays):
    return [pl.BlockSpec(a.shape, lambda i, n=a.ndim: (0,) * n) for a in arrays]


def _in_proj_kernel(x_ref, g_ref, *refs):
    n_w = len(refs) // 2
    w_refs, o_refs = refs[:n_w], refs[n_w:]
    nb = _norm(x_ref[...], g_ref[...]).astype(BF16)
    for w_ref, o_ref in zip(w_refs, o_refs):
        o_ref[...] = jnp.dot(nb, w_ref[...], preferred_element_type=F32).astype(o_ref.dtype)


def _in_proj(x2d, g, w_pieces, out_dtypes, tm=512):
    t, d = x2d.shape
    consts = [g.reshape(1, d)] + list(w_pieces)
    return pl.pallas_call(
        _in_proj_kernel,
        grid=(t // tm,),
        in_specs=_row_specs([x2d], tm) + _const_specs(consts),
        out_specs=[pl.BlockSpec((tm, w.shape[1]), lambda i: (i, 0)) for w in w_pieces],
        out_shape=[jax.ShapeDtypeStruct((t, w.shape[1]), dt) for w, dt in zip(w_pieces, out_dtypes)],
        compiler_params=pltpu.CompilerParams(dimension_semantics=("parallel",),
                                             vmem_limit_bytes=VMEM_LIMIT),
        name="in_proj",
    )(x2d, *consts)


def _rope_patterns(positions, n_rope, lead, width):
    inv_freq = ROPE_BASE ** (-jnp.arange(0, n_rope, 2, dtype=F32) / n_rope)
    ang = positions.astype(F32).reshape(-1, 1) * inv_freq
    cos, sin = jnp.cos(ang), jnp.sin(ang)
    t = ang.shape[0]
    tail = width - lead - n_rope
    cosp = jnp.concatenate([jnp.ones((t, lead)), cos, cos, jnp.zeros((t, tail))], axis=1)
    sinp = jnp.concatenate([jnp.zeros((t, lead)), -sin, sin, jnp.zeros((t, tail))], axis=1)
    rep = LANES // width
    return jnp.tile(cosp, (1, rep)), jnp.tile(sinp, (1, rep))


def _rotate_half(x, cosp, sinp, n_rope, lead, width):
    half = n_rope // 2
    lane = lax.broadcasted_iota(jnp.int32, x.shape, 1) & (width - 1)
    partner = jnp.where(lane < lead + half, pltpu.roll(x, LANES - half, 1), pltpu.roll(x, half, 1))
    return x * cosp + partner * sinp


MLA_QK = MLA_NOPE + MLA_ROPE


def _pad_head_cols(w, per_head):
    r = w.shape[0]
    w = w.reshape(r, MLA_HEADS, per_head)
    return jnp.pad(w, ((0, 0), (0, 0), (0, LANES - per_head))).reshape(r, MLA_HEADS * LANES)


def _mla_prep_kernel(cq_ref, ckv_ref, kr_ref, cos_ref, sin_ref, gq_ref, gkv_ref, wuq_ref, wuk_ref,
                     wuv_ref, qn_ref, kn_ref, q_ref, k_ref, v_ref):
    cqn = _norm(cq_ref[...], gq_ref[...]).astype(BF16)
    ckvn = _norm(ckv_ref[...], gkv_ref[...]).astype(BF16)
    qp = jnp.dot(cqn, wuq_ref[...], preferred_element_type=F32)
    kp = jnp.dot(ckvn, wuk_ref[...], preferred_element_type=F32)
    v_ref[...] = jnp.dot(ckvn, wuv_ref[...], preferred_element_type=F32).astype(v_ref.dtype)
    tm = qp.shape[0]
    kr = jnp.concatenate([jnp.zeros((tm, MLA_NOPE), F32), kr_ref[...],
                          jnp.zeros((tm, LANES - MLA_QK), F32)], axis=1)
    cosp, sinp = cos_ref[...], sin_ref[...]

    def qk_norm_rope(xh, g, scale):
        ss = jnp.sum(xh * xh, axis=-1, keepdims=True) * (1.0 / MLA_QK)
        xn = xh * lax.rsqrt(ss + EPS) * g
        return _rotate_half(xn, cosp, sinp, MLA_ROPE, MLA_NOPE, LANES) * scale

    for hd in range(MLA_HEADS):
        sl = slice(hd * LANES, (hd + 1) * LANES)
        q_ref[:, sl] = qk_norm_rope(qp[:, sl], qn_ref[...], MLA_QK ** -0.5).astype(q_ref.dtype)
        k_ref[:, sl] = qk_norm_rope(kp[:, sl] + kr, kn_ref[...], 1.0).astype(k_ref.dtype)


def _mla_prep(c_q, c_kv, k_r, cosp, sinp, q_norm_g, w_uq, kv_norm_g, w_ukv, qn_g, kn_g, tm=512):
    t = c_q.shape[0]
    w_ukv3 = w_ukv.reshape(MLA_KV_RANK, MLA_HEADS, MLA_NOPE + MLA_V)
    w_uk = _pad_head_cols(w_ukv3[:, :, :MLA_NOPE].reshape(MLA_KV_RANK, -1), MLA_NOPE).astype(BF16)
    w_uv = w_ukv3[:, :, MLA_NOPE:].reshape(MLA_KV_RANK, MLA_HEADS * MLA_V).astype(BF16)
    w_uqp = _pad_head_cols(w_uq, MLA_QK).astype(BF16)
    pad_g = lambda g: jnp.pad(g, (0, LANES - MLA_QK)).reshape(1, LANES)
    consts = [q_norm_g.reshape(1, -1), kv_norm_g.reshape(1, -1), w_uqp, w_uk, w_uv, pad_g(qn_g), pad_g(kn_g)]
    rows = [c_q, c_kv, k_r, cosp, sinp]
    wide = MLA_HEADS * LANES
    return pl.pallas_call(
        _mla_prep_kernel,
        grid=(t // tm,),
        in_specs=_row_specs(rows, tm) + _const_specs(consts),
        out_specs=[pl.BlockSpec((tm, wide), lambda i: (i, 0)),
                   pl.BlockSpec((tm, wide), lambda i: (i, 0)),
                   pl.BlockSpec((tm, MLA_HEADS * MLA_V), lambda i: (i, 0))],
        out_shape=[jax.ShapeDtypeStruct((t, wide), BF16), jax.ShapeDtypeStruct((t, wide), BF16),
                   jax.ShapeDtypeStruct((t, MLA_HEADS * MLA_V), BF16)],
        compiler_params=pltpu.CompilerParams(dimension_semantics=("parallel",),
                                             vmem_limit_bytes=VMEM_LIMIT),
        name="mla_prep",
    )(*rows, *consts)


def _mla_attn_kernel(qt_ref, k_ref, vt_ref, o_ref, st_a, st_b, m_ref, l_ref, acc_ref, *, tile):
    qi = pl.program_id(2)
    key = lax.broadcasted_iota(jnp.int32, (tile, tile), 0)
    qry = lax.broadcasted_iota(jnp.int32, (tile, tile), 1)
    head_lanes = [slice(a * LANES, (a + 1) * LANES) for a in range(2)]
    qts = [qt_ref[0, lanes, :] for lanes in head_lanes]

    def scores_into(buf, j):
        start = pl.multiple_of(j * tile, tile)
        for a, lanes in enumerate(head_lanes):
            buf[a] = jnp.dot(k_ref[0, pl.ds(start, tile), lanes], qts[a], preferred_element_type=F32)

    def consume(buf, j, masked):
        start = pl.multiple_of(j * tile, tile)
        vtj = vt_ref[0, :, pl.ds(start, tile)]
        for a in range(2):
            st = buf[a]
            if masked:
                st = jnp.where(key <= qry, st, -1e30)
            m = m_ref[a]
            m_new = jnp.maximum(m, jnp.max(st, axis=0, keepdims=True))
            alpha = jnp.exp(m - m_new)
            p = jnp.exp(st - m_new)
            m_ref[a] = m_new
            l_ref[a] = alpha * l_ref[a] + jnp.sum(p, axis=0, keepdims=True)
            pv = jnp.dot(vtj, p.astype(BF16), preferred_element_type=F32)
            acc_ref[a] = alpha * acc_ref[a] + pv

    m_ref[...] = jnp.full(m_ref.shape, -1e30, F32)
    l_ref[...] = jnp.zeros(l_ref.shape, F32)
    acc_ref[...] = jnp.zeros(acc_ref.shape, F32)
    scores_into(st_a, 0)

    def two_tiles(i, carry):
        j = 2 * i
        scores_into(st_b, j + 1)
        consume(st_a, j, False)
        scores_into(st_a, j + 2)
        consume(st_b, j + 1, False)
        return carry

    lax.fori_loop(0, qi // 2, two_tiles, 0)

    @pl.when(qi % 2 == 0)
    def _():
        consume(st_a, qi, True)

    @pl.when(qi % 2 == 1)
    def _():
        scores_into(st_b, qi)
        consume(st_a, qi - 1, False)
        consume(st_b, qi, True)

    dv_row = lax.broadcasted_iota(jnp.int32, acc_ref.shape[1:], 0)
    out_t = jnp.where(dv_row < MLA_V, acc_ref[0] / l_ref[0], acc_ref[1] / l_ref[1])
    o_ref[0] = out_t.T.astype(o_ref.dtype)


def _mla_attn(q, k, v, tile=512):
    b, s, _ = q.shape
    pair = 2 * LANES
    qt, vt = q.transpose(0, 2, 1), v.transpose(0, 2, 1)
    return pl.pallas_call(
        functools.partial(_mla_attn_kernel, tile=tile),
        grid=(b, MLA_HEADS // 2, s // tile),
        in_specs=[pl.BlockSpec((1, pair, tile), lambda bi, hp, qi: (bi, hp, qi)),
                  pl.BlockSpec((1, s, pair), lambda bi, hp, qi: (bi, 0, hp)),
                  pl.BlockSpec((1, 2 * MLA_V, s), lambda bi, hp, qi: (bi, hp, 0))],
        out_specs=pl.BlockSpec((1, tile, 2 * MLA_V), lambda bi, hp, qi: (bi, qi, hp)),
        out_shape=jax.ShapeDtypeStruct((b, s, MLA_HEADS * MLA_V), BF16),
        scratch_shapes=[pltpu.VMEM((2, tile, tile), F32), pltpu.VMEM((2, tile, tile), F32),
                        pltpu.VMEM((2, 1, tile), F32), pltpu.VMEM((2, 1, tile), F32),
                        pltpu.VMEM((2, 2 * MLA_V, tile), F32)],
        compiler_params=pltpu.CompilerParams(dimension_semantics=("parallel", "parallel", "arbitrary"),
                                             vmem_limit_bytes=VMEM_LIMIT),
        name="mla_attn",
    )(qt, k, vt)


def _mla_branch(c_q, c_kv, k_r, positions, q_norm_g, w_uq, kv_norm_g, w_ukv, qn_g, kn_g, b, s):
    cosp, sinp = _rope_patterns(positions, MLA_ROPE, MLA_NOPE, LANES)
    q, k, v = _mla_prep(c_q, c_kv, k_r, cosp, sinp, q_norm_g, w_uq, kv_norm_g, w_ukv, qn_g, kn_g)
    o = _mla_attn(q.reshape(b, s, -1), k.reshape(b, s, -1), v.reshape(b, s, -1))
    return o.reshape(b * s, -1)


def _retention_kernel(q_ref, k_ref, v_ref, g_ref, cos_ref, sin_ref, decay_ref, xi_ref, zeta_ref,
                      cd_ref, gn_ref, o_ref):
    L = RET_CHUNK
    n_chunks = q_ref.shape[1] // L
    lane = lax.broadcasted_iota(jnp.int32, (L, LANES), 1)

    def chunk_body(c, states):
        rows = pl.ds(pl.multiple_of(c * L, L), L)
        cosp, sinp = cos_ref[0, rows, :], sin_ref[0, rows, :]
        qc = _rotate_half(q_ref[0, rows, :], cosp, sinp, RET_DK, 0, RET_DK)
        kc = _rotate_half(k_ref[0, rows, :], cosp, sinp, RET_DK, 0, RET_DK) * (RET_DK ** -0.5)
        qx = qc * xi_ref[0]
        kz = kc * zeta_ref[0]
        new_states = []
        for a in range(2):
            mine = (lane >= a * RET_DK) & (lane < (a + 1) * RET_DK)
            qa = jnp.where(mine, qc, 0.0).astype(BF16)
            ka = jnp.where(mine, kc, 0.0).astype(BF16)
            va = v_ref[0, rows, a * RET_DV:(a + 1) * RET_DV]
            inner = lax.dot_general(qa, ka, (((1,), (1,)), ((), ())),
                                    preferred_element_type=F32) * decay_ref[a]
            cross = jnp.dot(jnp.where(mine, qx, 0.0).astype(BF16), states[a].astype(BF16),
                            preferred_element_type=F32)
            o = jnp.dot(inner.astype(BF16), va, preferred_element_type=F32) + cross
            kzt = jnp.where(mine, kz, 0.0).T.astype(BF16)
            incr = jnp.dot(kzt, va, preferred_element_type=F32)
            new_states.append(cd_ref[a, 0:1, :] * states[a] + incr)
            o = _norm(o, gn_ref[a])
            gate = g_ref[0, rows, a * RET_DV:(a + 1) * RET_DV].astype(F32)
            o_ref[0, rows, a * RET_DV:(a + 1) * RET_DV] = (o * gate * jax.nn.sigmoid(gate)).astype(o_ref.dtype)
        return tuple(new_states)

    zero = jnp.zeros((LANES, RET_DV), F32)
    lax.fori_loop(0, n_chunks, chunk_body, (zero, zero))


def _retention_branch(r_q, r_k, r_v, r_g, positions, gn_g):
    b, s, _ = r_q.shape
    L = RET_CHUNK
    cosp, sinp = _rope_patterns(positions, RET_DK, 0, RET_DK)
    cosp, sinp = cosp.reshape(b, s, LANES), sinp.reshape(b, s, LANES)
    log_gamma = jnp.log(1.0 - 2.0 ** (-5.0 - jnp.arange(RET_HEADS, dtype=F32)))
    idx = jnp.arange(L, dtype=F32)
    diff = idx[:, None] - idx[None, :]
    decay = jnp.where(diff >= 0, jnp.exp(log_gamma[:, None, None] * jnp.maximum(diff, 0.0)), 0.0)
    xi = jnp.exp(log_gamma[:, None] * (idx + 1.0))
    zeta = jnp.exp(log_gamma[:, None] * (L - 1.0 - idx))
    per_pair = lambda a: jnp.repeat(a.reshape(RET_HEADS // 2, 2, L), RET_DK, axis=1).transpose(0, 2, 1)
    cd = jnp.broadcast_to(jnp.exp(log_gamma * L)[:, None, None], (RET_HEADS, 8, RET_DV))
    gn = gn_g.reshape(RET_HEADS, 1, RET_DV)
    n_pairs = RET_HEADS // 2
    seq = lambda width: pl.BlockSpec((1, s, width), lambda bi, hp: (bi, 0, hp))
    return pl.pallas_call(
        _retention_kernel,
        grid=(b, n_pairs),
        in_specs=[seq(LANES), seq(LANES), seq(2 * RET_DV), seq(2 * RET_DV),
                  pl.BlockSpec((1, s, LANES), lambda bi, hp: (bi, 0, 0)),
                  pl.BlockSpec((1, s, LANES), lambda bi, hp: (bi, 0, 0)),
                  pl.BlockSpec((2, L, L), lambda bi, hp: (hp, 0, 0)),
                  pl.BlockSpec((1, L, LANES), lambda bi, hp: (hp, 0, 0)),
                  pl.BlockSpec((1, L, LANES), lambda bi, hp: (hp, 0, 0)),
                  pl.BlockSpec((2, 8, RET_DV), lambda bi, hp: (hp, 0, 0)),
                  pl.BlockSpec((2, 1, RET_DV), lambda bi, hp: (hp, 0, 0))],
        out_specs=seq(2 * RET_DV),
        out_shape=jax.ShapeDtypeStruct((b, s, RET_HEADS * RET_DV), BF16),
        compiler_params=pltpu.CompilerParams(dimension_semantics=("parallel", "parallel"),
                                             vmem_limit_bytes=VMEM_LIMIT),
        name="retention",
    )(r_q, r_k, r_v, r_g, cosp, sinp, decay, per_pair(xi), per_pair(zeta), cd, gn)


def _mem_kv_kernel(mem_ref, g_ref, w_ref, kn_ref, k_ref, v_ref):
    kv = jnp.dot(_norm(mem_ref[0], g_ref[...]).astype(BF16), w_ref[...], preferred_element_type=F32)
    width = X_HEADS * X_DH
    for hd in range(X_HEADS):
        sl = slice(hd * X_DH, (hd + 1) * X_DH)
        k_ref[0, :, sl] = _norm(kv[:, sl], kn_ref[...]).astype(k_ref.dtype)
    v_ref[0] = kv[:, width:].astype(v_ref.dtype)


def _mem_kv(mem, mem_g, w_mem_kv, kn_g):
    b, m, d = mem.shape
    width = X_HEADS * X_DH
    w4 = w_mem_kv.reshape(d, X_HEADS, 2, X_DH)
    w = jnp.concatenate([w4[:, :, 0].reshape(d, width), w4[:, :, 1].reshape(d, width)], axis=1).astype(BF16)
    consts = [mem_g.reshape(1, d), w, kn_g.reshape(1, X_DH)]
    blk = pl.BlockSpec((1, m, width), lambda i: (i, 0, 0))
    return pl.pallas_call(
        _mem_kv_kernel,
        grid=(b,),
        in_specs=[pl.BlockSpec((1, m, d), lambda i: (i, 0, 0))] + _const_specs(consts),
        out_specs=[blk, blk],
        out_shape=[jax.ShapeDtypeStruct((b, m, width), BF16)] * 2,
        compiler_params=pltpu.CompilerParams(dimension_semantics=("parallel",),
                                             vmem_limit_bytes=VMEM_LIMIT),
        name="mem_kv",
    )(mem, *consts)


def _mem_attn_kernel(q_ref, k_ref, v_ref, qn_ref, o_ref):
    for hd in range(X_HEADS):
        sl = slice(hd * X_DH, (hd + 1) * X_DH)
        q = (_norm(q_ref[0, :, sl].astype(F32), qn_ref[...]) * (X_DH ** -0.5)).astype(BF16)
        s = lax.dot_general(q, k_ref[0, :, sl], (((1,), (1,)), ((), ())), preferred_element_type=F32)
        p = jnp.exp(s - jnp.max(s, axis=-1, keepdims=True))
        o = jnp.dot(p.astype(BF16), v_ref[0, :, sl], preferred_element_type=F32)
        o_ref[0, :, sl] = (o / jnp.sum(p, axis=-1, keepdims=True)).astype(o_ref.dtype)


def _mem_attn(x_q, k, v, qn_g, tm=512):
    b, s, width = x_q.shape
    m = k.shape[1]
    kv_spec = pl.BlockSpec((1, m, width), lambda bi, i: (bi, 0, 0))
    return pl.pallas_call(
        _mem_attn_kernel,
        grid=(b, s // tm),
        in_specs=[pl.BlockSpec((1, tm, width), lambda bi, i: (bi, i, 0)), kv_spec, kv_spec,
                  pl.BlockSpec((1, X_DH), lambda bi, i: (0, 0))],
        out_specs=pl.BlockSpec((1, tm, width), lambda bi, i: (bi, i, 0)),
        out_shape=jax.ShapeDtypeStruct((b, s, width), BF16),
        compiler_params=pltpu.CompilerParams(dimension_semantics=("parallel", "parallel"),
                                             vmem_limit_bytes=VMEM_LIMIT),
        name="mem_attn",
    )(x_q, k, v, qn_g.reshape(1, X_DH))


def _merge_kernel(x_ref, gates_ref, oa_ref, ob_ref, oc_ref, wa_ref, wb_ref, wc_ref, wout_ref, h_ref):
    merged = None
    for i, (o_ref, w_ref) in enumerate(((oa_ref, wa_ref), (ob_ref, wb_ref), (oc_ref, wc_ref))):
        y = jnp.dot(o_ref[...], w_ref[...], preferred_element_type=F32)
        gate = jax.nn.sigmoid(gates_ref[:, i * D_MODEL:(i + 1) * D_MODEL].astype(F32))
        merged = gate * y if merged is None else merged + gate * y
    h_ref[...] = x_ref[...] + jnp.dot(merged.astype(BF16), wout_ref[...], preferred_element_type=F32)


def _merge(x2d, gates, o_mla, o_ret, o_mem, w_o_mla, w_o_ret, w_o_cross, w_out, tm=512):
    t, d = x2d.shape
    rows = [x2d, gates, o_mla, o_ret, o_mem]
    consts = [w.astype(BF16) for w in (w_o_mla, w_o_ret, w_o_cross, w_out)]
    return pl.pallas_call(
        _merge_kernel,
        grid=(t // tm,),
        in_specs=_row_specs(rows, tm) + _const_specs(consts),
        out_specs=pl.BlockSpec((tm, d), lambda i: (i, 0)),
        out_shape=jax.ShapeDtypeStruct((t, d), F32),
        compiler_params=pltpu.CompilerParams(dimension_semantics=("parallel",),
                                             vmem_limit_bytes=VMEM_LIMIT),
        name="merge",
    )(*rows, *consts)


PEER_SLOTS = PEER_HEADS * PEER_TOPK
PAIR_CANDS = [(i, j) for i in range(PEER_TOPK) for j in range(PEER_TOPK)
              if (i + 1) * (j + 1) <= PEER_TOPK]
N_CANDS = len(PAIR_CANDS)
NEG_INF = float("-inf")


def _argmax_tree(nodes):
    while len(nodes) > 1:
        nxt = []
        for i in range(0, len(nodes) - 1, 2):
            a, b = nodes[i], nodes[i + 1]
            take_b = b[0] > a[0]
            nxt.append(tuple(jnp.where(take_b, y, x) for x, y in zip(a, b)))
        if len(nodes) % 2:
            nxt.append(nodes[-1])
        nodes = nxt
    return nodes[0]


def _top_rounds(rows, n_rounds):
    vals, idxs = [], []
    for _ in range(n_rounds):
        m, idx = _argmax_tree([(r, n) for n, r in enumerate(rows)])
        rows = [jnp.where(idx == n, NEG_INF, r) for n, r in enumerate(rows)]
        vals.append(m)
        idxs.append(idx)
    return vals, idxs


def _peer_select_kernel(h_ref, g_ref, wq_ref, keys_ref, xn_ref, exp_ref, gate_ref, q_ref):
    xn = _norm(h_ref[...], g_ref[...])
    xn_ref[...] = xn
    xnb = xn.astype(BF16)
    q_ref[...] = jnp.dot(xnb, wq_ref[...], preferred_element_type=F32).astype(BF16)
    tb = xnb.shape[0]
    tok = (tb // LANES, LANES)

    def head_body(hd, carry):
        sv, si = [], []
        for c in range(2):
            col = pl.multiple_of(hd * PEER_DQ + c * PEER_DHALF, PEER_DHALF)
            qh = q_ref[:, pl.ds(col, PEER_DHALF)]
            st = lax.dot_general(keys_ref[hd, c], qh, (((1,), (1,)), ((), ())),
                                 preferred_element_type=F32)
            st = st.reshape((PEER_N_KEYS,) + tok)
            v, i = _top_rounds([st[n] for n in range(PEER_N_KEYS)], PEER_TOPK)
            sv.append(v)
            si.append(i)
        cands = [sv[0][i] + sv[1][j] for i, j in PAIR_CANDS]
        cand_e = [si[0][i] * PEER_N_KEYS + si[1][j] for i, j in PAIR_CANDS]
        best, experts = [], []
        for _ in range(PEER_TOPK):
            m, pos, e = _argmax_tree([(c, p, ce) for p, (c, ce) in enumerate(zip(cands, cand_e))])
            cands = [jnp.where(pos == p, NEG_INF, c) for p, c in enumerate(cands)]
            best.append(m)
            experts.append(e)
        p = [jnp.exp(b - best[0]) for b in best]
        denom = p[0]
        for pk in p[1:]:
            denom = denom + pk
        row0 = pl.multiple_of(hd * PEER_TOPK, PEER_TOPK)
        exp_ref[pl.ds(row0, PEER_TOPK)] = jnp.stack(experts, axis=0)
        gate_ref[pl.ds(row0, PEER_TOPK)] = jnp.stack([pk / denom for pk in p], axis=0)
        return carry

    lax.fori_loop(0, PEER_HEADS, head_body, 0)


def _peer_select(h2d, g, w_q, sub_keys, tb=1024):
    t, d = h2d.shape
    consts = [g.reshape(1, d), w_q.astype(BF16), sub_keys.astype(BF16)]
    slot_spec = pl.BlockSpec((PEER_SLOTS, tb // LANES, LANES), lambda i: (0, i, 0))
    xn, experts, gates = pl.pallas_call(
        _peer_select_kernel,
        grid=(t // tb,),
        in_specs=_row_specs([h2d], tb) + _const_specs(consts),
        out_specs=[pl.BlockSpec((tb, d), lambda i: (i, 0)), slot_spec, slot_spec],
        out_shape=[
            jax.ShapeDtypeStruct((t, d), F32),
            jax.ShapeDtypeStruct((PEER_SLOTS, t // LANES, LANES), jnp.int32),
            jax.ShapeDtypeStruct((PEER_SLOTS, t // LANES, LANES), F32),
        ],
        scratch_shapes=[pltpu.VMEM((tb, PEER_HEADS * PEER_DQ), BF16)],
        compiler_params=pltpu.CompilerParams(dimension_semantics=("parallel",),
                                             vmem_limit_bytes=VMEM_LIMIT),
        name="peer_select",
    )(h2d, *consts)
    return xn, experts.reshape(PEER_SLOTS, t), gates.reshape(PEER_SLOTS, t)


ROW_TILE = D_MODEL // LANES
PAIR_ROWS = 2 * ROW_TILE


def _pack_expert_table(tab):
    n, d = tab.shape
    t = tab.astype(BF16).reshape(2, n // 2, ROW_TILE, LANES)
    bits = lax.bitcast_convert_type(t, jnp.uint16).astype(jnp.uint32)
    return (bits[0] | (bits[1] << 16)).reshape(n // 2 * ROW_TILE, LANES)


TOKENS_PER_ITER = 8


def _gather_tiles(idx_ref, tab_ref, g_ref, t):
    for k in range(PEER_SLOTS):
        start = pl.multiple_of(idx_ref[t, k], ROW_TILE)
        g_ref[k * ROW_TILE:(k + 1) * ROW_TILE, :] = tab_ref[pl.ds(start, ROW_TILE), :]


def _peer_u_kernel(idx_ref, x_ref, hi_ref, gate_ref, fold_ref, tab_ref, w_ref, sel_ref, *g_refs):
    tb = x_ref.shape[0]
    shape = (ROW_TILE, PEER_SLOTS * PAIR_ROWS)
    row_in_tile = lax.broadcasted_iota(jnp.int32, shape, 1) & (PAIR_ROWS - 1)
    chunk = lax.broadcasted_iota(jnp.int32, shape, 0)
    keep = (row_in_tile >> 1) == chunk

    def token_pair(i, carry):
        for j, g_ref in enumerate(g_refs):
            t = i * TOKENS_PER_ITER + j
            _gather_tiles(idx_ref, tab_ref, g_ref, t)
            x8 = x_ref[pl.ds(t, 1), :].reshape(ROW_TILE, LANES)
            x16 = jnp.concatenate([x8, jnp.zeros_like(x8)], axis=0).astype(BF16)
            out = lax.dot_general(x16, pltpu.bitcast(g_ref[...], BF16), (((1,), (1,)), ((), ())),
                                  preferred_element_type=F32)
            sel_ref[pl.ds(t, 1), :] = jnp.sum(jnp.where(keep, out[:ROW_TILE], 0.0), axis=0, keepdims=True)
        return carry

    lax.fori_loop(0, tb // TOKENS_PER_ITER, token_pair, 0)
    sel = sel_ref[...]
    sel_hi = sel.astype(BF16)
    sel_lo = (sel - sel_hi.astype(F32)).astype(BF16)
    r = (jnp.dot(sel_hi, fold_ref[...], preferred_element_type=F32)
         + jnp.dot(sel_lo, fold_ref[...], preferred_element_type=F32))
    a = jnp.where(hi_ref[...] == 1, r[:, PEER_SLOTS:], r[:, :PEER_SLOTS])
    w_ref[...] = gate_ref[...] * jax.nn.gelu(a)


def _peer_u(idx, xn, hi, gate, tab, tb):
    t = idx.shape[0]
    eye = jnp.eye(PEER_SLOTS, dtype=BF16)
    parity = (jnp.arange(PAIR_ROWS) & 1).astype(BF16)
    fold = jnp.concatenate([jnp.kron(eye, (1 - parity)[:, None]), jnp.kron(eye, parity[:, None])], axis=1)
    g_shape = pltpu.VMEM((PEER_SLOTS * ROW_TILE, LANES), jnp.uint32)
    return pl.pallas_call(
        _peer_u_kernel,
        grid=(t // tb,),
        in_specs=[
            pl.BlockSpec((tb, PEER_SLOTS), lambda i: (i, 0), memory_space=pltpu.SMEM),
            pl.BlockSpec((tb, D_MODEL), lambda i: (i, 0)),
            pl.BlockSpec((tb, PEER_SLOTS), lambda i: (i, 0)),
            pl.BlockSpec((tb, PEER_SLOTS), lambda i: (i, 0)),
            pl.BlockSpec(fold.shape, lambda i: (0, 0), pipeline_mode=pl.Buffered(1)),
            pl.BlockSpec(tab.shape, lambda i: (0, 0), pipeline_mode=pl.Buffered(1)),
        ],
        out_specs=pl.BlockSpec((tb, PEER_SLOTS), lambda i: (i, 0)),
        out_shape=jax.ShapeDtypeStruct((t, PEER_SLOTS), F32),
        scratch_shapes=[pltpu.VMEM((tb, PEER_SLOTS * PAIR_ROWS), F32)] + [g_shape] * TOKENS_PER_ITER,
        compiler_params=pltpu.CompilerParams(dimension_semantics=("arbitrary",),
                                             vmem_limit_bytes=PEER_VMEM_LIMIT),
        name="peer_u",
    )(idx, xn, hi, gate, fold, tab)


def _peer_v_kernel(idx_ref, hi_ref, w_ref, h_ref, route_ref, tab_ref, o_ref, wx_ref, *g_refs):
    tb = hi_ref.shape[0]
    w = w_ref[...]
    w_hi = jnp.where(hi_ref[...] == 1, w, 0.0)
    w2 = jnp.concatenate([w - w_hi, w_hi], axis=1).astype(BF16)
    wx_ref[...] = jnp.dot(w2, route_ref[...], preferred_element_type=F32)
    shape = (PAIR_ROWS, PEER_SLOTS * PAIR_ROWS)
    row_in_tile = lax.broadcasted_iota(jnp.int32, shape, 1) & (PAIR_ROWS - 1)
    chunk = lax.broadcasted_iota(jnp.int32, shape, 0)
    keep = (row_in_tile >> 1) == chunk

    def token_pair(i, carry):
        for j, g_ref in enumerate(g_refs):
            t = i * TOKENS_PER_ITER + j
            _gather_tiles(idx_ref, tab_ref, g_ref, t)
            lhs = jnp.where(keep, wx_ref[pl.ds(t, 1), :], 0.0).astype(BF16)
            o = jnp.dot(lhs, pltpu.bitcast(g_ref[...], BF16), preferred_element_type=F32)
            o_ref[pl.ds(t, 1), :] = h_ref[pl.ds(t, 1), :] + o[:ROW_TILE].reshape(1, D_MODEL)
        return carry

    lax.fori_loop(0, tb // TOKENS_PER_ITER, token_pair, 0)


def _peer_v(idx, hi, w, h2d, tab, tb):
    t = idx.shape[0]
    eye = jnp.eye(PEER_SLOTS, dtype=BF16)
    parity = (jnp.arange(PAIR_ROWS) & 1).astype(BF16)
    route = jnp.concatenate([jnp.kron(eye, (1 - parity)[None, :]), jnp.kron(eye, parity[None, :])], axis=0)
    return pl.pallas_call(
        _peer_v_kernel,
        grid=(t // tb,),
        in_specs=[
            pl.BlockSpec((tb, PEER_SLOTS), lambda i: (i, 0), memory_space=pltpu.SMEM),
            pl.BlockSpec((tb, PEER_SLOTS), lambda i: (i, 0)),
            pl.BlockSpec((tb, PEER_SLOTS), lambda i: (i, 0)),
            pl.BlockSpec((tb, D_MODEL), lambda i: (i, 0)),
            pl.BlockSpec(route.shape, lambda i: (0, 0), pipeline_mode=pl.Buffered(1)),
            pl.BlockSpec(tab.shape, lambda i: (0, 0), pipeline_mode=pl.Buffered(1)),
        ],
        out_specs=pl.BlockSpec((tb, D_MODEL), lambda i: (i, 0)),
        out_shape=jax.ShapeDtypeStruct((t, D_MODEL), F32),
        scratch_shapes=[pltpu.VMEM((tb, PEER_SLOTS * PAIR_ROWS), F32)]
        + [pltpu.VMEM((PEER_SLOTS * ROW_TILE, LANES), jnp.uint32)] * TOKENS_PER_ITER,
        compiler_params=pltpu.CompilerParams(dimension_semantics=("arbitrary",),
                                             vmem_limit_bytes=PEER_VMEM_LIMIT),
        name="peer_v",
    )(idx, hi, w, h2d, route, tab)


def _peer_experts(h2d, xn2d, experts, gate, u_tab, v_tab, tb_u=128, tb_v=128):
    t, d = xn2d.shape
    half = u_tab.shape[0] // 2
    hi = (experts >= half).astype(jnp.int32)
    idx = (experts - hi * half) * ROW_TILE
    w = _peer_u(idx, xn2d, hi, gate, _pack_expert_table(u_tab), tb_u)
    return _peer_v(idx, hi, w, h2d, _pack_expert_table(v_tab), tb_v)


def kernel(x, mem, positions, attn_norm_g, w_in, mla_q_norm_g, w_uq, mla_kv_norm_g, w_ukv,
           mla_qn_g, mla_kn_g, w_o_mla, ret_gn_g, w_o_ret, mem_norm_g, w_mem_kv, x_qn_g, x_kn_g,
           w_o_cross, w_out, ffn_norm_g, peer_w_q, peer_keys, peer_u, peer_v):
    B, S, D = x.shape
    T = B * S
    h2d = x.reshape(T, D)
    offs = [0]
    for sz in IN_SPLITS:
        offs.append(offs[-1] + sz)
    for layer in range(w_in.shape[0]):
        w_pieces = [w_in[layer][:, offs[i]:offs[i + 1]].astype(BF16) for i in range(len(IN_SPLITS))]
        out_dtypes = [F32, F32, F32, F32, F32, BF16, BF16, BF16, BF16]
        c_q, c_kv, k_r, r_q, r_k, r_v, r_g, x_q, gates = _in_proj(h2d, attn_norm_g[layer], w_pieces, out_dtypes)
        o_mla = _mla_branch(c_q, c_kv, k_r, positions, mla_q_norm_g[layer], w_uq[layer],
                            mla_kv_norm_g[layer], w_ukv[layer], mla_qn_g[layer], mla_kn_g[layer], B, S)
        seq = lambda a: a.reshape(B, S, -1)
        o_ret = _retention_branch(seq(r_q), seq(r_k), seq(r_v), seq(r_g), positions, ret_gn_g[layer])
        mk, mv = _mem_kv(mem, mem_norm_g[layer], w_mem_kv[layer], x_kn_g[layer])
        o_mem = _mem_attn(seq(x_q), mk, mv, x_qn_g[layer])
        h2d = _merge(h2d, gates, o_mla, o_ret.reshape(T, -1), o_mem.reshape(T, -1),
                     w_o_mla[layer], w_o_ret[layer], w_o_cross[layer], w_out[layer])
        xn2d, experts_t, gate_t = _peer_select(h2d, ffn_norm_g[layer], peer_w_q[layer], peer_keys[layer])
        h2d = _peer_experts(h2d, xn2d, experts_t.T, gate_t.T, peer_u[layer], peer_v[layer])
    return h2d.reshape(B, S, D)
```

```python
import functools

import jax
import jax.numpy as jnp
from jax import lax
from jax.experimental import pallas as pl
from jax.experimental.pallas import tpu as pltpu

D_MODEL = 1024
MEM_LEN = 256
MLA_HEADS = 8
MLA_Q_RANK = 384
MLA_KV_RANK = 128
MLA_NOPE = 64
MLA_ROPE = 32
MLA_V = 64
RET_HEADS = 4
RET_DK = 64
RET_DV = 128
RET_CHUNK = 128
X_HEADS = 4
X_DH = 128
N_BRANCH = 3
PEER_HEADS = 8
PEER_N_KEYS = 128
PEER_TOPK = 16
PEER_DQ = 256
PEER_DHALF = PEER_DQ // 2
ROPE_BASE = 10000.0
EPS = 1e-6
IN_SPLITS = (MLA_Q_RANK, MLA_KV_RANK, MLA_ROPE,
             RET_HEADS * RET_DK, RET_HEADS * RET_DK, RET_HEADS * RET_DV, RET_HEADS * RET_DV,
             X_HEADS * X_DH, N_BRANCH * D_MODEL)

LANES = 128
VMEM_LIMIT = 48 * 1024 * 1024
PEER_VMEM_LIMIT = 56 * 1024 * 1024
BF16 = jnp.bfloat16
F32 = jnp.float32


def _norm(x, g):
    return x * lax.rsqrt(jnp.mean(x * x, axis=-1, keepdims=True) + EPS) * g


def _row_specs(arrays, tm):
    return [pl.BlockSpec((tm, a.shape[1]), lambda i: (i, 0)) for a in arrays]


def _const_specs(arrays):
    return [pl.BlockSpec(a.shape, lambda i, n=a.ndim: (0,) * n) for a in arrays]


def _in_proj_kernel(x_ref, g_ref, *refs):
    n_w = len(refs) // 2
    w_refs, o_refs = refs[:n_w], refs[n_w:]
    nb = _norm(x_ref[...], g_ref[...]).astype(BF16)
    for w_ref, o_ref in zip(w_refs, o_refs):
        o_ref[...] = jnp.dot(nb, w_ref[...], preferred_element_type=F32).astype(o_ref.dtype)


def _in_proj(x2d, g, w_pieces, out_dtypes, tm=512):
    t, d = x2d.shape
    consts = [g.reshape(1, d)] + list(w_pieces)
    return pl.pallas_call(
        _in_proj_kernel,
        grid=(t // tm,),
        in_specs=_row_specs([x2d], tm) + _const_specs(consts),
        out_specs=[pl.BlockSpec((tm, w.shape[1]), lambda i: (i, 0)) for w in w_pieces],
        out_shape=[jax.ShapeDtypeStruct((t, w.shape[1]), dt) for w, dt in zip(w_pieces, out_dtypes)],
        compiler_params=pltpu.CompilerParams(dimension_semantics=("parallel",),
                                             vmem_limit_bytes=VMEM_LIMIT),
        name="in_proj",
    )(x2d, *consts)


def _rope_patterns(positions, n_rope, lead, width):
    inv_freq = ROPE_BASE ** (-jnp.arange(0, n_rope, 2, dtype=F32) / n_rope)
    ang = positions.astype(F32).reshape(-1, 1) * inv_freq
    cos, sin = jnp.cos(ang), jnp.sin(ang)
    t = ang.shape[0]
    tail = width - lead - n_rope
    cosp = jnp.concatenate([jnp.ones((t, lead)), cos, cos, jnp.zeros((t, tail))], axis=1)
    sinp = jnp.concatenate([jnp.zeros((t, lead)), -sin, sin, jnp.zeros((t, tail))], axis=1)
    rep = LANES // width
    return jnp.tile(cosp, (1, rep)), jnp.tile(sinp, (1, rep))


def _rotate_half(x, cosp, sinp, n_rope, lead, width):
    half = n_rope // 2
    lane = lax.broadcasted_iota(jnp.int32, x.shape, 1) & (width - 1)
    partner = jnp.where(lane < lead + half, pltpu.roll(x, LANES - half, 1), pltpu.roll(x, half, 1))
    return x * cosp + partner * sinp


MLA_QK = MLA_NOPE + MLA_ROPE


def _pad_head_cols(w, per_head):
    r = w.shape[0]
    w = w.reshape(r, MLA_HEADS, per_head)
    return jnp.pad(w, ((0, 0), (0, 0), (0, LANES - per_head))).reshape(r, MLA_HEADS * LANES)


def _mla_prep_kernel(cq_ref, ckv_ref, kr_ref, cos_ref, sin_ref, gq_ref, gkv_ref, wuq_ref, wuq_sw_ref,
                     wuk_ref, wuv_ref, qn_ref, qn_sw_ref, kn_ref, kn_sw_ref, q_ref, k_ref, v_ref):
    cqn = _norm(cq_ref[...], gq_ref[...]).astype(BF16)
    ckvn = _norm(ckv_ref[...], gkv_ref[...]).astype(BF16)
    qp = jnp.dot(cqn, wuq_ref[...], preferred_element_type=F32)
    qp_sw = jnp.dot(cqn, wuq_sw_ref[...], preferred_element_type=F32)
    kp = jnp.dot(ckvn, wuk_ref[...], preferred_element_type=F32)
    v_ref[...] = jnp.dot(ckvn, wuv_ref[...], preferred_element_type=F32).astype(v_ref.dtype)
    tm = qp.shape[0]
    kr_in = kr_ref[...]
    half = MLA_ROPE // 2
    lead, tail = jnp.zeros((tm, MLA_NOPE), F32), jnp.zeros((tm, LANES - MLA_QK), F32)
    kr = jnp.concatenate([lead, kr_in, tail], axis=1)
    kr_sw = jnp.concatenate([lead, kr_in[:, half:], kr_in[:, :half], tail], axis=1)
    cosp, sinp = cos_ref[...], sin_ref[...]

    def qk_norm_rope(xh, xh_sw, g, g_sw, scale):
        ss = jnp.sum(xh * xh, axis=-1, keepdims=True) * (1.0 / MLA_QK)
        return (xh * g * cosp + xh_sw * g_sw * sinp) * (lax.rsqrt(ss + EPS) * scale)

    for hd in range(MLA_HEADS):
        sl = slice(hd * LANES, (hd + 1) * LANES)
        q_ref[:, sl] = qk_norm_rope(qp[:, sl], qp_sw[:, sl], qn_ref[...], qn_sw_ref[...],
                                    MLA_QK ** -0.5).astype(q_ref.dtype)
        k_ref[:, sl] = qk_norm_rope(kp[:, sl] + kr, kr_sw, kn_ref[...], kn_sw_ref[...],
                                    1.0).astype(k_ref.dtype)


def _swap_rope_halves(a):
    half = MLA_ROPE // 2
    blocks = a.reshape(a.shape[:-1] + (-1, LANES))
    swapped = jnp.concatenate([blocks[..., :MLA_NOPE], blocks[..., MLA_NOPE + half:MLA_QK],
                               blocks[..., MLA_NOPE:MLA_NOPE + half], blocks[..., MLA_QK:]], axis=-1)
    return swapped.reshape(a.shape)


def _mla_prep(c_q, c_kv, k_r, cosp, sinp, q_norm_g, w_uq, kv_norm_g, w_ukv, qn_g, kn_g, tm=512):
    t = c_q.shape[0]
    w_ukv3 = w_ukv.reshape(MLA_KV_RANK, MLA_HEADS, MLA_NOPE + MLA_V)
    w_uk = _pad_head_cols(w_ukv3[:, :, :MLA_NOPE].reshape(MLA_KV_RANK, -1), MLA_NOPE).astype(BF16)
    w_uv = w_ukv3[:, :, MLA_NOPE:].reshape(MLA_KV_RANK, MLA_HEADS * MLA_V).astype(BF16)
    w_uqp = _pad_head_cols(w_uq, MLA_QK).astype(BF16)
    pad_g = lambda g: jnp.pad(g, (0, LANES - MLA_QK)).reshape(1, LANES)
    qn, kn = pad_g(qn_g), pad_g(kn_g)
    consts = [q_norm_g.reshape(1, -1), kv_norm_g.reshape(1, -1), w_uqp, _swap_rope_halves(w_uqp), w_uk, w_uv,
              qn, _swap_rope_halves(qn), kn, _swap_rope_halves(kn)]
    rows = [c_q, c_kv, k_r, cosp, sinp]
    wide = MLA_HEADS * LANES
    return pl.pallas_call(
        _mla_prep_kernel,
        grid=(t // tm,),
        in_specs=_row_specs(rows, tm) + _const_specs(consts),
        out_specs=[pl.BlockSpec((tm, wide), lambda i: (i, 0)),
                   pl.BlockSpec((tm, wide), lambda i: (i, 0)),
                   pl.BlockSpec((tm, MLA_HEADS * MLA_V), lambda i: (i, 0))],
        out_shape=[jax.ShapeDtypeStruct((t, wide), BF16), jax.ShapeDtypeStruct((t, wide), BF16),
                   jax.ShapeDtypeStruct((t, MLA_HEADS * MLA_V), BF16)],
        compiler_params=pltpu.CompilerParams(dimension_semantics=("parallel",),
                                             vmem_limit_bytes=VMEM_LIMIT),
        name="mla_prep",
    )(*rows, *consts)


def _mla_attn_kernel(qt_ref, k_ref, vt_ref, o_ref, st_a, st_b, m_ref, l_ref, acc_ref, *, tile):
    qi = pl.program_id(2)
    key = lax.broadcasted_iota(jnp.int32, (tile, tile), 0)
    qry = lax.broadcasted_iota(jnp.int32, (tile, tile), 1)
    head_lanes = [slice(a * LANES, (a + 1) * LANES) for a in range(2)]
    qts = [qt_ref[0, lanes, :] for lanes in head_lanes]

    def scores_into(buf, j):
        start = pl.multiple_of(j * tile, tile)
        for a, lanes in enumerate(head_lanes):
            buf[a] = jnp.dot(k_ref[0, pl.ds(start, tile), lanes], qts[a], preferred_element_type=F32)

    def consume(buf, j, masked):
        start = pl.multiple_of(j * tile, tile)
        vtj = vt_ref[0, :, pl.ds(start, tile)]
        for a in range(2):
            st = buf[a]
            if masked:
                st = jnp.where(key <= qry, st, -1e30)
            m = m_ref[a]
            m_new = jnp.maximum(m, jnp.max(st, axis=0, keepdims=True))
            alpha = jnp.exp(m - m_new)
            p = jnp.exp(st - m_new)
            m_ref[a] = m_new
            l_ref[a] = alpha * l_ref[a] + jnp.sum(p, axis=0, keepdims=True)
            pv = jnp.dot(vtj, p.astype(BF16), preferred_element_type=F32)
            acc_ref[a] = alpha * acc_ref[a] + pv

    m_ref[...] = jnp.full(m_ref.shape, -1e30, F32)
    l_ref[...] = jnp.zeros(l_ref.shape, F32)
    acc_ref[...] = jnp.zeros(acc_ref.shape, F32)
    scores_into(st_a, 0)

    def two_tiles(i, carry):
        j = 2 * i
        scores_into(st_b, j + 1)
        consume(st_a, j, False)
        scores_into(st_a, j + 2)
        consume(st_b, j + 1, False)
        return carry

    lax.fori_loop(0, qi // 2, two_tiles, 0)

    @pl.when(qi % 2 == 0)
    def _():
        consume(st_a, qi, True)

    @pl.when(qi % 2 == 1)
    def _():
        scores_into(st_b, qi)
        consume(st_a, qi - 1, False)
        consume(st_b, qi, True)

    dv_row = lax.broadcasted_iota(jnp.int32, acc_ref.shape[1:], 0)
    out_t = jnp.where(dv_row < MLA_V, acc_ref[0] / l_ref[0], acc_ref[1] / l_ref[1])
    o_ref[0] = out_t.T.astype(o_ref.dtype)


def _mla_attn(q, k, v, tile=512):
    b, s, _ = q.shape
    pair = 2 * LANES
    qt, vt = q.transpose(0, 2, 1), v.transpose(0, 2, 1)
    return pl.pallas_call(
        functools.partial(_mla_attn_kernel, tile=tile),
        grid=(b, MLA_HEADS // 2, s // tile),
        in_specs=[pl.BlockSpec((1, pair, tile), lambda bi, hp, qi: (bi, hp, qi)),
                  pl.BlockSpec((1, s, pair), lambda bi, hp, qi: (bi, 0, hp)),
                  pl.BlockSpec((1, 2 * MLA_V, s), lambda bi, hp, qi: (bi, hp, 0))],
        out_specs=pl.BlockSpec((1, tile, 2 * MLA_V), lambda bi, hp, qi: (bi, qi, hp)),
        out_shape=jax.ShapeDtypeStruct((b, s, MLA_HEADS * MLA_V), BF16),
        scratch_shapes=[pltpu.VMEM((2, tile, tile), F32), pltpu.VMEM((2, tile, tile), F32),
                        pltpu.VMEM((2, 1, tile), F32), pltpu.VMEM((2, 1, tile), F32),
                        pltpu.VMEM((2, 2 * MLA_V, tile), F32)],
        compiler_params=pltpu.CompilerParams(dimension_semantics=("parallel", "parallel", "arbitrary"),
                                             vmem_limit_bytes=VMEM_LIMIT),
        name="mla_attn",
    )(qt, k, vt)


def _mla_branch(c_q, c_kv, k_r, positions, q_norm_g, w_uq, kv_norm_g, w_ukv, qn_g, kn_g, b, s):
    cosp, sinp = _rope_patterns(positions, MLA_ROPE, MLA_NOPE, LANES)
    q, k, v = _mla_prep(c_q, c_kv, k_r, cosp, sinp, q_norm_g, w_uq, kv_norm_g, w_ukv, qn_g, kn_g)
    o = _mla_attn(q.reshape(b, s, -1), k.reshape(b, s, -1), v.reshape(b, s, -1))
    return o.reshape(b * s, -1)


def _retention_kernel(q_ref, k_ref, v_ref, g_ref, cos_ref, sin_ref, decay_ref, xi_ref, zeta_ref,
                      cd_ref, gn_ref, o_ref):
    L = RET_CHUNK
    n_chunks = q_ref.shape[1] // L
    lane = lax.broadcasted_iota(jnp.int32, (L, LANES), 1)

    def chunk_body(c, states):
        rows = pl.ds(pl.multiple_of(c * L, L), L)
        cosp, sinp = cos_ref[0, rows, :], sin_ref[0, rows, :]
        new_states = []
        for hp in range(RET_HEADS // 2):
            pair = slice(hp * LANES, (hp + 1) * LANES)
            qc = _rotate_half(q_ref[0, rows, pair], cosp, sinp, RET_DK, 0, RET_DK)
            kc = _rotate_half(k_ref[0, rows, pair], cosp, sinp, RET_DK, 0, RET_DK) * (RET_DK ** -0.5)
            qx = qc * xi_ref[hp]
            kz = kc * zeta_ref[hp]
            for a in range(2):
                hd = 2 * hp + a
                cols = slice(hd * RET_DV, (hd + 1) * RET_DV)
                mine = (lane >= a * RET_DK) & (lane < (a + 1) * RET_DK)
                qa = jnp.where(mine, qc, 0.0).astype(BF16)
                ka = jnp.where(mine, kc, 0.0).astype(BF16)
                va = v_ref[0, rows, cols]
                inner = lax.dot_general(qa, ka, (((1,), (1,)), ((), ())),
                                        preferred_element_type=F32) * decay_ref[hd]
                cross = jnp.dot(jnp.where(mine, qx, 0.0).astype(BF16), states[hd].astype(BF16),
                                preferred_element_type=F32)
                o = jnp.dot(inner.astype(BF16), va, preferred_element_type=F32) + cross
                kzt = jnp.where(mine, kz, 0.0).T.astype(BF16)
                incr = jnp.dot(kzt, va, preferred_element_type=F32)
                new_states.append(cd_ref[hd, 0:1, :] * states[hd] + incr)
                o = _norm(o, gn_ref[hd])
                gate = g_ref[0, rows, cols].astype(F32)
                o_ref[0, rows, cols] = (o * gate * jax.nn.sigmoid(gate)).astype(o_ref.dtype)
        return tuple(new_states)

    zero = jnp.zeros((LANES, RET_DV), F32)
    lax.fori_loop(0, n_chunks, chunk_body, (zero,) * RET_HEADS)


def _retention_branch(r_q, r_k, r_v, r_g, positions, gn_g):
    b, s, _ = r_q.shape
    L = RET_CHUNK
    cosp, sinp = _rope_patterns(positions, RET_DK, 0, RET_DK)
    cosp, sinp = cosp.reshape(b, s, LANES), sinp.reshape(b, s, LANES)
    log_gamma = jnp.log(1.0 - 2.0 ** (-5.0 - jnp.arange(RET_HEADS, dtype=F32)))
    idx = jnp.arange(L, dtype=F32)
    diff = idx[:, None] - idx[None, :]
    decay = jnp.where(diff >= 0, jnp.exp(log_gamma[:, None, None] * jnp.maximum(diff, 0.0)), 0.0)
    xi = jnp.exp(log_gamma[:, None] * (idx + 1.0))
    zeta = jnp.exp(log_gamma[:, None] * (L - 1.0 - idx))
    per_pair = lambda a: jnp.repeat(a.reshape(RET_HEADS // 2, 2, L), RET_DK, axis=1).transpose(0, 2, 1)
    cd = jnp.broadcast_to(jnp.exp(log_gamma * L)[:, None, None], (RET_HEADS, 8, RET_DV))
    gn = gn_g.reshape(RET_HEADS, 1, RET_DV)
    consts = [decay, per_pair(xi), per_pair(zeta), cd, gn]
    seqs = [r_q, r_k, r_v, r_g, cosp, sinp]
    return pl.pallas_call(
        _retention_kernel,
        grid=(b,),
        in_specs=[pl.BlockSpec((1, s, a.shape[2]), lambda bi: (bi, 0, 0)) for a in seqs]
        + _const_specs(consts),
        out_specs=pl.BlockSpec((1, s, RET_HEADS * RET_DV), lambda bi: (bi, 0, 0)),
        out_shape=jax.ShapeDtypeStruct((b, s, RET_HEADS * RET_DV), BF16),
        compiler_params=pltpu.CompilerParams(dimension_semantics=("parallel",),
                                             vmem_limit_bytes=PEER_VMEM_LIMIT),
        name="retention",
    )(*seqs, *consts)


def _mem_kv_kernel(mem_ref, g_ref, w_ref, kn_ref, k_ref, v_ref):
    kv = jnp.dot(_norm(mem_ref[0], g_ref[...]).astype(BF16), w_ref[...], preferred_element_type=F32)
    width = X_HEADS * X_DH
    for hd in range(X_HEADS):
        sl = slice(hd * X_DH, (hd + 1) * X_DH)
        k_ref[0, :, sl] = _norm(kv[:, sl], kn_ref[...]).astype(k_ref.dtype)
    v_ref[0] = kv[:, width:].astype(v_ref.dtype)


def _mem_kv(mem, mem_g, w_mem_kv, kn_g):
    b, m, d = mem.shape
    width = X_HEADS * X_DH
    w4 = w_mem_kv.reshape(d, X_HEADS, 2, X_DH)
    w = jnp.concatenate([w4[:, :, 0].reshape(d, width), w4[:, :, 1].reshape(d, width)], axis=1).astype(BF16)
    consts = [mem_g.reshape(1, d), w, kn_g.reshape(1, X_DH)]
    blk = pl.BlockSpec((1, m, width), lambda i: (i, 0, 0))
    return pl.pallas_call(
        _mem_kv_kernel,
        grid=(b,),
        in_specs=[pl.BlockSpec((1, m, d), lambda i: (i, 0, 0))] + _const_specs(consts),
        out_specs=[blk, blk],
        out_shape=[jax.ShapeDtypeStruct((b, m, width), BF16)] * 2,
        compiler_params=pltpu.CompilerParams(dimension_semantics=("parallel",),
                                             vmem_limit_bytes=VMEM_LIMIT),
        name="mem_kv",
    )(mem, *consts)


def _mem_attn_kernel(q_ref, k_ref, v_ref, qn_ref, o_ref):
    for hd in range(X_HEADS):
        sl = slice(hd * X_DH, (hd + 1) * X_DH)
        q = (_norm(q_ref[0, :, sl].astype(F32), qn_ref[...]) * (X_DH ** -0.5)).astype(BF16)
        s = lax.dot_general(q, k_ref[0, :, sl], (((1,), (1,)), ((), ())), preferred_element_type=F32)
        p = jnp.exp(s - jnp.max(s, axis=-1, keepdims=True))
        o = jnp.dot(p.astype(BF16), v_ref[0, :, sl], preferred_element_type=F32)
        o_ref[0, :, sl] = (o / jnp.sum(p, axis=-1, keepdims=True)).astype(o_ref.dtype)


def _mem_attn(x_q, k, v, qn_g, tm=512):
    b, s, width = x_q.shape
    m = k.shape[1]
    kv_spec = pl.BlockSpec((1, m, width), lambda bi, i: (bi, 0, 0))
    return pl.pallas_call(
        _mem_attn_kernel,
        grid=(b, s // tm),
        in_specs=[pl.BlockSpec((1, tm, width), lambda bi, i: (bi, i, 0)), kv_spec, kv_spec,
                  pl.BlockSpec((1, X_DH), lambda bi, i: (0, 0))],
        out_specs=pl.BlockSpec((1, tm, width), lambda bi, i: (bi, i, 0)),
        out_shape=jax.ShapeDtypeStruct((b, s, width), BF16),
        compiler_params=pltpu.CompilerParams(dimension_semantics=("parallel", "parallel"),
                                             vmem_limit_bytes=VMEM_LIMIT),
        name="mem_attn",
    )(x_q, k, v, qn_g.reshape(1, X_DH))


def _merge_kernel(x_ref, gates_ref, oa_ref, ob_ref, oc_ref, wa_ref, wb_ref, wc_ref, wout_ref, h_ref):
    merged = None
    for i, (o_ref, w_ref) in enumerate(((oa_ref, wa_ref), (ob_ref, wb_ref), (oc_ref, wc_ref))):
        y = jnp.dot(o_ref[...], w_ref[...], preferred_element_type=F32)
        gate = jax.nn.sigmoid(gates_ref[:, i * D_MODEL:(i + 1) * D_MODEL].astype(F32))
        merged = gate * y if merged is None else merged + gate * y
    h_ref[...] = x_ref[...] + jnp.dot(merged.astype(BF16), wout_ref[...], preferred_element_type=F32)


def _merge(x2d, gates, o_mla, o_ret, o_mem, w_o_mla, w_o_ret, w_o_cross, w_out, tm=512):
    t, d = x2d.shape
    rows = [x2d, gates, o_mla, o_ret, o_mem]
    consts = [w.astype(BF16) for w in (w_o_mla, w_o_ret, w_o_cross, w_out)]
    return pl.pallas_call(
        _merge_kernel,
        grid=(t // tm,),
        in_specs=_row_specs(rows, tm) + _const_specs(consts),
        out_specs=pl.BlockSpec((tm, d), lambda i: (i, 0)),
        out_shape=jax.ShapeDtypeStruct((t, d), F32),
        compiler_params=pltpu.CompilerParams(dimension_semantics=("parallel",),
                                             vmem_limit_bytes=VMEM_LIMIT),
        name="merge",
    )(*rows, *consts)


PEER_SLOTS = PEER_HEADS * PEER_TOPK
PAIR_CANDS = [(i, j) for i in range(PEER_TOPK) for j in range(PEER_TOPK)
              if (i + 1) * (j + 1) <= PEER_TOPK]
N_CANDS = len(PAIR_CANDS)
NEG_INF = float("-inf")


def _argmax_tree(nodes):
    while len(nodes) > 1:
        nxt = []
        for i in range(0, len(nodes) - 1, 2):
            a, b = nodes[i], nodes[i + 1]
            take_b = b[0] > a[0]
            nxt.append(tuple(jnp.where(take_b, y, x) for x, y in zip(a, b)))
        if len(nodes) % 2:
            nxt.append(nodes[-1])
        nodes = nxt
    return nodes[0]


def _top_rounds(rows, n_rounds):
    vals, idxs = [], []
    for _ in range(n_rounds):
        m, idx = _argmax_tree([(r, n) for n, r in enumerate(rows)])
        rows = [jnp.where(idx == n, NEG_INF, r) for n, r in enumerate(rows)]
        vals.append(m)
        idxs.append(idx)
    return vals, idxs


def _peer_select_kernel(h_ref, g_ref, wq_ref, keys_ref, xn_ref, exp_ref, gate_ref, q_ref):
    xn = _norm(h_ref[...], g_ref[...])
    xn_ref[...] = xn
    xnb = xn.astype(BF16)
    q_ref[...] = jnp.dot(xnb, wq_ref[...], preferred_element_type=F32).astype(BF16)
    tb = xnb.shape[0]
    tok = (tb // LANES, LANES)

    def head_body(hd, carry):
        sv, si = [], []
        for c in range(2):
            col = pl.multiple_of(hd * PEER_DQ + c * PEER_DHALF, PEER_DHALF)
            qh = q_ref[:, pl.ds(col, PEER_DHALF)]
            st = lax.dot_general(keys_ref[hd, c], qh, (((1,), (1,)), ((), ())),
                                 preferred_element_type=F32)
            st = st.reshape((PEER_N_KEYS,) + tok)
            v, i = _top_rounds([st[n] for n in range(PEER_N_KEYS)], PEER_TOPK)
            sv.append(v)
            si.append(i)
        cands = [sv[0][i] + sv[1][j] for i, j in PAIR_CANDS]
        cand_e = [si[0][i] * PEER_N_KEYS + si[1][j] for i, j in PAIR_CANDS]
        best, experts = [], []
        for _ in range(PEER_TOPK):
            m, pos, e = _argmax_tree([(c, p, ce) for p, (c, ce) in enumerate(zip(cands, cand_e))])
            cands = [jnp.where(pos == p, NEG_INF, c) for p, c in enumerate(cands)]
            best.append(m)
            experts.append(e)
        p = [jnp.exp(b - best[0]) for b in best]
        denom = p[0]
        for pk in p[1:]:
            denom = denom + pk
        row0 = pl.multiple_of(hd * PEER_TOPK, PEER_TOPK)
        exp_ref[pl.ds(row0, PEER_TOPK)] = jnp.stack(experts, axis=0)
        gate_ref[pl.ds(row0, PEER_TOPK)] = jnp.stack([pk / denom for pk in p], axis=0)
        return carry

    lax.fori_loop(0, PEER_HEADS, head_body, 0)


def _peer_select(h2d, g, w_q, sub_keys, tb=1024):
    t, d = h2d.shape
    consts = [g.reshape(1, d), w_q.astype(BF16), sub_keys.astype(BF16)]
    slot_spec = pl.BlockSpec((PEER_SLOTS, tb // LANES, LANES), lambda i: (0, i, 0))
    xn, experts, gates = pl.pallas_call(
        _peer_select_kernel,
        grid=(t // tb,),
        in_specs=_row_specs([h2d], tb) + _const_specs(consts),
        out_specs=[pl.BlockSpec((tb, d), lambda i: (i, 0)), slot_spec, slot_spec],
        out_shape=[
            jax.ShapeDtypeStruct((t, d), F32),
            jax.ShapeDtypeStruct((PEER_SLOTS, t // LANES, LANES), jnp.int32),
            jax.ShapeDtypeStruct((PEER_SLOTS, t // LANES, LANES), F32),
        ],
        scratch_shapes=[pltpu.VMEM((tb, PEER_HEADS * PEER_DQ), BF16)],
        compiler_params=pltpu.CompilerParams(dimension_semantics=("parallel",),
                                             vmem_limit_bytes=VMEM_LIMIT),
        name="peer_select",
    )(h2d, *consts)
    return xn, experts.reshape(PEER_SLOTS, t), gates.reshape(PEER_SLOTS, t)


ROW_TILE = D_MODEL // LANES
PAIR_ROWS = 2 * ROW_TILE


def _pack_expert_table(tab):
    n, d = tab.shape
    t = tab.astype(BF16).reshape(2, n // 2, ROW_TILE, LANES)
    bits = lax.bitcast_convert_type(t, jnp.uint16).astype(jnp.uint32)
    return (bits[0] | (bits[1] << 16)).reshape(n // 2 * ROW_TILE, LANES)


TOKENS_PER_ITER = 8


def _gather_tiles(idx_ref, tab_ref, g_ref, t):
    for k in range(PEER_SLOTS):
        start = pl.multiple_of(idx_ref[t, k], ROW_TILE)
        g_ref[k * ROW_TILE:(k + 1) * ROW_TILE, :] = tab_ref[pl.ds(start, ROW_TILE), :]


def _peer_u_kernel(idx_ref, x_ref, hi_ref, gate_ref, fold_ref, tab_ref, w_ref, sel_ref, *g_refs):
    tb = x_ref.shape[0]
    shape = (ROW_TILE, PEER_SLOTS * PAIR_ROWS)
    row_in_tile = lax.broadcasted_iota(jnp.int32, shape, 1) & (PAIR_ROWS - 1)
    chunk = lax.broadcasted_iota(jnp.int32, shape, 0)
    keep = (row_in_tile >> 1) == chunk

    def token_pair(i, carry):
        for j, g_ref in enumerate(g_refs):
            t = i * TOKENS_PER_ITER + j
            _gather_tiles(idx_ref, tab_ref, g_ref, t)
            x8 = x_ref[pl.ds(t, 1), :].reshape(ROW_TILE, LANES)
            x16 = jnp.concatenate([x8, jnp.zeros_like(x8)], axis=0).astype(BF16)
            out = lax.dot_general(x16, pltpu.bitcast(g_ref[...], BF16), (((1,), (1,)), ((), ())),
                                  preferred_element_type=F32)
            sel_ref[pl.ds(t, 1), :] = jnp.sum(jnp.where(keep, out[:ROW_TILE], 0.0), axis=0, keepdims=True)
        return carry

    lax.fori_loop(0, tb // TOKENS_PER_ITER, token_pair, 0)
    sel = sel_ref[...]
    sel_hi = sel.astype(BF16)
    sel_lo = (sel - sel_hi.astype(F32)).astype(BF16)
    r = (jnp.dot(sel_hi, fold_ref[...], preferred_element_type=F32)
         + jnp.dot(sel_lo, fold_ref[...], preferred_element_type=F32))
    a = jnp.where(hi_ref[...] == 1, r[:, PEER_SLOTS:], r[:, :PEER_SLOTS])
    w_ref[...] = gate_ref[...] * jax.nn.gelu(a)


def _peer_u(idx, xn, hi, gate, tab, tb):
    t = idx.shape[0]
    eye = jnp.eye(PEER_SLOTS, dtype=BF16)
    parity = (jnp.arange(PAIR_ROWS) & 1).astype(BF16)
    fold = jnp.concatenate([jnp.kron(eye, (1 - parity)[:, None]), jnp.kron(eye, parity[:, None])], axis=1)
    g_shape = pltpu.VMEM((PEER_SLOTS * ROW_TILE, LANES), jnp.uint32)
    return pl.pallas_call(
        _peer_u_kernel,
        grid=(t // tb,),
        in_specs=[
            pl.BlockSpec((tb, PEER_SLOTS), lambda i: (i, 0), memory_space=pltpu.SMEM),
            pl.BlockSpec((tb, D_MODEL), lambda i: (i, 0)),
            pl.BlockSpec((tb, PEER_SLOTS), lambda i: (i, 0)),
            pl.BlockSpec((tb, PEER_SLOTS), lambda i: (i, 0)),
            pl.BlockSpec(fold.shape, lambda i: (0, 0), pipeline_mode=pl.Buffered(1)),
            pl.BlockSpec(tab.shape, lambda i: (0, 0), pipeline_mode=pl.Buffered(1)),
        ],
        out_specs=pl.BlockSpec((tb, PEER_SLOTS), lambda i: (i, 0)),
        out_shape=jax.ShapeDtypeStruct((t, PEER_SLOTS), F32),
        scratch_shapes=[pltpu.VMEM((tb, PEER_SLOTS * PAIR_ROWS), F32)] + [g_shape] * TOKENS_PER_ITER,
        compiler_params=pltpu.CompilerParams(dimension_semantics=("arbitrary",),
                                             vmem_limit_bytes=PEER_VMEM_LIMIT),
        name="peer_u",
    )(idx, xn, hi, gate, fold, tab)


def _peer_v_kernel(idx_ref, hi_ref, w_ref, h_ref, route_ref, tab_ref, o_ref, wx_ref, *g_refs):
    tb = hi_ref.shape[0]
    w = w_ref[...]
    w_hi = jnp.where(hi_ref[...] == 1, w, 0.0)
    w2 = jnp.concatenate([w - w_hi, w_hi], axis=1).astype(BF16)
    wx_ref[...] = jnp.dot(w2, route_ref[...], preferred_element_type=F32)
    shape = (PAIR_ROWS, PEER_SLOTS * PAIR_ROWS)
    row_in_tile = lax.broadcasted_iota(jnp.int32, shape, 1) & (PAIR_ROWS - 1)
    chunk = lax.broadcasted_iota(jnp.int32, shape, 0)
    keep = (row_in_tile >> 1) == chunk

    def token_pair(i, carry):
        for j, g_ref in enumerate(g_refs):
            t = i * TOKENS_PER_ITER + j
            _gather_tiles(idx_ref, tab_ref, g_ref, t)
            lhs = jnp.where(keep, wx_ref[pl.ds(t, 1), :], 0.0).astype(BF16)
            o = jnp.dot(lhs, pltpu.bitcast(g_ref[...], BF16), preferred_element_type=F32)
            o_ref[pl.ds(t, 1), :] = h_ref[pl.ds(t, 1), :] + o[:ROW_TILE].reshape(1, D_MODEL)
        return carry

    lax.fori_loop(0, tb // TOKENS_PER_ITER, token_pair, 0)


def _peer_v(idx, hi, w, h2d, tab, tb):
    t = idx.shape[0]
    eye = jnp.eye(PEER_SLOTS, dtype=BF16)
    parity = (jnp.arange(PAIR_ROWS) & 1).astype(BF16)
    route = jnp.concatenate([jnp.kron(eye, (1 - parity)[None, :]), jnp.kron(eye, parity[None, :])], axis=0)
    return pl.pallas_call(
        _peer_v_kernel,
        grid=(t // tb,),
        in_specs=[
            pl.BlockSpec((tb, PEER_SLOTS), lambda i: (i, 0), memory_space=pltpu.SMEM),
            pl.BlockSpec((tb, PEER_SLOTS), lambda i: (i, 0)),
            pl.BlockSpec((tb, PEER_SLOTS), lambda i: (i, 0)),
            pl.BlockSpec((tb, D_MODEL), lambda i: (i, 0)),
            pl.BlockSpec(route.shape, lambda i: (0, 0), pipeline_mode=pl.Buffered(1)),
            pl.BlockSpec(tab.shape, lambda i: (0, 0), pipeline_mode=pl.Buffered(1)),
        ],
        out_specs=pl.BlockSpec((tb, D_MODEL), lambda i: (i, 0)),
        out_shape=jax.ShapeDtypeStruct((t, D_MODEL), F32),
        scratch_shapes=[pltpu.VMEM((tb, PEER_SLOTS * PAIR_ROWS), F32)]
        + [pltpu.VMEM((PEER_SLOTS * ROW_TILE, LANES), jnp.uint32)] * TOKENS_PER_ITER,
        compiler_params=pltpu.CompilerParams(dimension_semantics=("arbitrary",),
                                             vmem_limit_bytes=PEER_VMEM_LIMIT),
        name="peer_v",
    )(idx, hi, w, h2d, route, tab)


def _peer_experts(h2d, xn2d, experts, gate, u_tab, v_tab, tb_u=128, tb_v=128):
    t, d = xn2d.shape
    half = u_tab.shape[0] // 2
    hi = (experts >= half).astype(jnp.int32)
    idx = (experts - hi * half) * ROW_TILE
    w = _peer_u(idx, xn2d, hi, gate, _pack_expert_table(u_tab), tb_u)
    return _peer_v(idx, hi, w, h2d, _pack_expert_table(v_tab), tb_v)


def kernel(x, mem, positions, attn_norm_g, w_in, mla_q_norm_g, w_uq, mla_kv_norm_g, w_ukv,
           mla_qn_g, mla_kn_g, w_o_mla, ret_gn_g, w_o_ret, mem_norm_g, w_mem_kv, x_qn_g, x_kn_g,
           w_o_cross, w_out, ffn_norm_g, peer_w_q, peer_keys, peer_u, peer_v):
    B, S, D = x.shape
    T = B * S
    h2d = x.reshape(T, D)
    offs = [0]
    for sz in IN_SPLITS:
        offs.append(offs[-1] + sz)
    for layer in range(w_in.shape[0]):
        w_pieces = [w_in[layer][:, offs[i]:offs[i + 1]].astype(BF16) for i in range(len(IN_SPLITS))]
        out_dtypes = [F32, F32, F32, F32, F32, BF16, BF16, BF16, BF16]
        c_q, c_kv, k_r, r_q, r_k, r_v, r_g, x_q, gates = _in_proj(h2d, attn_norm_g[layer], w_pieces, out_dtypes)
        o_mla = _mla_branch(c_q, c_kv, k_r, positions, mla_q_norm_g[layer], w_uq[layer],
                            mla_kv_norm_g[layer], w_ukv[layer], mla_qn_g[layer], mla_kn_g[layer], B, S)
        seq = lambda a: a.reshape(B, S, -1)
        o_ret = _retention_branch(seq(r_q), seq(r_k), seq(r_v), seq(r_g), positions, ret_gn_g[layer])
        mk, mv = _mem_kv(mem, mem_norm_g[layer], w_mem_kv[layer], x_kn_g[layer])
        o_mem = _mem_attn(seq(x_q), mk, mv, x_qn_g[layer])
        h2d = _merge(h2d, gates, o_mla, o_ret.reshape(T, -1), o_mem.reshape(T, -1),
                     w_o_mla[layer], w_o_ret[layer], w_o_cross[layer], w_out[layer])
        xn2d, experts_t, gate_t = _peer_select(h2d, ffn_norm_g[layer], peer_w_q[layer], peer_keys[layer])
        h2d = _peer_experts(h2d, xn2d, experts_t.T, gate_t.T, peer_u[layer], peer_v[layer])
    return h2d.reshape(B, S, D)
```

```python
import functools

import jax
import jax.numpy as jnp
from jax import lax
from jax.experimental import pallas as pl
from jax.experimental.pallas import tpu as pltpu

D_MODEL = 1024
MEM_LEN = 256
MLA_HEADS = 8
MLA_Q_RANK = 384
MLA_KV_RANK = 128
MLA_NOPE = 64
MLA_ROPE = 32
MLA_V = 64
RET_HEADS = 4
RET_DK = 64
RET_DV = 128
RET_CHUNK = 128
X_HEADS = 4
X_DH = 128
N_BRANCH = 3
PEER_HEADS = 8
PEER_N_KEYS = 128
PEER_TOPK = 16
PEER_DQ = 256
PEER_DHALF = PEER_DQ // 2
ROPE_BASE = 10000.0
EPS = 1e-6
IN_SPLITS = (MLA_Q_RANK, MLA_KV_RANK, MLA_ROPE,
             RET_HEADS * RET_DK, RET_HEADS * RET_DK, RET_HEADS * RET_DV, RET_HEADS * RET_DV,
             X_HEADS * X_DH, N_BRANCH * D_MODEL)

LANES = 128
VMEM_LIMIT = 48 * 1024 * 1024
PEER_VMEM_LIMIT = 56 * 1024 * 1024
BF16 = jnp.bfloat16
F32 = jnp.float32


def _norm(x, g):
    return x * lax.rsqrt(jnp.mean(x * x, axis=-1, keepdims=True) + EPS) * g


def _row_specs(arrays, tm):
    return [pl.BlockSpec((tm, a.shape[1]), lambda i: (i, 0)) for a in arrays]


def _const_specs(arrays):
    return [pl.BlockSpec(a.shape, lambda i, n=a.ndim: (0,) * n) for a in arrays]


def _in_proj_kernel(x_ref, g_ref, *refs):
    n_w = len(refs) // 2
    w_refs, o_refs = refs[:n_w], refs[n_w:]
    nb = _norm(x_ref[...], g_ref[...]).astype(BF16)
    for w_ref, o_ref in zip(w_refs, o_refs):
        o_ref[...] = jnp.dot(nb, w_ref[...], preferred_element_type=F32).astype(o_ref.dtype)


def _in_proj(x2d, g, w_pieces, out_dtypes, tm=512):
    t, d = x2d.shape
    consts = [g.reshape(1, d)] + list(w_pieces)
    return pl.pallas_call(
        _in_proj_kernel,
        grid=(t // tm,),
        in_specs=_row_specs([x2d], tm) + _const_specs(consts),
        out_specs=[pl.BlockSpec((tm, w.shape[1]), lambda i: (i, 0)) for w in w_pieces],
        out_shape=[jax.ShapeDtypeStruct((t, w.shape[1]), dt) for w, dt in zip(w_pieces, out_dtypes)],
        compiler_params=pltpu.CompilerParams(dimension_semantics=("parallel",),
                                             vmem_limit_bytes=VMEM_LIMIT),
        name="in_proj",
    )(x2d, *consts)


def _rope_patterns(positions, n_rope, lead, width):
    inv_freq = ROPE_BASE ** (-jnp.arange(0, n_rope, 2, dtype=F32) / n_rope)
    ang = positions.astype(F32).reshape(-1, 1) * inv_freq
    cos, sin = jnp.cos(ang), jnp.sin(ang)
    t = ang.shape[0]
    tail = width - lead - n_rope
    cosp = jnp.concatenate([jnp.ones((t, lead)), cos, cos, jnp.zeros((t, tail))], axis=1)
    sinp = jnp.concatenate([jnp.zeros((t, lead)), -sin, sin, jnp.zeros((t, tail))], axis=1)
    rep = LANES // width
    return jnp.tile(cosp, (1, rep)), jnp.tile(sinp, (1, rep))


def _rotate_half(x, cosp, sinp, n_rope, lead, width):
    half = n_rope // 2
    lane = lax.broadcasted_iota(jnp.int32, x.shape, 1) & (width - 1)
    partner = jnp.where(lane < lead + half, pltpu.roll(x, LANES - half, 1), pltpu.roll(x, half, 1))
    return x * cosp + partner * sinp


MLA_QK = MLA_NOPE + MLA_ROPE


def _pad_head_cols(w, per_head):
    r = w.shape[0]
    w = w.reshape(r, MLA_HEADS, per_head)
    return jnp.pad(w, ((0, 0), (0, 0), (0, LANES - per_head))).reshape(r, MLA_HEADS * LANES)


def _mla_prep_kernel(cq_ref, ckv_ref, kr_ref, cos_ref, sin_ref, gq_ref, gkv_ref, wuq_ref, wuq_sw_ref,
                     wuk_ref, wuv_ref, qn_ref, qn_sw_ref, kn_ref, kn_sw_ref, q_ref, k_ref, v_ref):
    cqn = _norm(cq_ref[...], gq_ref[...]).astype(BF16)
    ckvn = _norm(ckv_ref[...], gkv_ref[...]).astype(BF16)
    qp = jnp.dot(cqn, wuq_ref[...], preferred_element_type=F32)
    qp_sw = jnp.dot(cqn, wuq_sw_ref[...], preferred_element_type=F32)
    kp = jnp.dot(ckvn, wuk_ref[...], preferred_element_type=F32)
    v_ref[...] = jnp.dot(ckvn, wuv_ref[...], preferred_element_type=F32).astype(v_ref.dtype)
    tm = qp.shape[0]
    kr_in = kr_ref[...]
    half = MLA_ROPE // 2
    lead, tail = jnp.zeros((tm, MLA_NOPE), F32), jnp.zeros((tm, LANES - MLA_QK), F32)
    kr = jnp.concatenate([lead, kr_in, tail], axis=1)
    kr_sw = jnp.concatenate([lead, kr_in[:, half:], kr_in[:, :half], tail], axis=1)
    cosp, sinp = cos_ref[...], sin_ref[...]

    def qk_norm_rope(xh, xh_sw, g, g_sw, scale):
        ss = jnp.sum(xh * xh, axis=-1, keepdims=True) * (1.0 / MLA_QK)
        return (xh * g * cosp + xh_sw * g_sw * sinp) * (lax.rsqrt(ss + EPS) * scale)

    for hd in range(MLA_HEADS):
        sl = slice(hd * LANES, (hd + 1) * LANES)
        q_ref[:, sl] = qk_norm_rope(qp[:, sl], qp_sw[:, sl], qn_ref[...], qn_sw_ref[...],
                                    MLA_QK ** -0.5).astype(q_ref.dtype)
        k_ref[:, sl] = qk_norm_rope(kp[:, sl] + kr, kr_sw, kn_ref[...], kn_sw_ref[...],
                                    1.0).astype(k_ref.dtype)


def _swap_rope_halves(a):
    half = MLA_ROPE // 2
    blocks = a.reshape(a.shape[:-1] + (-1, LANES))
    swapped = jnp.concatenate([blocks[..., :MLA_NOPE], blocks[..., MLA_NOPE + half:MLA_QK],
                               blocks[..., MLA_NOPE:MLA_NOPE + half], blocks[..., MLA_QK:]], axis=-1)
    return swapped.reshape(a.shape)


def _mla_prep(c_q, c_kv, k_r, cosp, sinp, q_norm_g, w_uq, kv_norm_g, w_ukv, qn_g, kn_g, tm=512):
    t = c_q.shape[0]
    w_ukv3 = w_ukv.reshape(MLA_KV_RANK, MLA_HEADS, MLA_NOPE + MLA_V)
    w_uk = _pad_head_cols(w_ukv3[:, :, :MLA_NOPE].reshape(MLA_KV_RANK, -1), MLA_NOPE).astype(BF16)
    w_uv = w_ukv3[:, :, MLA_NOPE:].reshape(MLA_KV_RANK, MLA_HEADS * MLA_V).astype(BF16)
    w_uqp = _pad_head_cols(w_uq, MLA_QK).astype(BF16)
    pad_g = lambda g: jnp.pad(g, (0, LANES - MLA_QK)).reshape(1, LANES)
    qn, kn = pad_g(qn_g), pad_g(kn_g)
    consts = [q_norm_g.reshape(1, -1), kv_norm_g.reshape(1, -1), w_uqp, _swap_rope_halves(w_uqp), w_uk, w_uv,
              qn, _swap_rope_halves(qn), kn, _swap_rope_halves(kn)]
    rows = [c_q, c_kv, k_r, cosp, sinp]
    wide = MLA_HEADS * LANES
    return pl.pallas_call(
        _mla_prep_kernel,
        grid=(t // tm,),
        in_specs=_row_specs(rows, tm) + _const_specs(consts),
        out_specs=[pl.BlockSpec((tm, wide), lambda i: (i, 0)),
                   pl.BlockSpec((tm, wide), lambda i: (i, 0)),
                   pl.BlockSpec((tm, MLA_HEADS * MLA_V), lambda i: (i, 0))],
        out_shape=[jax.ShapeDtypeStruct((t, wide), BF16), jax.ShapeDtypeStruct((t, wide), BF16),
                   jax.ShapeDtypeStruct((t, MLA_HEADS * MLA_V), BF16)],
        compiler_params=pltpu.CompilerParams(dimension_semantics=("parallel",),
                                             vmem_limit_bytes=VMEM_LIMIT),
        name="mla_prep",
    )(*rows, *consts)


def _mla_attn_kernel(qt_ref, k_ref, vt_ref, o_ref, st_a, st_b, m_ref, l_ref, acc_ref, *, tile):
    qi = pl.program_id(2)
    key = lax.broadcasted_iota(jnp.int32, (tile, tile), 0)
    qry = lax.broadcasted_iota(jnp.int32, (tile, tile), 1)
    head_lanes = [slice(a * LANES, (a + 1) * LANES) for a in range(2)]
    qts = [qt_ref[0, lanes, :] for lanes in head_lanes]

    def scores_into(buf, j):
        start = pl.multiple_of(j * tile, tile)
        for a, lanes in enumerate(head_lanes):
            buf[a] = jnp.dot(k_ref[0, pl.ds(start, tile), lanes], qts[a], preferred_element_type=F32)

    def consume(buf, j, masked):
        start = pl.multiple_of(j * tile, tile)
        vtj = vt_ref[0, :, pl.ds(start, tile)]
        for a in range(2):
            st = buf[a]
            if masked:
                st = jnp.where(key <= qry, st, -1e30)
            m = m_ref[a]
            m_new = jnp.maximum(m, jnp.max(st, axis=0, keepdims=True))
            alpha = jnp.exp(m - m_new)
            p = jnp.exp(st - m_new)
            m_ref[a] = m_new
            l_ref[a] = alpha * l_ref[a] + jnp.sum(p, axis=0, keepdims=True)
            pv = jnp.dot(vtj, p.astype(BF16), preferred_element_type=F32)
            acc_ref[a] = alpha * acc_ref[a] + pv

    m_ref[...] = jnp.full(m_ref.shape, -1e30, F32)
    l_ref[...] = jnp.zeros(l_ref.shape, F32)
    acc_ref[...] = jnp.zeros(acc_ref.shape, F32)
    scores_into(st_a, 0)

    def two_tiles(i, carry):
        j = 2 * i
        scores_into(st_b, j + 1)
        consume(st_a, j, False)
        scores_into(st_a, j + 2)
        consume(st_b, j + 1, False)
        return carry

    lax.fori_loop(0, qi // 2, two_tiles, 0)

    @pl.when(qi % 2 == 0)
    def _():
        consume(st_a, qi, True)

    @pl.when(qi % 2 == 1)
    def _():
        scores_into(st_b, qi)
        consume(st_a, qi - 1, False)
        consume(st_b, qi, True)

    dv_row = lax.broadcasted_iota(jnp.int32, acc_ref.shape[1:], 0)
    out_t = jnp.where(dv_row < MLA_V, acc_ref[0] / l_ref[0], acc_ref[1] / l_ref[1])
    o_ref[0] = out_t.T.astype(o_ref.dtype)


def _mla_attn(q, k, v, tile=512):
    b, s, _ = q.shape
    pair = 2 * LANES
    qt, vt = q.transpose(0, 2, 1), v.transpose(0, 2, 1)
    return pl.pallas_call(
        functools.partial(_mla_attn_kernel, tile=tile),
        grid=(b, MLA_HEADS // 2, s // tile),
        in_specs=[pl.BlockSpec((1, pair, tile), lambda bi, hp, qi: (bi, hp, qi)),
                  pl.BlockSpec((1, s, pair), lambda bi, hp, qi: (bi, 0, hp)),
                  pl.BlockSpec((1, 2 * MLA_V, s), lambda bi, hp, qi: (bi, hp, 0))],
        out_specs=pl.BlockSpec((1, tile, 2 * MLA_V), lambda bi, hp, qi: (bi, qi, hp)),
        out_shape=jax.ShapeDtypeStruct((b, s, MLA_HEADS * MLA_V), BF16),
        scratch_shapes=[pltpu.VMEM((2, tile, tile), F32), pltpu.VMEM((2, tile, tile), F32),
                        pltpu.VMEM((2, 1, tile), F32), pltpu.VMEM((2, 1, tile), F32),
                        pltpu.VMEM((2, 2 * MLA_V, tile), F32)],
        compiler_params=pltpu.CompilerParams(dimension_semantics=("parallel", "parallel", "arbitrary"),
                                             vmem_limit_bytes=VMEM_LIMIT),
        name="mla_attn",
    )(qt, k, vt)


def _mla_branch(c_q, c_kv, k_r, positions, q_norm_g, w_uq, kv_norm_g, w_ukv, qn_g, kn_g, b, s):
    cosp, sinp = _rope_patterns(positions, MLA_ROPE, MLA_NOPE, LANES)
    q, k, v = _mla_prep(c_q, c_kv, k_r, cosp, sinp, q_norm_g, w_uq, kv_norm_g, w_ukv, qn_g, kn_g)
    o = _mla_attn(q.reshape(b, s, -1), k.reshape(b, s, -1), v.reshape(b, s, -1))
    return o.reshape(b * s, -1)


def _retention_kernel(q_ref, k_ref, v_ref, g_ref, cos_ref, sin_ref, decay_ref, xi_ref, zeta_ref,
                      cd_ref, gn_ref, o_ref):
    L = RET_CHUNK
    n_chunks = q_ref.shape[1] // L
    lane = lax.broadcasted_iota(jnp.int32, (L, LANES), 1)

    def chunk_body(c, states):
        rows = pl.ds(pl.multiple_of(c * L, L), L)
        cosp, sinp = cos_ref[0, rows, :], sin_ref[0, rows, :]
        new_states = []
        for hp in range(RET_HEADS // 2):
            pair = slice(hp * LANES, (hp + 1) * LANES)
            qc = _rotate_half(q_ref[0, rows, pair], cosp, sinp, RET_DK, 0, RET_DK)
            kc = _rotate_half(k_ref[0, rows, pair], cosp, sinp, RET_DK, 0, RET_DK) * (RET_DK ** -0.5)
            qx = qc * xi_ref[hp]
            kz = kc * zeta_ref[hp]
            for a in range(2):
                hd = 2 * hp + a
                cols = slice(hd * RET_DV, (hd + 1) * RET_DV)
                mine = (lane >= a * RET_DK) & (lane < (a + 1) * RET_DK)
                qa = jnp.where(mine, qc, 0.0).astype(BF16)
                ka = jnp.where(mine, kc, 0.0).astype(BF16)
                va = v_ref[0, rows, cols]
                inner = lax.dot_general(qa, ka, (((1,), (1,)), ((), ())),
                                        preferred_element_type=F32) * decay_ref[hd]
                cross = jnp.dot(jnp.where(mine, qx, 0.0).astype(BF16), states[hd].astype(BF16),
                                preferred_element_type=F32)
                o = jnp.dot(inner.astype(BF16), va, preferred_element_type=F32) + cross
                kzt = jnp.where(mine, kz, 0.0).T.astype(BF16)
                incr = jnp.dot(kzt, va, preferred_element_type=F32)
                new_states.append(cd_ref[hd, 0:1, :] * states[hd] + incr)
                o = _norm(o, gn_ref[hd])
                gate = g_ref[0, rows, cols].astype(F32)
                o_ref[0, rows, cols] = (o * gate * jax.nn.sigmoid(gate)).astype(o_ref.dtype)
        return tuple(new_states)

    zero = jnp.zeros((LANES, RET_DV), F32)
    lax.fori_loop(0, n_chunks, chunk_body, (zero,) * RET_HEADS)


def _retention_branch(r_q, r_k, r_v, r_g, positions, gn_g):
    b, s, _ = r_q.shape
    L = RET_CHUNK
    cosp, sinp = _rope_patterns(positions, RET_DK, 0, RET_DK)
    cosp, sinp = cosp.reshape(b, s, LANES), sinp.reshape(b, s, LANES)
    log_gamma = jnp.log(1.0 - 2.0 ** (-5.0 - jnp.arange(RET_HEADS, dtype=F32)))
    idx = jnp.arange(L, dtype=F32)
    diff = idx[:, None] - idx[None, :]
    decay = jnp.where(diff >= 0, jnp.exp(log_gamma[:, None, None] * jnp.maximum(diff, 0.0)), 0.0)
    xi = jnp.exp(log_gamma[:, None] * (idx + 1.0))
    zeta = jnp.exp(log_gamma[:, None] * (L - 1.0 - idx))
    per_pair = lambda a: jnp.repeat(a.reshape(RET_HEADS // 2, 2, L), RET_DK, axis=1).transpose(0, 2, 1)
    cd = jnp.broadcast_to(jnp.exp(log_gamma * L)[:, None, None], (RET_HEADS, 8, RET_DV))
    gn = gn_g.reshape(RET_HEADS, 1, RET_DV)
    consts = [decay, per_pair(xi), per_pair(zeta), cd, gn]
    seqs = [r_q, r_k, r_v, r_g, cosp, sinp]
    return pl.pallas_call(
        _retention_kernel,
        grid=(b,),
        in_specs=[pl.BlockSpec((1, s, a.shape[2]), lambda bi: (bi, 0, 0)) for a in seqs]
        + _const_specs(consts),
        out_specs=pl.BlockSpec((1, s, RET_HEADS * RET_DV), lambda bi: (bi, 0, 0)),
        out_shape=jax.ShapeDtypeStruct((b, s, RET_HEADS * RET_DV), BF16),
        compiler_params=pltpu.CompilerParams(dimension_semantics=("parallel",),
                                             vmem_limit_bytes=PEER_VMEM_LIMIT),
        name="retention",
    )(*seqs, *consts)


def _mem_kv_kernel(mem_ref, g_ref, w_ref, kn_ref, k_ref, v_ref):
    kv = jnp.dot(_norm(mem_ref[0], g_ref[...]).astype(BF16), w_ref[...], preferred_element_type=F32)
    width = X_HEADS * X_DH
    for hd in range(X_HEADS):
        sl = slice(hd * X_DH, (hd + 1) * X_DH)
        k_ref[0, :, sl] = _norm(kv[:, sl], kn_ref[...]).astype(k_ref.dtype)
    v_ref[0] = kv[:, width:].astype(v_ref.dtype)


def _mem_kv(mem, mem_g, w_mem_kv, kn_g):
    b, m, d = mem.shape
    width = X_HEADS * X_DH
    w4 = w_mem_kv.reshape(d, X_HEADS, 2, X_DH)
    w = jnp.concatenate([w4[:, :, 0].reshape(d, width), w4[:, :, 1].reshape(d, width)], axis=1).astype(BF16)
    consts = [mem_g.reshape(1, d), w, kn_g.reshape(1, X_DH)]
    blk = pl.BlockSpec((1, m, width), lambda i: (i, 0, 0))
    return pl.pallas_call(
        _mem_kv_kernel,
        grid=(b,),
        in_specs=[pl.BlockSpec((1, m, d), lambda i: (i, 0, 0))] + _const_specs(consts),
        out_specs=[blk, blk],
        out_shape=[jax.ShapeDtypeStruct((b, m, width), BF16)] * 2,
        compiler_params=pltpu.CompilerParams(dimension_semantics=("parallel",),
                                             vmem_limit_bytes=VMEM_LIMIT),
        name="mem_kv",
    )(mem, *consts)


def _mem_attn_kernel(q_ref, k_ref, v_ref, qn_ref, o_ref):
    for hd in range(X_HEADS):
        sl = slice(hd * X_DH, (hd + 1) * X_DH)
        q = (_norm(q_ref[0, :, sl].astype(F32), qn_ref[...]) * (X_DH ** -0.5)).astype(BF16)
        s = lax.dot_general(q, k_ref[0, :, sl], (((1,), (1,)), ((), ())), preferred_element_type=F32)
        p = jnp.exp(s - jnp.max(s, axis=-1, keepdims=True))
        o = jnp.dot(p.astype(BF16), v_ref[0, :, sl], preferred_element_type=F32)
        o_ref[0, :, sl] = (o / jnp.sum(p, axis=-1, keepdims=True)).astype(o_ref.dtype)


def _mem_attn(x_q, k, v, qn_g, tm=512):
    b, s, width = x_q.shape
    m = k.shape[1]
    kv_spec = pl.BlockSpec((1, m, width), lambda bi, i: (bi, 0, 0))
    return pl.pallas_call(
        _mem_attn_kernel,
        grid=(b, s // tm),
        in_specs=[pl.BlockSpec((1, tm, width), lambda bi, i: (bi, i, 0)), kv_spec, kv_spec,
                  pl.BlockSpec((1, X_DH), lambda bi, i: (0, 0))],
        out_specs=pl.BlockSpec((1, tm, width), lambda bi, i: (bi, i, 0)),
        out_shape=jax.ShapeDtypeStruct((b, s, width), BF16),
        compiler_params=pltpu.CompilerParams(dimension_semantics=("parallel", "parallel"),
                                             vmem_limit_bytes=VMEM_LIMIT),
        name="mem_attn",
    )(x_q, k, v, qn_g.reshape(1, X_DH))


def _merge_kernel(x_ref, gates_ref, oa_ref, ob_ref, oc_ref, wa_ref, wb_ref, wc_ref, wout_ref, h_ref):
    merged = None
    for i, (o_ref, w_ref) in enumerate(((oa_ref, wa_ref), (ob_ref, wb_ref), (oc_ref, wc_ref))):
        y = jnp.dot(o_ref[...], w_ref[...], preferred_element_type=F32)
        gate = jax.nn.sigmoid(gates_ref[:, i * D_MODEL:(i + 1) * D_MODEL].astype(F32))
        merged = gate * y if merged is None else merged + gate * y
    h_ref[...] = x_ref[...] + jnp.dot(merged.astype(BF16), wout_ref[...], preferred_element_type=F32)


def _merge(x2d, gates, o_mla, o_ret, o_mem, w_o_mla, w_o_ret, w_o_cross, w_out, tm=512):
    t, d = x2d.shape
    rows = [x2d, gates, o_mla, o_ret, o_mem]
    consts = [w.astype(BF16) for w in (w_o_mla, w_o_ret, w_o_cross, w_out)]
    return pl.pallas_call(
        _merge_kernel,
        grid=(t // tm,),
        in_specs=_row_specs(rows, tm) + _const_specs(consts),
        out_specs=pl.BlockSpec((tm, d), lambda i: (i, 0)),
        out_shape=jax.ShapeDtypeStruct((t, d), F32),
        compiler_params=pltpu.CompilerParams(dimension_semantics=("parallel",),
                                             vmem_limit_bytes=VMEM_LIMIT),
        name="merge",
    )(*rows, *consts)


PEER_SLOTS = PEER_HEADS * PEER_TOPK
PAIR_CANDS = [(i, j) for i in range(PEER_TOPK) for j in range(PEER_TOPK)
              if (i + 1) * (j + 1) <= PEER_TOPK]
N_CANDS = len(PAIR_CANDS)
NEG_INF = float("-inf")


def _argmax_tree(nodes):
    while len(nodes) > 1:
        nxt = []
        for i in range(0, len(nodes) - 1, 2):
            a, b = nodes[i], nodes[i + 1]
            take_b = b[0] > a[0]
            nxt.append(tuple(jnp.where(take_b, y, x) for x, y in zip(a, b)))
        if len(nodes) % 2:
            nxt.append(nodes[-1])
        nodes = nxt
    return nodes[0]


def _top_rounds(rows, n_rounds):
    vals, idxs = [], []
    for _ in range(n_rounds):
        m, idx = _argmax_tree([(r, n) for n, r in enumerate(rows)])
        rows = [jnp.where(idx == n, NEG_INF, r) for n, r in enumerate(rows)]
        vals.append(m)
        idxs.append(idx)
    return vals, idxs


def _peer_select_kernel(h_ref, g_ref, wq_ref, keys_ref, xn_ref, exp_ref, gate_ref, q_ref):
    xn = _norm(h_ref[...], g_ref[...])
    xn_ref[...] = xn
    xnb = xn.astype(BF16)
    q_ref[...] = jnp.dot(xnb, wq_ref[...], preferred_element_type=F32).astype(BF16)
    tb = xnb.shape[0]
    tok = (tb // LANES, LANES)

    def head_body(hd, carry):
        sv, si = [], []
        for c in range(2):
            col = pl.multiple_of(hd * PEER_DQ + c * PEER_DHALF, PEER_DHALF)
            qh = q_ref[:, pl.ds(col, PEER_DHALF)]
            st = lax.dot_general(keys_ref[hd, c], qh, (((1,), (1,)), ((), ())),
                                 preferred_element_type=F32)
            st = st.reshape((PEER_N_KEYS,) + tok)
            v, i = _top_rounds([st[n] for n in range(PEER_N_KEYS)], PEER_TOPK)
            sv.append(v)
            si.append(i)
        cands = [sv[0][i] + sv[1][j] for i, j in PAIR_CANDS]
        cand_e = [si[0][i] * PEER_N_KEYS + si[1][j] for i, j in PAIR_CANDS]
        best, experts = [], []
        for _ in range(PEER_TOPK):
            m, pos, e = _argmax_tree([(c, p, ce) for p, (c, ce) in enumerate(zip(cands, cand_e))])
            cands = [jnp.where(pos == p, NEG_INF, c) for p, c in enumerate(cands)]
            best.append(m)
            experts.append(e)
        p = [jnp.exp(b - best[0]) for b in best]
        denom = p[0]
        for pk in p[1:]:
            denom = denom + pk
        row0 = pl.multiple_of(hd * PEER_TOPK, PEER_TOPK)
        exp_ref[pl.ds(row0, PEER_TOPK)] = jnp.stack(experts, axis=0)
        gate_ref[pl.ds(row0, PEER_TOPK)] = jnp.stack([pk / denom for pk in p], axis=0)
        return carry

    lax.fori_loop(0, PEER_HEADS, head_body, 0)


def _peer_select(h2d, g, w_q, sub_keys, tb=1024):
    t, d = h2d.shape
    consts = [g.reshape(1, d), w_q.astype(BF16), sub_keys.astype(BF16)]
    slot_spec = pl.BlockSpec((PEER_SLOTS, tb // LANES, LANES), lambda i: (0, i, 0))
    xn, experts, gates = pl.pallas_call(
        _peer_select_kernel,
        grid=(t // tb,),
        in_specs=_row_specs([h2d], tb) + _const_specs(consts),
        out_specs=[pl.BlockSpec((tb, d), lambda i: (i, 0)), slot_spec, slot_spec],
        out_shape=[
            jax.ShapeDtypeStruct((t, d), F32),
            jax.ShapeDtypeStruct((PEER_SLOTS, t // LANES, LANES), jnp.int32),
            jax.ShapeDtypeStruct((PEER_SLOTS, t // LANES, LANES), F32),
        ],
        scratch_shapes=[pltpu.VMEM((tb, PEER_HEADS * PEER_DQ), BF16)],
        compiler_params=pltpu.CompilerParams(dimension_semantics=("parallel",),
                                             vmem_limit_bytes=VMEM_LIMIT),
        name="peer_select",
    )(h2d, *consts)
    return xn, experts.reshape(PEER_SLOTS, t), gates.reshape(PEER_SLOTS, t)


ROW_TILE = D_MODEL // LANES
PAIR_ROWS = 2 * ROW_TILE


def _pack_expert_table(tab):
    n, d = tab.shape
    t = tab.astype(BF16).reshape(2, n // 2, ROW_TILE, LANES)
    bits = lax.bitcast_convert_type(t, jnp.uint16).astype(jnp.uint32)
    return (bits[0] | (bits[1] << 16)).reshape(n // 2 * ROW_TILE, LANES)


TOKENS_PER_ITER = 16


def _gather_tiles(idx_ref, tab_ref, g_ref, t):
    for k in range(PEER_SLOTS):
        start = pl.multiple_of(idx_ref[t, k], ROW_TILE)
        g_ref[k * ROW_TILE:(k + 1) * ROW_TILE, :] = tab_ref[pl.ds(start, ROW_TILE), :]


def _peer_u_kernel(idx_ref, x_ref, hi_ref, gate_ref, fold_ref, tab_ref, w_ref, sel_ref, *g_refs):
    tb = x_ref.shape[0]
    shape = (ROW_TILE, PEER_SLOTS * PAIR_ROWS)
    row_in_tile = lax.broadcasted_iota(jnp.int32, shape, 1) & (PAIR_ROWS - 1)
    chunk = lax.broadcasted_iota(jnp.int32, shape, 0)
    keep = (row_in_tile >> 1) == chunk

    def token_pair(i, carry):
        for j, g_ref in enumerate(g_refs):
            t = i * TOKENS_PER_ITER + j
            _gather_tiles(idx_ref, tab_ref, g_ref, t)
            x8 = x_ref[pl.ds(t, 1), :].reshape(ROW_TILE, LANES)
            x16 = jnp.concatenate([x8, jnp.zeros_like(x8)], axis=0).astype(BF16)
            out = lax.dot_general(x16, pltpu.bitcast(g_ref[...], BF16), (((1,), (1,)), ((), ())),
                                  preferred_element_type=F32)
            sel_ref[pl.ds(t, 1), :] = jnp.sum(jnp.where(keep, out[:ROW_TILE], 0.0), axis=0, keepdims=True)
        return carry

    lax.fori_loop(0, tb // TOKENS_PER_ITER, token_pair, 0)
    sel = sel_ref[...]
    sel_hi = sel.astype(BF16)
    sel_lo = (sel - sel_hi.astype(F32)).astype(BF16)
    r = (jnp.dot(sel_hi, fold_ref[...], preferred_element_type=F32)
         + jnp.dot(sel_lo, fold_ref[...], preferred_element_type=F32))
    a = jnp.where(hi_ref[...] == 1, r[:, PEER_SLOTS:], r[:, :PEER_SLOTS])
    w_ref[...] = gate_ref[...] * jax.nn.gelu(a)


def _peer_u(idx, xn, hi, gate, tab, tb):
    t = idx.shape[0]
    eye = jnp.eye(PEER_SLOTS, dtype=BF16)
    parity = (jnp.arange(PAIR_ROWS) & 1).astype(BF16)
    fold = jnp.concatenate([jnp.kron(eye, (1 - parity)[:, None]), jnp.kron(eye, parity[:, None])], axis=1)
    g_shape = pltpu.VMEM((PEER_SLOTS * ROW_TILE, LANES), jnp.uint32)
    return pl.pallas_call(
        _peer_u_kernel,
        grid=(t // tb,),
        in_specs=[
            pl.BlockSpec((tb, PEER_SLOTS), lambda i: (i, 0), memory_space=pltpu.SMEM),
            pl.BlockSpec((tb, D_MODEL), lambda i: (i, 0)),
            pl.BlockSpec((tb, PEER_SLOTS), lambda i: (i, 0)),
            pl.BlockSpec((tb, PEER_SLOTS), lambda i: (i, 0)),
            pl.BlockSpec(fold.shape, lambda i: (0, 0), pipeline_mode=pl.Buffered(1)),
            pl.BlockSpec(tab.shape, lambda i: (0, 0), pipeline_mode=pl.Buffered(1)),
        ],
        out_specs=pl.BlockSpec((tb, PEER_SLOTS), lambda i: (i, 0)),
        out_shape=jax.ShapeDtypeStruct((t, PEER_SLOTS), F32),
        scratch_shapes=[pltpu.VMEM((tb, PEER_SLOTS * PAIR_ROWS), F32)] + [g_shape] * TOKENS_PER_ITER,
        compiler_params=pltpu.CompilerParams(dimension_semantics=("arbitrary",),
                                             vmem_limit_bytes=PEER_VMEM_LIMIT),
        name="peer_u",
    )(idx, xn, hi, gate, fold, tab)


def _peer_v_kernel(idx_ref, hi_ref, w_ref, h_ref, route_ref, tab_ref, o_ref, wx_ref, *g_refs):
    tb = hi_ref.shape[0]
    w = w_ref[...]
    w_hi = jnp.where(hi_ref[...] == 1, w, 0.0)
    w2 = jnp.concatenate([w - w_hi, w_hi], axis=1).astype(BF16)
    wx_ref[...] = jnp.dot(w2, route_ref[...], preferred_element_type=F32)
    shape = (PAIR_ROWS, PEER_SLOTS * PAIR_ROWS)
    row_in_tile = lax.broadcasted_iota(jnp.int32, shape, 1) & (PAIR_ROWS - 1)
    chunk = lax.broadcasted_iota(jnp.int32, shape, 0)
    keep = (row_in_tile >> 1) == chunk

    def token_pair(i, carry):
        for j, g_ref in enumerate(g_refs):
            t = i * TOKENS_PER_ITER + j
            _gather_tiles(idx_ref, tab_ref, g_ref, t)
            lhs = jnp.where(keep, wx_ref[pl.ds(t, 1), :], 0.0).astype(BF16)
            o = jnp.dot(lhs, pltpu.bitcast(g_ref[...], BF16), preferred_element_type=F32)
            o_ref[pl.ds(t, 1), :] = h_ref[pl.ds(t, 1), :] + o[:ROW_TILE].reshape(1, D_MODEL)
        return carry

    lax.fori_loop(0, tb // TOKENS_PER_ITER, token_pair, 0)


def _peer_v(idx, hi, w, h2d, tab, tb):
    t = idx.shape[0]
    eye = jnp.eye(PEER_SLOTS, dtype=BF16)
    parity = (jnp.arange(PAIR_ROWS) & 1).astype(BF16)
    route = jnp.concatenate([jnp.kron(eye, (1 - parity)[None, :]), jnp.kron(eye, parity[None, :])], axis=0)
    return pl.pallas_call(
        _peer_v_kernel,
        grid=(t // tb,),
        in_specs=[
            pl.BlockSpec((tb, PEER_SLOTS), lambda i: (i, 0), memory_space=pltpu.SMEM),
            pl.BlockSpec((tb, PEER_SLOTS), lambda i: (i, 0)),
            pl.BlockSpec((tb, PEER_SLOTS), lambda i: (i, 0)),
            pl.BlockSpec((tb, D_MODEL), lambda i: (i, 0)),
            pl.BlockSpec(route.shape, lambda i: (0, 0), pipeline_mode=pl.Buffered(1)),
            pl.BlockSpec(tab.shape, lambda i: (0, 0), pipeline_mode=pl.Buffered(1)),
        ],
        out_specs=pl.BlockSpec((tb, D_MODEL), lambda i: (i, 0)),
        out_shape=jax.ShapeDtypeStruct((t, D_MODEL), F32),
        scratch_shapes=[pltpu.VMEM((tb, PEER_SLOTS * PAIR_ROWS), F32)]
        + [pltpu.VMEM((PEER_SLOTS * ROW_TILE, LANES), jnp.uint32)] * TOKENS_PER_ITER,
        compiler_params=pltpu.CompilerParams(dimension_semantics=("arbitrary",),
                                             vmem_limit_bytes=PEER_VMEM_LIMIT),
        name="peer_v",
    )(idx, hi, w, h2d, route, tab)


def _peer_experts(h2d, xn2d, experts, gate, u_tab, v_tab, tb_u=128, tb_v=128):
    t, d = xn2d.shape
    half = u_tab.shape[0] // 2
    hi = (experts >= half).astype(jnp.int32)
    idx = (experts - hi * half) * ROW_TILE
    w = _peer_u(idx, xn2d, hi, gate, _pack_expert_table(u_tab), tb_u)
    return _peer_v(idx, hi, w, h2d, _pack_expert_table(v_tab), tb_v)


def kernel(x, mem, positions, attn_norm_g, w_in, mla_q_norm_g, w_uq, mla_kv_norm_g, w_ukv,
           mla_qn_g, mla_kn_g, w_o_mla, ret_gn_g, w_o_ret, mem_norm_g, w_mem_kv, x_qn_g, x_kn_g,
           w_o_cross, w_out, ffn_norm_g, peer_w_q, peer_keys, peer_u, peer_v):
    B, S, D = x.shape
    T = B * S
    h2d = x.reshape(T, D)
    offs = [0]
    for sz in IN_SPLITS:
        offs.append(offs[-1] + sz)
    for layer in range(w_in.shape[0]):
        w_pieces = [w_in[layer][:, offs[i]:offs[i + 1]].astype(BF16) for i in range(len(IN_SPLITS))]
        out_dtypes = [F32, F32, F32, F32, F32, BF16, BF16, BF16, BF16]
        c_q, c_kv, k_r, r_q, r_k, r_v, r_g, x_q, gates = _in_proj(h2d, attn_norm_g[layer], w_pieces, out_dtypes)
        o_mla = _mla_branch(c_q, c_kv, k_r, positions, mla_q_norm_g[layer], w_uq[layer],
                            mla_kv_norm_g[layer], w_ukv[layer], mla_qn_g[layer], mla_kn_g[layer], B, S)
        seq = lambda a: a.reshape(B, S, -1)
        o_ret = _retention_branch(seq(r_q), seq(r_k), seq(r_v), seq(r_g), positions, ret_gn_g[layer])
        mk, mv = _mem_kv(mem, mem_norm_g[layer], w_mem_kv[layer], x_kn_g[layer])
        o_mem = _mem_attn(seq(x_q), mk, mv, x_qn_g[layer])
        h2d = _merge(h2d, gates, o_mla, o_ret.reshape(T, -1), o_mem.reshape(T, -1),
                     w_o_mla[layer], w_o_ret[layer], w_o_cross[layer], w_out[layer])
        xn2d, experts_t, gate_t = _peer_select(h2d, ffn_norm_g[layer], peer_w_q[layer], peer_keys[layer])
        h2d = _peer_experts(h2d, xn2d, experts_t.T, gate_t.T, peer_u[layer], peer_v[layer])
    return h2d.reshape(B, S, D)
```

```python
import functools

import jax
import jax.numpy as jnp
from jax import lax
from jax.experimental import pallas as pl
from jax.experimental.pallas import tpu as pltpu

D_MODEL = 1024
MEM_LEN = 256
MLA_HEADS = 8
MLA_Q_RANK = 384
MLA_KV_RANK = 128
MLA_NOPE = 64
MLA_ROPE = 32
MLA_V = 64
RET_HEADS = 4
RET_DK = 64
RET_DV = 128
RET_CHUNK = 128
X_HEADS = 4
X_DH = 128
N_BRANCH = 3
PEER_HEADS = 8
PEER_N_KEYS = 128
PEER_TOPK = 16
PEER_DQ = 256
PEER_DHALF = PEER_DQ // 2
ROPE_BASE = 10000.0
EPS = 1e-6
IN_SPLITS = (MLA_Q_RANK, MLA_KV_RANK, MLA_ROPE,
             RET_HEADS * RET_DK, RET_HEADS * RET_DK, RET_HEADS * RET_DV, RET_HEADS * RET_DV,
             X_HEADS * X_DH, N_BRANCH * D_MODEL)

LANES = 128
VMEM_LIMIT = 48 * 1024 * 1024
PEER_VMEM_LIMIT = 56 * 1024 * 1024
BF16 = jnp.bfloat16
F32 = jnp.float32


def _norm(x, g):
    return x * lax.rsqrt(jnp.mean(x * x, axis=-1, keepdims=True) + EPS) * g


def _row_specs(arrays, tm):
    return [pl.BlockSpec((tm, a.shape[1]), lambda i: (i, 0)) for a in arrays]


def _const_specs(arrays):
    return [pl.BlockSpec(a.shape, lambda i, n=a.ndim: (0,) * n) for a in arrays]


def _in_proj_kernel(x_ref, g_ref, *refs):
    n_w = len(refs) // 2
    w_refs, o_refs = refs[:n_w], refs[n_w:]
    nb = _norm(x_ref[...], g_ref[...]).astype(BF16)
    for w_ref, o_ref in zip(w_refs, o_refs):
        o_ref[...] = jnp.dot(nb, w_ref[...], preferred_element_type=F32).astype(o_ref.dtype)


def _in_proj(x2d, g, w_pieces, out_dtypes, tm=512):
    t, d = x2d.shape
    consts = [g.reshape(1, d)] + list(w_pieces)
    return pl.pallas_call(
        _in_proj_kernel,
        grid=(t // tm,),
        in_specs=_row_specs([x2d], tm) + _const_specs(consts),
        out_specs=[pl.BlockSpec((tm, w.shape[1]), lambda i: (i, 0)) for w in w_pieces],
        out_shape=[jax.ShapeDtypeStruct((t, w.shape[1]), dt) for w, dt in zip(w_pieces, out_dtypes)],
        compiler_params=pltpu.CompilerParams(dimension_semantics=("parallel",),
                                             vmem_limit_bytes=VMEM_LIMIT),
        name="in_proj",
    )(x2d, *consts)


def _rope_patterns(positions, n_rope, lead, width):
    inv_freq = ROPE_BASE ** (-jnp.arange(0, n_rope, 2, dtype=F32) / n_rope)
    ang = positions.astype(F32).reshape(-1, 1) * inv_freq
    cos, sin = jnp.cos(ang), jnp.sin(ang)
    t = ang.shape[0]
    tail = width - lead - n_rope
    cosp = jnp.concatenate([jnp.ones((t, lead)), cos, cos, jnp.zeros((t, tail))], axis=1)
    sinp = jnp.concatenate([jnp.zeros((t, lead)), -sin, sin, jnp.zeros((t, tail))], axis=1)
    rep = LANES // width
    return jnp.tile(cosp, (1, rep)), jnp.tile(sinp, (1, rep))


def _rotate_half(x, cosp, sinp, n_rope, lead, width):
    half = n_rope // 2
    lane = lax.broadcasted_iota(jnp.int32, x.shape, 1) & (width - 1)
    partner = jnp.where(lane < lead + half, pltpu.roll(x, LANES - half, 1), pltpu.roll(x, half, 1))
    return x * cosp + partner * sinp


MLA_QK = MLA_NOPE + MLA_ROPE


def _pad_head_cols(w, per_head):
    r = w.shape[0]
    w = w.reshape(r, MLA_HEADS, per_head)
    return jnp.pad(w, ((0, 0), (0, 0), (0, LANES - per_head))).reshape(r, MLA_HEADS * LANES)


def _mla_prep_kernel(cq_ref, ckv_ref, kr_ref, cos_ref, sin_ref, gq_ref, gkv_ref, wuq_ref, wuq_sw_ref,
                     wuk_ref, wuv_ref, qn_ref, qn_sw_ref, kn_ref, kn_sw_ref, q_ref, k_ref, v_ref):
    cqn = _norm(cq_ref[...], gq_ref[...]).astype(BF16)
    ckvn = _norm(ckv_ref[...], gkv_ref[...]).astype(BF16)
    qp = jnp.dot(cqn, wuq_ref[...], preferred_element_type=F32)
    qp_sw = jnp.dot(cqn, wuq_sw_ref[...], preferred_element_type=F32)
    kp = jnp.dot(ckvn, wuk_ref[...], preferred_element_type=F32)
    v_ref[...] = jnp.dot(ckvn, wuv_ref[...], preferred_element_type=F32).astype(v_ref.dtype)
    tm = qp.shape[0]
    kr_in = kr_ref[...]
    half = MLA_ROPE // 2
    lead, tail = jnp.zeros((tm, MLA_NOPE), F32), jnp.zeros((tm, LANES - MLA_QK), F32)
    kr = jnp.concatenate([lead, kr_in, tail], axis=1)
    kr_sw = jnp.concatenate([lead, kr_in[:, half:], kr_in[:, :half], tail], axis=1)
    cosp, sinp = cos_ref[...], sin_ref[...]

    def qk_norm_rope(xh, xh_sw, g, g_sw, scale):
        ss = jnp.sum(xh * xh, axis=-1, keepdims=True) * (1.0 / MLA_QK)
        return (xh * g * cosp + xh_sw * g_sw * sinp) * (lax.rsqrt(ss + EPS) * scale)

    for hd in range(MLA_HEADS):
        sl = slice(hd * LANES, (hd + 1) * LANES)
        q_ref[:, sl] = qk_norm_rope(qp[:, sl], qp_sw[:, sl], qn_ref[...], qn_sw_ref[...],
                                    MLA_QK ** -0.5).astype(q_ref.dtype)
        k_ref[:, sl] = qk_norm_rope(kp[:, sl] + kr, kr_sw, kn_ref[...], kn_sw_ref[...],
                                    1.0).astype(k_ref.dtype)


def _swap_rope_halves(a):
    half = MLA_ROPE // 2
    blocks = a.reshape(a.shape[:-1] + (-1, LANES))
    swapped = jnp.concatenate([blocks[..., :MLA_NOPE], blocks[..., MLA_NOPE + half:MLA_QK],
                               blocks[..., MLA_NOPE:MLA_NOPE + half], blocks[..., MLA_QK:]], axis=-1)
    return swapped.reshape(a.shape)


def _mla_prep(c_q, c_kv, k_r, cosp, sinp, q_norm_g, w_uq, kv_norm_g, w_ukv, qn_g, kn_g, tm=512):
    t = c_q.shape[0]
    w_ukv3 = w_ukv.reshape(MLA_KV_RANK, MLA_HEADS, MLA_NOPE + MLA_V)
    w_uk = _pad_head_cols(w_ukv3[:, :, :MLA_NOPE].reshape(MLA_KV_RANK, -1), MLA_NOPE).astype(BF16)
    w_uv = w_ukv3[:, :, MLA_NOPE:].reshape(MLA_KV_RANK, MLA_HEADS * MLA_V).astype(BF16)
    w_uqp = _pad_head_cols(w_uq, MLA_QK).astype(BF16)
    pad_g = lambda g: jnp.pad(g, (0, LANES - MLA_QK)).reshape(1, LANES)
    qn, kn = pad_g(qn_g), pad_g(kn_g)
    consts = [q_norm_g.reshape(1, -1), kv_norm_g.reshape(1, -1), w_uqp, _swap_rope_halves(w_uqp), w_uk, w_uv,
              qn, _swap_rope_halves(qn), kn, _swap_rope_halves(kn)]
    rows = [c_q, c_kv, k_r, cosp, sinp]
    wide = MLA_HEADS * LANES
    return pl.pallas_call(
        _mla_prep_kernel,
        grid=(t // tm,),
        in_specs=_row_specs(rows, tm) + _const_specs(consts),
        out_specs=[pl.BlockSpec((tm, wide), lambda i: (i, 0)),
                   pl.BlockSpec((tm, wide), lambda i: (i, 0)),
                   pl.BlockSpec((tm, MLA_HEADS * MLA_V), lambda i: (i, 0))],
        out_shape=[jax.ShapeDtypeStruct((t, wide), BF16), jax.ShapeDtypeStruct((t, wide), BF16),
                   jax.ShapeDtypeStruct((t, MLA_HEADS * MLA_V), BF16)],
        compiler_params=pltpu.CompilerParams(dimension_semantics=("parallel",),
                                             vmem_limit_bytes=VMEM_LIMIT),
        name="mla_prep",
    )(*rows, *consts)


def _mla_attn_kernel(qt_ref, k_ref, vt_ref, o_ref, st_a, st_b, m_ref, l_ref, acc_ref, *, tile):
    qi = pl.program_id(2)
    key = lax.broadcasted_iota(jnp.int32, (tile, tile), 0)
    qry = lax.broadcasted_iota(jnp.int32, (tile, tile), 1)
    head_lanes = [slice(a * LANES, (a + 1) * LANES) for a in range(2)]
    qts = [qt_ref[0, lanes, :] for lanes in head_lanes]

    def scores_into(buf, j):
        start = pl.multiple_of(j * tile, tile)
        for a, lanes in enumerate(head_lanes):
            buf[a] = jnp.dot(k_ref[0, pl.ds(start, tile), lanes], qts[a], preferred_element_type=F32)

    def consume(buf, j, masked):
        start = pl.multiple_of(j * tile, tile)
        vtj = vt_ref[0, :, pl.ds(start, tile)]
        for a in range(2):
            st = buf[a]
            if masked:
                st = jnp.where(key <= qry, st, -1e30)
            m = m_ref[a]
            m_new = jnp.maximum(m, jnp.max(st, axis=0, keepdims=True))
            alpha = jnp.exp(m - m_new)
            p = jnp.exp(st - m_new)
            m_ref[a] = m_new
            l_ref[a] = alpha * l_ref[a] + jnp.sum(p, axis=0, keepdims=True)
            pv = jnp.dot(vtj, p.astype(BF16), preferred_element_type=F32)
            acc_ref[a] = alpha * acc_ref[a] + pv

    m_ref[...] = jnp.full(m_ref.shape, -1e30, F32)
    l_ref[...] = jnp.zeros(l_ref.shape, F32)
    acc_ref[...] = jnp.zeros(acc_ref.shape, F32)
    scores_into(st_a, 0)

    def two_tiles(i, carry):
        j = 2 * i
        scores_into(st_b, j + 1)
        consume(st_a, j, False)
        scores_into(st_a, j + 2)
        consume(st_b, j + 1, False)
        return carry

    lax.fori_loop(0, qi // 2, two_tiles, 0)

    @pl.when(qi % 2 == 0)
    def _():
        consume(st_a, qi, True)

    @pl.when(qi % 2 == 1)
    def _():
        scores_into(st_b, qi)
        consume(st_a, qi - 1, False)
        consume(st_b, qi, True)

    dv_row = lax.broadcasted_iota(jnp.int32, acc_ref.shape[1:], 0)
    out_t = jnp.where(dv_row < MLA_V, acc_ref[0] / l_ref[0], acc_ref[1] / l_ref[1])
    o_ref[0] = out_t.T.astype(o_ref.dtype)


def _mla_attn(q, k, v, tile=512):
    b, s, _ = q.shape
    pair = 2 * LANES
    qt, vt = q.transpose(0, 2, 1), v.transpose(0, 2, 1)
    return pl.pallas_call(
        functools.partial(_mla_attn_kernel, tile=tile),
        grid=(b, MLA_HEADS // 2, s // tile),
        in_specs=[pl.BlockSpec((1, pair, tile), lambda bi, hp, qi: (bi, hp, qi)),
                  pl.BlockSpec((1, s, pair), lambda bi, hp, qi: (bi, 0, hp)),
                  pl.BlockSpec((1, 2 * MLA_V, s), lambda bi, hp, qi: (bi, hp, 0))],
        out_specs=pl.BlockSpec((1, tile, 2 * MLA_V), lambda bi, hp, qi: (bi, qi, hp)),
        out_shape=jax.ShapeDtypeStruct((b, s, MLA_HEADS * MLA_V), BF16),
        scratch_shapes=[pltpu.VMEM((2, tile, tile), F32), pltpu.VMEM((2, tile, tile), F32),
                        pltpu.VMEM((2, 1, tile), F32), pltpu.VMEM((2, 1, tile), F32),
                        pltpu.VMEM((2, 2 * MLA_V, tile), F32)],
        compiler_params=pltpu.CompilerParams(dimension_semantics=("parallel", "parallel", "arbitrary"),
                                             vmem_limit_bytes=VMEM_LIMIT),
        name="mla_attn",
    )(qt, k, vt)


def _mla_branch(c_q, c_kv, k_r, positions, q_norm_g, w_uq, kv_norm_g, w_ukv, qn_g, kn_g, b, s):
    cosp, sinp = _rope_patterns(positions, MLA_ROPE, MLA_NOPE, LANES)
    q, k, v = _mla_prep(c_q, c_kv, k_r, cosp, sinp, q_norm_g, w_uq, kv_norm_g, w_ukv, qn_g, kn_g)
    o = _mla_attn(q.reshape(b, s, -1), k.reshape(b, s, -1), v.reshape(b, s, -1))
    return o.reshape(b * s, -1)


def _retention_kernel(q_ref, k_ref, v_ref, g_ref, cos_ref, sin_ref, decay_ref, xi_ref, zeta_ref,
                      cd_ref, gn_ref, o_ref):
    L = RET_CHUNK
    n_chunks = q_ref.shape[1] // L
    lane = lax.broadcasted_iota(jnp.int32, (L, LANES), 1)

    def chunk_body(c, states):
        rows = pl.ds(pl.multiple_of(c * L, L), L)
        cosp, sinp = cos_ref[0, rows, :], sin_ref[0, rows, :]
        new_states = []
        for hp in range(RET_HEADS // 2):
            pair = slice(hp * LANES, (hp + 1) * LANES)
            qc = _rotate_half(q_ref[0, rows, pair], cosp, sinp, RET_DK, 0, RET_DK)
            kc = _rotate_half(k_ref[0, rows, pair], cosp, sinp, RET_DK, 0, RET_DK) * (RET_DK ** -0.5)
            qx = qc * xi_ref[hp]
            kz = kc * zeta_ref[hp]
            for a in range(2):
                hd = 2 * hp + a
                cols = slice(hd * RET_DV, (hd + 1) * RET_DV)
                mine = (lane >= a * RET_DK) & (lane < (a + 1) * RET_DK)
                qa = jnp.where(mine, qc, 0.0).astype(BF16)
                ka = jnp.where(mine, kc, 0.0).astype(BF16)
                va = v_ref[0, rows, cols]
                inner = lax.dot_general(qa, ka, (((1,), (1,)), ((), ())),
                                        preferred_element_type=F32) * decay_ref[hd]
                cross = jnp.dot(jnp.where(mine, qx, 0.0).astype(BF16), states[hd].astype(BF16),
                                preferred_element_type=F32)
                o = jnp.dot(inner.astype(BF16), va, preferred_element_type=F32) + cross
                kzt = jnp.where(mine, kz, 0.0).T.astype(BF16)
                incr = jnp.dot(kzt, va, preferred_element_type=F32)
                new_states.append(cd_ref[hd, 0:1, :] * states[hd] + incr)
                o = _norm(o, gn_ref[hd])
                gate = g_ref[0, rows, cols].astype(F32)
                o_ref[0, rows, cols] = (o * gate * jax.nn.sigmoid(gate)).astype(o_ref.dtype)
        return tuple(new_states)

    zero = jnp.zeros((LANES, RET_DV), F32)
    lax.fori_loop(0, n_chunks, chunk_body, (zero,) * RET_HEADS)


def _retention_branch(r_q, r_k, r_v, r_g, positions, gn_g):
    b, s, _ = r_q.shape
    L = RET_CHUNK
    cosp, sinp = _rope_patterns(positions, RET_DK, 0, RET_DK)
    cosp, sinp = cosp.reshape(b, s, LANES), sinp.reshape(b, s, LANES)
    log_gamma = jnp.log(1.0 - 2.0 ** (-5.0 - jnp.arange(RET_HEADS, dtype=F32)))
    idx = jnp.arange(L, dtype=F32)
    diff = idx[:, None] - idx[None, :]
    decay = jnp.where(diff >= 0, jnp.exp(log_gamma[:, None, None] * jnp.maximum(diff, 0.0)), 0.0)
    xi = jnp.exp(log_gamma[:, None] * (idx + 1.0))
    zeta = jnp.exp(log_gamma[:, None] * (L - 1.0 - idx))
    per_pair = lambda a: jnp.repeat(a.reshape(RET_HEADS // 2, 2, L), RET_DK, axis=1).transpose(0, 2, 1)
    cd = jnp.broadcast_to(jnp.exp(log_gamma * L)[:, None, None], (RET_HEADS, 8, RET_DV))
    gn = gn_g.reshape(RET_HEADS, 1, RET_DV)
    consts = [decay, per_pair(xi), per_pair(zeta), cd, gn]
    seqs = [r_q, r_k, r_v, r_g, cosp, sinp]
    return pl.pallas_call(
        _retention_kernel,
        grid=(b,),
        in_specs=[pl.BlockSpec((1, s, a.shape[2]), lambda bi: (bi, 0, 0)) for a in seqs]
        + _const_specs(consts),
        out_specs=pl.BlockSpec((1, s, RET_HEADS * RET_DV), lambda bi: (bi, 0, 0)),
        out_shape=jax.ShapeDtypeStruct((b, s, RET_HEADS * RET_DV), BF16),
        compiler_params=pltpu.CompilerParams(dimension_semantics=("parallel",),
                                             vmem_limit_bytes=PEER_VMEM_LIMIT),
        name="retention",
    )(*seqs, *consts)


def _mem_kv_kernel(mem_ref, g_ref, w_ref, kn_ref, k_ref, v_ref):
    kv = jnp.dot(_norm(mem_ref[0], g_ref[...]).astype(BF16), w_ref[...], preferred_element_type=F32)
    width = X_HEADS * X_DH
    for hd in range(X_HEADS):
        sl = slice(hd * X_DH, (hd + 1) * X_DH)
        k_ref[0, :, sl] = _norm(kv[:, sl], kn_ref[...]).astype(k_ref.dtype)
    v_ref[0] = kv[:, width:].astype(v_ref.dtype)


def _mem_kv(mem, mem_g, w_mem_kv, kn_g):
    b, m, d = mem.shape
    width = X_HEADS * X_DH
    w4 = w_mem_kv.reshape(d, X_HEADS, 2, X_DH)
    w = jnp.concatenate([w4[:, :, 0].reshape(d, width), w4[:, :, 1].reshape(d, width)], axis=1).astype(BF16)
    consts = [mem_g.reshape(1, d), w, kn_g.reshape(1, X_DH)]
    blk = pl.BlockSpec((1, m, width), lambda i: (i, 0, 0))
    return pl.pallas_call(
        _mem_kv_kernel,
        grid=(b,),
        in_specs=[pl.BlockSpec((1, m, d), lambda i: (i, 0, 0))] + _const_specs(consts),
        out_specs=[blk, blk],
        out_shape=[jax.ShapeDtypeStruct((b, m, width), BF16)] * 2,
        compiler_params=pltpu.CompilerParams(dimension_semantics=("parallel",),
                                             vmem_limit_bytes=VMEM_LIMIT),
        name="mem_kv",
    )(mem, *consts)


def _mem_attn_kernel(q_ref, k_ref, v_ref, qn_ref, o_ref):
    for hd in range(X_HEADS):
        sl = slice(hd * X_DH, (hd + 1) * X_DH)
        q = (_norm(q_ref[0, :, sl].astype(F32), qn_ref[...]) * (X_DH ** -0.5)).astype(BF16)
        s = lax.dot_general(q, k_ref[0, :, sl], (((1,), (1,)), ((), ())), preferred_element_type=F32)
        p = jnp.exp(s - jnp.max(s, axis=-1, keepdims=True))
        o = jnp.dot(p.astype(BF16), v_ref[0, :, sl], preferred_element_type=F32)
        o_ref[0, :, sl] = (o / jnp.sum(p, axis=-1, keepdims=True)).astype(o_ref.dtype)


def _mem_attn(x_q, k, v, qn_g, tm=512):
    b, s, width = x_q.shape
    m = k.shape[1]
    kv_spec = pl.BlockSpec((1, m, width), lambda bi, i: (bi, 0, 0))
    return pl.pallas_call(
        _mem_attn_kernel,
        grid=(b, s // tm),
        in_specs=[pl.BlockSpec((1, tm, width), lambda bi, i: (bi, i, 0)), kv_spec, kv_spec,
                  pl.BlockSpec((1, X_DH), lambda bi, i: (0, 0))],
        out_specs=pl.BlockSpec((1, tm, width), lambda bi, i: (bi, i, 0)),
        out_shape=jax.ShapeDtypeStruct((b, s, width), BF16),
        compiler_params=pltpu.CompilerParams(dimension_semantics=("parallel", "parallel"),
                                             vmem_limit_bytes=VMEM_LIMIT),
        name="mem_attn",
    )(x_q, k, v, qn_g.reshape(1, X_DH))


def _merge_kernel(x_ref, gates_ref, oa_ref, ob_ref, oc_ref, wa_ref, wb_ref, wc_ref, wout_ref, h_ref):
    merged = None
    for i, (o_ref, w_ref) in enumerate(((oa_ref, wa_ref), (ob_ref, wb_ref), (oc_ref, wc_ref))):
        y = jnp.dot(o_ref[...], w_ref[...], preferred_element_type=F32)
        gate = jax.nn.sigmoid(gates_ref[:, i * D_MODEL:(i + 1) * D_MODEL].astype(F32))
        merged = gate * y if merged is None else merged + gate * y
    h_ref[...] = x_ref[...] + jnp.dot(merged.astype(BF16), wout_ref[...], preferred_element_type=F32)


def _merge(x2d, gates, o_mla, o_ret, o_mem, w_o_mla, w_o_ret, w_o_cross, w_out, tm=512):
    t, d = x2d.shape
    rows = [x2d, gates, o_mla, o_ret, o_mem]
    consts = [w.astype(BF16) for w in (w_o_mla, w_o_ret, w_o_cross, w_out)]
    return pl.pallas_call(
        _merge_kernel,
        grid=(t // tm,),
        in_specs=_row_specs(rows, tm) + _const_specs(consts),
        out_specs=pl.BlockSpec((tm, d), lambda i: (i, 0)),
        out_shape=jax.ShapeDtypeStruct((t, d), F32),
        compiler_params=pltpu.CompilerParams(dimension_semantics=("parallel",),
                                             vmem_limit_bytes=VMEM_LIMIT),
        name="merge",
    )(*rows, *consts)


PEER_SLOTS = PEER_HEADS * PEER_TOPK
PAIR_CANDS = [(i, j) for i in range(PEER_TOPK) for j in range(PEER_TOPK)
              if (i + 1) * (j + 1) <= PEER_TOPK]
N_CANDS = len(PAIR_CANDS)
NEG_INF = float("-inf")


def _argmax_tree(nodes):
    while len(nodes) > 1:
        nxt = []
        for i in range(0, len(nodes) - 1, 2):
            a, b = nodes[i], nodes[i + 1]
            take_b = b[0] > a[0]
            nxt.append(tuple(jnp.where(take_b, y, x) for x, y in zip(a, b)))
        if len(nodes) % 2:
            nxt.append(nodes[-1])
        nodes = nxt
    return nodes[0]


def _top_rounds(rows, n_rounds):
    vals, idxs = [], []
    for _ in range(n_rounds):
        m, idx = _argmax_tree([(r, n) for n, r in enumerate(rows)])
        rows = [jnp.where(idx == n, NEG_INF, r) for n, r in enumerate(rows)]
        vals.append(m)
        idxs.append(idx)
    return vals, idxs


def _peer_select_kernel(h_ref, g_ref, wq_ref, keys_ref, xn_ref, exp_ref, gate_ref, q_ref):
    xn = _norm(h_ref[...], g_ref[...])
    xn_ref[...] = xn
    xnb = xn.astype(BF16)
    q_ref[...] = jnp.dot(xnb, wq_ref[...], preferred_element_type=F32).astype(BF16)
    tb = xnb.shape[0]
    tok = (tb // LANES, LANES)

    def head_body(hd, carry):
        sv, si = [], []
        for c in range(2):
            col = pl.multiple_of(hd * PEER_DQ + c * PEER_DHALF, PEER_DHALF)
            qh = q_ref[:, pl.ds(col, PEER_DHALF)]
            st = lax.dot_general(keys_ref[hd, c], qh, (((1,), (1,)), ((), ())),
                                 preferred_element_type=F32)
            st = st.reshape((PEER_N_KEYS,) + tok)
            v, i = _top_rounds([st[n] for n in range(PEER_N_KEYS)], PEER_TOPK)
            sv.append(v)
            si.append(i)
        cands = [sv[0][i] + sv[1][j] for i, j in PAIR_CANDS]
        cand_e = [si[0][i] * PEER_N_KEYS + si[1][j] for i, j in PAIR_CANDS]
        best, experts = [], []
        for _ in range(PEER_TOPK):
            m, pos, e = _argmax_tree([(c, p, ce) for p, (c, ce) in enumerate(zip(cands, cand_e))])
            cands = [jnp.where(pos == p, NEG_INF, c) for p, c in enumerate(cands)]
            best.append(m)
            experts.append(e)
        p = [jnp.exp(b - best[0]) for b in best]
        denom = p[0]
        for pk in p[1:]:
            denom = denom + pk
        row0 = pl.multiple_of(hd * PEER_TOPK, PEER_TOPK)
        exp_ref[pl.ds(row0, PEER_TOPK)] = jnp.stack(experts, axis=0)
        gate_ref[pl.ds(row0, PEER_TOPK)] = jnp.stack([pk / denom for pk in p], axis=0)
        return carry

    lax.fori_loop(0, PEER_HEADS, head_body, 0)


def _peer_select(h2d, g, w_q, sub_keys, tb=1024):
    t, d = h2d.shape
    consts = [g.reshape(1, d), w_q.astype(BF16), sub_keys.astype(BF16)]
    slot_spec = pl.BlockSpec((PEER_SLOTS, tb // LANES, LANES), lambda i: (0, i, 0))
    xn, experts, gates = pl.pallas_call(
        _peer_select_kernel,
        grid=(t // tb,),
        in_specs=_row_specs([h2d], tb) + _const_specs(consts),
        out_specs=[pl.BlockSpec((tb, d), lambda i: (i, 0)), slot_spec, slot_spec],
        out_shape=[
            jax.ShapeDtypeStruct((t, d), F32),
            jax.ShapeDtypeStruct((PEER_SLOTS, t // LANES, LANES), jnp.int32),
            jax.ShapeDtypeStruct((PEER_SLOTS, t // LANES, LANES), F32),
        ],
        scratch_shapes=[pltpu.VMEM((tb, PEER_HEADS * PEER_DQ), BF16)],
        compiler_params=pltpu.CompilerParams(dimension_semantics=("parallel",),
                                             vmem_limit_bytes=VMEM_LIMIT),
        name="peer_select",
    )(h2d, *consts)
    return xn, experts.reshape(PEER_SLOTS, t), gates.reshape(PEER_SLOTS, t)


ROW_TILE = D_MODEL // LANES
PAIR_ROWS = 2 * ROW_TILE


def _pack_expert_table(tab):
    n, d = tab.shape
    t = tab.astype(BF16).reshape(2, n // 2, ROW_TILE, LANES)
    bits = lax.bitcast_convert_type(t, jnp.uint16).astype(jnp.uint32)
    return (bits[0] | (bits[1] << 16)).reshape(n // 2 * ROW_TILE, LANES)


TOKENS_PER_ITER = 32


def _gather_tiles(idx_ref, tab_ref, g_ref, t):
    for k in range(PEER_SLOTS):
        start = pl.multiple_of(idx_ref[t, k], ROW_TILE)
        g_ref[k * ROW_TILE:(k + 1) * ROW_TILE, :] = tab_ref[pl.ds(start, ROW_TILE), :]


HALF_TILE = ROW_TILE // 2


def _pack_expert_halves(tab):
    n, d = tab.shape
    bits = lax.bitcast_convert_type(tab.astype(BF16).reshape(n, 2, HALF_TILE, LANES), jnp.uint16)
    bits = bits.astype(jnp.uint32)
    return (bits[:, 0] | (bits[:, 1] << 16)).reshape(n * HALF_TILE, LANES)


def _peer_u_kernel(idx_ref, x_ref, gate_ref, fold_ref, tab_ref, w_ref, sel_ref, *g_refs):
    tb = x_ref.shape[0]
    shape = (ROW_TILE, PEER_SLOTS * ROW_TILE)
    r = lax.broadcasted_iota(jnp.int32, shape, 1) & (ROW_TILE - 1)
    chunk = lax.broadcasted_iota(jnp.int32, shape, 0)
    keep = ((r >> 1) + HALF_TILE * (r & 1)) == chunk

    def token_group(i, carry):
        for j, g_ref in enumerate(g_refs):
            t = i * TOKENS_PER_ITER + j
            for k in range(PEER_SLOTS):
                start = pl.multiple_of(idx_ref[t, k], HALF_TILE)
                g_ref[k * HALF_TILE:(k + 1) * HALF_TILE, :] = tab_ref[pl.ds(start, HALF_TILE), :]
            x8 = x_ref[pl.ds(t, 1), :].reshape(ROW_TILE, LANES)
            x16 = jnp.concatenate([x8, jnp.zeros_like(x8)], axis=0).astype(BF16)
            out = lax.dot_general(x16, pltpu.bitcast(g_ref[...], BF16), (((1,), (1,)), ((), ())),
                                  preferred_element_type=F32)
            sel_ref[pl.ds(t, 1), :] = jnp.sum(jnp.where(keep, out[:ROW_TILE], 0.0), axis=0, keepdims=True)
        return carry

    lax.fori_loop(0, tb // TOKENS_PER_ITER, token_group, 0)
    sel = sel_ref[...]
    sel_hi = sel.astype(BF16)
    sel_lo = (sel - sel_hi.astype(F32)).astype(BF16)
    a = (jnp.dot(sel_hi, fold_ref[...], preferred_element_type=F32)
         + jnp.dot(sel_lo, fold_ref[...], preferred_element_type=F32))
    w_ref[...] = gate_ref[...] * jax.nn.gelu(a)


def _peer_u(idx, xn, gate, tab, tb):
    t = idx.shape[0]
    fold = jnp.repeat(jnp.eye(PEER_SLOTS, dtype=BF16), ROW_TILE, axis=0)
    g_shape = pltpu.VMEM((PEER_SLOTS * HALF_TILE, LANES), jnp.uint32)
    return pl.pallas_call(
        _peer_u_kernel,
        grid=(t // tb,),
        in_specs=[
            pl.BlockSpec((tb, PEER_SLOTS), lambda i: (i, 0), memory_space=pltpu.SMEM),
            pl.BlockSpec((tb, D_MODEL), lambda i: (i, 0)),
            pl.BlockSpec((tb, PEER_SLOTS), lambda i: (i, 0)),
            pl.BlockSpec(fold.shape, lambda i: (0, 0), pipeline_mode=pl.Buffered(1)),
            pl.BlockSpec(tab.shape, lambda i: (0, 0), pipeline_mode=pl.Buffered(1)),
        ],
        out_specs=pl.BlockSpec((tb, PEER_SLOTS), lambda i: (i, 0)),
        out_shape=jax.ShapeDtypeStruct((t, PEER_SLOTS), F32),
        scratch_shapes=[pltpu.VMEM((tb, PEER_SLOTS * ROW_TILE), F32)] + [g_shape] * TOKENS_PER_ITER,
        compiler_params=pltpu.CompilerParams(dimension_semantics=("arbitrary",),
                                             vmem_limit_bytes=PEER_VMEM_LIMIT),
        name="peer_u",
    )(idx, xn, gate, fold, tab)


def _peer_v_kernel(idx_ref, hi_ref, w_ref, h_ref, route_ref, tab_ref, o_ref, wx_ref, *g_refs):
    tb = hi_ref.shape[0]
    w = w_ref[...]
    w_hi = jnp.where(hi_ref[...] == 1, w, 0.0)
    w2 = jnp.concatenate([w - w_hi, w_hi], axis=1).astype(BF16)
    wx_ref[...] = jnp.dot(w2, route_ref[...], preferred_element_type=F32)
    shape = (PAIR_ROWS, PEER_SLOTS * PAIR_ROWS)
    row_in_tile = lax.broadcasted_iota(jnp.int32, shape, 1) & (PAIR_ROWS - 1)
    chunk = lax.broadcasted_iota(jnp.int32, shape, 0)
    keep = (row_in_tile >> 1) == chunk

    def token_pair(i, carry):
        for j, g_ref in enumerate(g_refs):
            t = i * TOKENS_PER_ITER + j
            _gather_tiles(idx_ref, tab_ref, g_ref, t)
            lhs = jnp.where(keep, wx_ref[pl.ds(t, 1), :], 0.0).astype(BF16)
            o = jnp.dot(lhs, pltpu.bitcast(g_ref[...], BF16), preferred_element_type=F32)
            o_ref[pl.ds(t, 1), :] = h_ref[pl.ds(t, 1), :] + o[:ROW_TILE].reshape(1, D_MODEL)
        return carry

    lax.fori_loop(0, tb // TOKENS_PER_ITER, token_pair, 0)


def _peer_v(idx, hi, w, h2d, tab, tb):
    t = idx.shape[0]
    eye = jnp.eye(PEER_SLOTS, dtype=BF16)
    parity = (jnp.arange(PAIR_ROWS) & 1).astype(BF16)
    route = jnp.concatenate([jnp.kron(eye, (1 - parity)[None, :]), jnp.kron(eye, parity[None, :])], axis=0)
    return pl.pallas_call(
        _peer_v_kernel,
        grid=(t // tb,),
        in_specs=[
            pl.BlockSpec((tb, PEER_SLOTS), lambda i: (i, 0), memory_space=pltpu.SMEM),
            pl.BlockSpec((tb, PEER_SLOTS), lambda i: (i, 0)),
            pl.BlockSpec((tb, PEER_SLOTS), lambda i: (i, 0)),
            pl.BlockSpec((tb, D_MODEL), lambda i: (i, 0)),
            pl.BlockSpec(route.shape, lambda i: (0, 0), pipeline_mode=pl.Buffered(1)),
            pl.BlockSpec(tab.shape, lambda i: (0, 0), pipeline_mode=pl.Buffered(1)),
        ],
        out_specs=pl.BlockSpec((tb, D_MODEL), lambda i: (i, 0)),
        out_shape=jax.ShapeDtypeStruct((t, D_MODEL), F32),
        scratch_shapes=[pltpu.VMEM((tb, PEER_SLOTS * PAIR_ROWS), F32)]
        + [pltpu.VMEM((PEER_SLOTS * ROW_TILE, LANES), jnp.uint32)] * TOKENS_PER_ITER,
        compiler_params=pltpu.CompilerParams(dimension_semantics=("arbitrary",),
                                             vmem_limit_bytes=PEER_VMEM_LIMIT),
        name="peer_v",
    )(idx, hi, w, h2d, route, tab)


def _peer_experts(h2d, xn2d, experts, gate, u_tab, v_tab, tb_u=128, tb_v=128):
    t, d = xn2d.shape
    half = u_tab.shape[0] // 2
    hi = (experts >= half).astype(jnp.int32)
    idx = (experts - hi * half) * ROW_TILE
    w = _peer_u(experts * HALF_TILE, xn2d, gate, _pack_expert_halves(u_tab), tb_u)
    return _peer_v(idx, hi, w, h2d, _pack_expert_table(v_tab), tb_v)


def kernel(x, mem, positions, attn_norm_g, w_in, mla_q_norm_g, w_uq, mla_kv_norm_g, w_ukv,
           mla_qn_g, mla_kn_g, w_o_mla, ret_gn_g, w_o_ret, mem_norm_g, w_mem_kv, x_qn_g, x_kn_g,
           w_o_cross, w_out, ffn_norm_g, peer_w_q, peer_keys, peer_u, peer_v):
    B, S, D = x.shape
    T = B * S
    h2d = x.reshape(T, D)
    offs = [0]
    for sz in IN_SPLITS:
        offs.append(offs[-1] + sz)
    for layer in range(w_in.shape[0]):
        w_pieces = [w_in[layer][:, offs[i]:offs[i + 1]].astype(BF16) for i in range(len(IN_SPLITS))]
        out_dtypes = [F32, F32, F32, F32, F32, BF16, BF16, BF16, BF16]
        c_q, c_kv, k_r, r_q, r_k, r_v, r_g, x_q, gates = _in_proj(h2d, attn_norm_g[layer], w_pieces, out_dtypes)
        o_mla = _mla_branch(c_q, c_kv, k_r, positions, mla_q_norm_g[layer], w_uq[layer],
                            mla_kv_norm_g[layer], w_ukv[layer], mla_qn_g[layer], mla_kn_g[layer], B, S)
        seq = lambda a: a.reshape(B, S, -1)
        o_ret = _retention_branch(seq(r_q), seq(r_k), seq(r_v), seq(r_g), positions, ret_gn_g[layer])
        mk, mv = _mem_kv(mem, mem_norm_g[layer], w_mem_kv[layer], x_kn_g[layer])
        o_mem = _mem_attn(seq(x_q), mk, mv, x_qn_g[layer])
        h2d = _merge(h2d, gates, o_mla, o_ret.reshape(T, -1), o_mem.reshape(T, -1),
                     w_o_mla[layer], w_o_ret[layer], w_o_cross[layer], w_out[layer])
        xn2d, experts_t, gate_t = _peer_select(h2d, ffn_norm_g[layer], peer_w_q[layer], peer_keys[layer])
        h2d = _peer_experts(h2d, xn2d, experts_t.T, gate_t.T, peer_u[layer], peer_v[layer])
    return h2d.reshape(B, S, D)
```

```python
import functools

import jax
import jax.numpy as jnp
from jax import lax
from jax.experimental import pallas as pl
from jax.experimental.pallas import tpu as pltpu

D_MODEL = 1024
MEM_LEN = 256
MLA_HEADS = 8
MLA_Q_RANK = 384
MLA_KV_RANK = 128
MLA_NOPE = 64
MLA_ROPE = 32
MLA_V = 64
RET_HEADS = 4
RET_DK = 64
RET_DV = 128
RET_CHUNK = 128
X_HEADS = 4
X_DH = 128
N_BRANCH = 3
PEER_HEADS = 8
PEER_N_KEYS = 128
PEER_TOPK = 16
PEER_DQ = 256
PEER_DHALF = PEER_DQ // 2
ROPE_BASE = 10000.0
EPS = 1e-6
IN_SPLITS = (MLA_Q_RANK, MLA_KV_RANK, MLA_ROPE,
             RET_HEADS * RET_DK, RET_HEADS * RET_DK, RET_HEADS * RET_DV, RET_HEADS * RET_DV,
             X_HEADS * X_DH, N_BRANCH * D_MODEL)

LANES = 128
VMEM_LIMIT = 48 * 1024 * 1024
PEER_VMEM_LIMIT = 56 * 1024 * 1024
BF16 = jnp.bfloat16
F32 = jnp.float32


def _norm(x, g):
    return x * lax.rsqrt(jnp.mean(x * x, axis=-1, keepdims=True) + EPS) * g


def _row_specs(arrays, tm):
    return [pl.BlockSpec((tm, a.shape[1]), lambda i: (i, 0)) for a in arrays]


def _const_specs(arrays):
    return [pl.BlockSpec(a.shape, lambda i, n=a.ndim: (0,) * n) for a in arrays]


def _in_proj_kernel(x_ref, g_ref, *refs):
    n_w = len(refs) // 2
    w_refs, o_refs = refs[:n_w], refs[n_w:]
    nb = _norm(x_ref[...], g_ref[...]).astype(BF16)
    for w_ref, o_ref in zip(w_refs, o_refs):
        o_ref[...] = jnp.dot(nb, w_ref[...], preferred_element_type=F32).astype(o_ref.dtype)


def _in_proj(x2d, g, w_pieces, out_dtypes, tm=512):
    t, d = x2d.shape
    consts = [g.reshape(1, d)] + list(w_pieces)
    return pl.pallas_call(
        _in_proj_kernel,
        grid=(t // tm,),
        in_specs=_row_specs([x2d], tm) + _const_specs(consts),
        out_specs=[pl.BlockSpec((tm, w.shape[1]), lambda i: (i, 0)) for w in w_pieces],
        out_shape=[jax.ShapeDtypeStruct((t, w.shape[1]), dt) for w, dt in zip(w_pieces, out_dtypes)],
        compiler_params=pltpu.CompilerParams(dimension_semantics=("parallel",),
                                             vmem_limit_bytes=VMEM_LIMIT),
        name="in_proj",
    )(x2d, *consts)


def _rope_patterns(positions, n_rope, lead, width):
    inv_freq = ROPE_BASE ** (-jnp.arange(0, n_rope, 2, dtype=F32) / n_rope)
    ang = positions.astype(F32).reshape(-1, 1) * inv_freq
    cos, sin = jnp.cos(ang), jnp.sin(ang)
    t = ang.shape[0]
    tail = width - lead - n_rope
    cosp = jnp.concatenate([jnp.ones((t, lead)), cos, cos, jnp.zeros((t, tail))], axis=1)
    sinp = jnp.concatenate([jnp.zeros((t, lead)), -sin, sin, jnp.zeros((t, tail))], axis=1)
    rep = LANES // width
    return jnp.tile(cosp, (1, rep)), jnp.tile(sinp, (1, rep))


def _rotate_half(x, cosp, sinp, n_rope, lead, width):
    half = n_rope // 2
    lane = lax.broadcasted_iota(jnp.int32, x.shape, 1) & (width - 1)
    partner = jnp.where(lane < lead + half, pltpu.roll(x, LANES - half, 1), pltpu.roll(x, half, 1))
    return x * cosp + partner * sinp


MLA_QK = MLA_NOPE + MLA_ROPE


def _pad_head_cols(w, per_head):
    r = w.shape[0]
    w = w.reshape(r, MLA_HEADS, per_head)
    return jnp.pad(w, ((0, 0), (0, 0), (0, LANES - per_head))).reshape(r, MLA_HEADS * LANES)


def _mla_prep_kernel(cq_ref, ckv_ref, kr_ref, cos_ref, sin_ref, gq_ref, gkv_ref, wuq_ref, wuq_sw_ref,
                     wuk_ref, wuv_ref, qn_ref, qn_sw_ref, kn_ref, kn_sw_ref, q_ref, k_ref, v_ref):
    cqn = _norm(cq_ref[...], gq_ref[...]).astype(BF16)
    ckvn = _norm(ckv_ref[...], gkv_ref[...]).astype(BF16)
    qp = jnp.dot(cqn, wuq_ref[...], preferred_element_type=F32)
    qp_sw = jnp.dot(cqn, wuq_sw_ref[...], preferred_element_type=F32)
    kp = jnp.dot(ckvn, wuk_ref[...], preferred_element_type=F32)
    v_ref[...] = jnp.dot(ckvn, wuv_ref[...], preferred_element_type=F32).astype(v_ref.dtype)
    tm = qp.shape[0]
    kr_in = kr_ref[...]
    half = MLA_ROPE // 2
    lead, tail = jnp.zeros((tm, MLA_NOPE), F32), jnp.zeros((tm, LANES - MLA_QK), F32)
    kr = jnp.concatenate([lead, kr_in, tail], axis=1)
    kr_sw = jnp.concatenate([lead, kr_in[:, half:], kr_in[:, :half], tail], axis=1)
    cosp, sinp = cos_ref[...], sin_ref[...]

    def qk_norm_rope(xh, xh_sw, g, g_sw, scale):
        ss = jnp.sum(xh * xh, axis=-1, keepdims=True) * (1.0 / MLA_QK)
        return (xh * g * cosp + xh_sw * g_sw * sinp) * (lax.rsqrt(ss + EPS) * scale)

    for hd in range(MLA_HEADS):
        sl = slice(hd * LANES, (hd + 1) * LANES)
        q_ref[:, sl] = qk_norm_rope(qp[:, sl], qp_sw[:, sl], qn_ref[...], qn_sw_ref[...],
                                    MLA_QK ** -0.5).astype(q_ref.dtype)
        k_ref[:, sl] = qk_norm_rope(kp[:, sl] + kr, kr_sw, kn_ref[...], kn_sw_ref[...],
                                    1.0).astype(k_ref.dtype)


def _swap_rope_halves(a):
    half = MLA_ROPE // 2
    blocks = a.reshape(a.shape[:-1] + (-1, LANES))
    swapped = jnp.concatenate([blocks[..., :MLA_NOPE], blocks[..., MLA_NOPE + half:MLA_QK],
                               blocks[..., MLA_NOPE:MLA_NOPE + half], blocks[..., MLA_QK:]], axis=-1)
    return swapped.reshape(a.shape)


def _mla_prep(c_q, c_kv, k_r, cosp, sinp, q_norm_g, w_uq, kv_norm_g, w_ukv, qn_g, kn_g, tm=512):
    t = c_q.shape[0]
    w_ukv3 = w_ukv.reshape(MLA_KV_RANK, MLA_HEADS, MLA_NOPE + MLA_V)
    w_uk = _pad_head_cols(w_ukv3[:, :, :MLA_NOPE].reshape(MLA_KV_RANK, -1), MLA_NOPE).astype(BF16)
    w_uv = w_ukv3[:, :, MLA_NOPE:].reshape(MLA_KV_RANK, MLA_HEADS * MLA_V).astype(BF16)
    w_uqp = _pad_head_cols(w_uq, MLA_QK).astype(BF16)
    pad_g = lambda g: jnp.pad(g, (0, LANES - MLA_QK)).reshape(1, LANES)
    qn, kn = pad_g(qn_g), pad_g(kn_g)
    consts = [q_norm_g.reshape(1, -1), kv_norm_g.reshape(1, -1), w_uqp, _swap_rope_halves(w_uqp), w_uk, w_uv,
              qn, _swap_rope_halves(qn), kn, _swap_rope_halves(kn)]
    rows = [c_q, c_kv, k_r, cosp, sinp]
    wide = MLA_HEADS * LANES
    return pl.pallas_call(
        _mla_prep_kernel,
        grid=(t // tm,),
        in_specs=_row_specs(rows, tm) + _const_specs(consts),
        out_specs=[pl.BlockSpec((tm, wide), lambda i: (i, 0)),
                   pl.BlockSpec((tm, wide), lambda i: (i, 0)),
                   pl.BlockSpec((tm, MLA_HEADS * MLA_V), lambda i: (i, 0))],
        out_shape=[jax.ShapeDtypeStruct((t, wide), BF16), jax.ShapeDtypeStruct((t, wide), BF16),
                   jax.ShapeDtypeStruct((t, MLA_HEADS * MLA_V), BF16)],
        compiler_params=pltpu.CompilerParams(dimension_semantics=("parallel",),
                                             vmem_limit_bytes=VMEM_LIMIT),
        name="mla_prep",
    )(*rows, *consts)


def _mla_attn_kernel(qt_ref, k_ref, vt_ref, o_ref, st_a, st_b, m_ref, l_ref, acc_ref, *, tile):
    qi = pl.program_id(2)
    key = lax.broadcasted_iota(jnp.int32, (tile, tile), 0)
    qry = lax.broadcasted_iota(jnp.int32, (tile, tile), 1)
    head_lanes = [slice(a * LANES, (a + 1) * LANES) for a in range(2)]
    qts = [qt_ref[0, lanes, :] for lanes in head_lanes]

    def scores_into(buf, j):
        start = pl.multiple_of(j * tile, tile)
        for a, lanes in enumerate(head_lanes):
            buf[a] = jnp.dot(k_ref[0, pl.ds(start, tile), lanes], qts[a], preferred_element_type=F32)

    def consume(buf, j, masked):
        start = pl.multiple_of(j * tile, tile)
        vtj = vt_ref[0, :, pl.ds(start, tile)]
        for a in range(2):
            st = buf[a]
            if masked:
                st = jnp.where(key <= qry, st, -1e30)
            m = m_ref[a]
            m_new = jnp.maximum(m, jnp.max(st, axis=0, keepdims=True))
            alpha = jnp.exp(m - m_new)
            p = jnp.exp(st - m_new)
            m_ref[a] = m_new
            l_ref[a] = alpha * l_ref[a] + jnp.sum(p, axis=0, keepdims=True)
            pv = jnp.dot(vtj, p.astype(BF16), preferred_element_type=F32)
            acc_ref[a] = alpha * acc_ref[a] + pv

    m_ref[...] = jnp.full(m_ref.shape, -1e30, F32)
    l_ref[...] = jnp.zeros(l_ref.shape, F32)
    acc_ref[...] = jnp.zeros(acc_ref.shape, F32)
    scores_into(st_a, 0)

    def two_tiles(i, carry):
        j = 2 * i
        scores_into(st_b, j + 1)
        consume(st_a, j, False)
        scores_into(st_a, j + 2)
        consume(st_b, j + 1, False)
        return carry

    lax.fori_loop(0, qi // 2, two_tiles, 0)

    @pl.when(qi % 2 == 0)
    def _():
        consume(st_a, qi, True)

    @pl.when(qi % 2 == 1)
    def _():
        scores_into(st_b, qi)
        consume(st_a, qi - 1, False)
        consume(st_b, qi, True)

    dv_row = lax.broadcasted_iota(jnp.int32, acc_ref.shape[1:], 0)
    out_t = jnp.where(dv_row < MLA_V, acc_ref[0] / l_ref[0], acc_ref[1] / l_ref[1])
    o_ref[0] = out_t.T.astype(o_ref.dtype)


def _mla_attn(q, k, v, tile=512):
    b, s, _ = q.shape
    pair = 2 * LANES
    qt, vt = q.transpose(0, 2, 1), v.transpose(0, 2, 1)
    return pl.pallas_call(
        functools.partial(_mla_attn_kernel, tile=tile),
        grid=(b, MLA_HEADS // 2, s // tile),
        in_specs=[pl.BlockSpec((1, pair, tile), lambda bi, hp, qi: (bi, hp, qi)),
                  pl.BlockSpec((1, s, pair), lambda bi, hp, qi: (bi, 0, hp)),
                  pl.BlockSpec((1, 2 * MLA_V, s), lambda bi, hp, qi: (bi, hp, 0))],
        out_specs=pl.BlockSpec((1, tile, 2 * MLA_V), lambda bi, hp, qi: (bi, qi, hp)),
        out_shape=jax.ShapeDtypeStruct((b, s, MLA_HEADS * MLA_V), BF16),
        scratch_shapes=[pltpu.VMEM((2, tile, tile), F32), pltpu.VMEM((2, tile, tile), F32),
                        pltpu.VMEM((2, 1, tile), F32), pltpu.VMEM((2, 1, tile), F32),
                        pltpu.VMEM((2, 2 * MLA_V, tile), F32)],
        compiler_params=pltpu.CompilerParams(dimension_semantics=("parallel", "parallel", "arbitrary"),
                                             vmem_limit_bytes=VMEM_LIMIT),
        name="mla_attn",
    )(qt, k, vt)


def _mla_branch(c_q, c_kv, k_r, positions, q_norm_g, w_uq, kv_norm_g, w_ukv, qn_g, kn_g, b, s):
    cosp, sinp = _rope_patterns(positions, MLA_ROPE, MLA_NOPE, LANES)
    q, k, v = _mla_prep(c_q, c_kv, k_r, cosp, sinp, q_norm_g, w_uq, kv_norm_g, w_ukv, qn_g, kn_g)
    o = _mla_attn(q.reshape(b, s, -1), k.reshape(b, s, -1), v.reshape(b, s, -1))
    return o.reshape(b * s, -1)


def _retention_kernel(q_ref, k_ref, v_ref, g_ref, cos_ref, sin_ref, decay_ref, xi_ref, zeta_ref,
                      cd_ref, gn_ref, o_ref):
    L = RET_CHUNK
    n_chunks = q_ref.shape[1] // L
    lane = lax.broadcasted_iota(jnp.int32, (L, LANES), 1)

    def chunk_body(c, states):
        rows = pl.ds(pl.multiple_of(c * L, L), L)
        cosp, sinp = cos_ref[0, rows, :], sin_ref[0, rows, :]
        new_states = []
        for hp in range(RET_HEADS // 2):
            pair = slice(hp * LANES, (hp + 1) * LANES)
            qc = _rotate_half(q_ref[0, rows, pair], cosp, sinp, RET_DK, 0, RET_DK)
            kc = _rotate_half(k_ref[0, rows, pair], cosp, sinp, RET_DK, 0, RET_DK) * (RET_DK ** -0.5)
            qx = qc * xi_ref[hp]
            kz = kc * zeta_ref[hp]
            for a in range(2):
                hd = 2 * hp + a
                cols = slice(hd * RET_DV, (hd + 1) * RET_DV)
                mine = (lane >= a * RET_DK) & (lane < (a + 1) * RET_DK)
                qa = jnp.where(mine, qc, 0.0).astype(BF16)
                ka = jnp.where(mine, kc, 0.0).astype(BF16)
                va = v_ref[0, rows, cols]
                inner = lax.dot_general(qa, ka, (((1,), (1,)), ((), ())),
                                        preferred_element_type=F32) * decay_ref[hd]
                cross = jnp.dot(jnp.where(mine, qx, 0.0).astype(BF16), states[hd].astype(BF16),
                                preferred_element_type=F32)
                o = jnp.dot(inner.astype(BF16), va, preferred_element_type=F32) + cross
                kzt = jnp.where(mine, kz, 0.0).T.astype(BF16)
                incr = jnp.dot(kzt, va, preferred_element_type=F32)
                new_states.append(cd_ref[hd, 0:1, :] * states[hd] + incr)
                o = _norm(o, gn_ref[hd])
                gate = g_ref[0, rows, cols].astype(F32)
                o_ref[0, rows, cols] = (o * gate * jax.nn.sigmoid(gate)).astype(o_ref.dtype)
        return tuple(new_states)

    zero = jnp.zeros((LANES, RET_DV), F32)
    lax.fori_loop(0, n_chunks, chunk_body, (zero,) * RET_HEADS)


def _retention_branch(r_q, r_k, r_v, r_g, positions, gn_g):
    b, s, _ = r_q.shape
    L = RET_CHUNK
    cosp, sinp = _rope_patterns(positions, RET_DK, 0, RET_DK)
    cosp, sinp = cosp.reshape(b, s, LANES), sinp.reshape(b, s, LANES)
    log_gamma = jnp.log(1.0 - 2.0 ** (-5.0 - jnp.arange(RET_HEADS, dtype=F32)))
    idx = jnp.arange(L, dtype=F32)
    diff = idx[:, None] - idx[None, :]
    decay = jnp.where(diff >= 0, jnp.exp(log_gamma[:, None, None] * jnp.maximum(diff, 0.0)), 0.0)
    xi = jnp.exp(log_gamma[:, None] * (idx + 1.0))
    zeta = jnp.exp(log_gamma[:, None] * (L - 1.0 - idx))
    per_pair = lambda a: jnp.repeat(a.reshape(RET_HEADS // 2, 2, L), RET_DK, axis=1).transpose(0, 2, 1)
    cd = jnp.broadcast_to(jnp.exp(log_gamma * L)[:, None, None], (RET_HEADS, 8, RET_DV))
    gn = gn_g.reshape(RET_HEADS, 1, RET_DV)
    consts = [decay, per_pair(xi), per_pair(zeta), cd, gn]
    seqs = [r_q, r_k, r_v, r_g, cosp, sinp]
    return pl.pallas_call(
        _retention_kernel,
        grid=(b,),
        in_specs=[pl.BlockSpec((1, s, a.shape[2]), lambda bi: (bi, 0, 0)) for a in seqs]
        + _const_specs(consts),
        out_specs=pl.BlockSpec((1, s, RET_HEADS * RET_DV), lambda bi: (bi, 0, 0)),
        out_shape=jax.ShapeDtypeStruct((b, s, RET_HEADS * RET_DV), BF16),
        compiler_params=pltpu.CompilerParams(dimension_semantics=("parallel",),
                                             vmem_limit_bytes=PEER_VMEM_LIMIT),
        name="retention",
    )(*seqs, *consts)


def _mem_kv_kernel(mem_ref, g_ref, w_ref, kn_ref, k_ref, v_ref):
    kv = jnp.dot(_norm(mem_ref[0], g_ref[...]).astype(BF16), w_ref[...], preferred_element_type=F32)
    width = X_HEADS * X_DH
    for hd in range(X_HEADS):
        sl = slice(hd * X_DH, (hd + 1) * X_DH)
        k_ref[0, :, sl] = _norm(kv[:, sl], kn_ref[...]).astype(k_ref.dtype)
    v_ref[0] = kv[:, width:].astype(v_ref.dtype)


def _mem_kv(mem, mem_g, w_mem_kv, kn_g):
    b, m, d = mem.shape
    width = X_HEADS * X_DH
    w4 = w_mem_kv.reshape(d, X_HEADS, 2, X_DH)
    w = jnp.concatenate([w4[:, :, 0].reshape(d, width), w4[:, :, 1].reshape(d, width)], axis=1).astype(BF16)
    consts = [mem_g.reshape(1, d), w, kn_g.reshape(1, X_DH)]
    blk = pl.BlockSpec((1, m, width), lambda i: (i, 0, 0))
    return pl.pallas_call(
        _mem_kv_kernel,
        grid=(b,),
        in_specs=[pl.BlockSpec((1, m, d), lambda i: (i, 0, 0))] + _const_specs(consts),
        out_specs=[blk, blk],
        out_shape=[jax.ShapeDtypeStruct((b, m, width), BF16)] * 2,
        compiler_params=pltpu.CompilerParams(dimension_semantics=("parallel",),
                                             vmem_limit_bytes=VMEM_LIMIT),
        name="mem_kv",
    )(mem, *consts)


def _mem_attn_kernel(q_ref, k_ref, v_ref, qn_ref, o_ref):
    for hd in range(X_HEADS):
        sl = slice(hd * X_DH, (hd + 1) * X_DH)
        q = (_norm(q_ref[0, :, sl].astype(F32), qn_ref[...]) * (X_DH ** -0.5)).astype(BF16)
        s = lax.dot_general(q, k_ref[0, :, sl], (((1,), (1,)), ((), ())), preferred_element_type=F32)
        p = jnp.exp(s - jnp.max(s, axis=-1, keepdims=True))
        o = jnp.dot(p.astype(BF16), v_ref[0, :, sl], preferred_element_type=F32)
        o_ref[0, :, sl] = (o / jnp.sum(p, axis=-1, keepdims=True)).astype(o_ref.dtype)


def _mem_attn(x_q, k, v, qn_g, tm=512):
    b, s, width = x_q.shape
    m = k.shape[1]
    kv_spec = pl.BlockSpec((1, m, width), lambda bi, i: (bi, 0, 0))
    return pl.pallas_call(
        _mem_attn_kernel,
        grid=(b, s // tm),
        in_specs=[pl.BlockSpec((1, tm, width), lambda bi, i: (bi, i, 0)), kv_spec, kv_spec,
                  pl.BlockSpec((1, X_DH), lambda bi, i: (0, 0))],
        out_specs=pl.BlockSpec((1, tm, width), lambda bi, i: (bi, i, 0)),
        out_shape=jax.ShapeDtypeStruct((b, s, width), BF16),
        compiler_params=pltpu.CompilerParams(dimension_semantics=("parallel", "parallel"),
                                             vmem_limit_bytes=VMEM_LIMIT),
        name="mem_attn",
    )(x_q, k, v, qn_g.reshape(1, X_DH))


def _merge_kernel(x_ref, gates_ref, oa_ref, ob_ref, oc_ref, wa_ref, wb_ref, wc_ref, wout_ref, h_ref):
    merged = None
    for i, (o_ref, w_ref) in enumerate(((oa_ref, wa_ref), (ob_ref, wb_ref), (oc_ref, wc_ref))):
        y = jnp.dot(o_ref[...], w_ref[...], preferred_element_type=F32)
        gate = jax.nn.sigmoid(gates_ref[:, i * D_MODEL:(i + 1) * D_MODEL].astype(F32))
        merged = gate * y if merged is None else merged + gate * y
    h_ref[...] = x_ref[...] + jnp.dot(merged.astype(BF16), wout_ref[...], preferred_element_type=F32)


def _merge(x2d, gates, o_mla, o_ret, o_mem, w_o_mla, w_o_ret, w_o_cross, w_out, tm=512):
    t, d = x2d.shape
    rows = [x2d, gates, o_mla, o_ret, o_mem]
    consts = [w.astype(BF16) for w in (w_o_mla, w_o_ret, w_o_cross, w_out)]
    return pl.pallas_call(
        _merge_kernel,
        grid=(t // tm,),
        in_specs=_row_specs(rows, tm) + _const_specs(consts),
        out_specs=pl.BlockSpec((tm, d), lambda i: (i, 0)),
        out_shape=jax.ShapeDtypeStruct((t, d), F32),
        compiler_params=pltpu.CompilerParams(dimension_semantics=("parallel",),
                                             vmem_limit_bytes=VMEM_LIMIT),
        name="merge",
    )(*rows, *consts)


PEER_SLOTS = PEER_HEADS * PEER_TOPK
PAIR_CANDS = [(i, j) for i in range(PEER_TOPK) for j in range(PEER_TOPK)
              if (i + 1) * (j + 1) <= PEER_TOPK]
N_CANDS = len(PAIR_CANDS)
NEG_INF = float("-inf")


def _argmax_tree(nodes):
    while len(nodes) > 1:
        nxt = []
        for i in range(0, len(nodes) - 1, 2):
            a, b = nodes[i], nodes[i + 1]
            take_b = b[0] > a[0]
            nxt.append(tuple(jnp.where(take_b, y, x) for x, y in zip(a, b)))
        if len(nodes) % 2:
            nxt.append(nodes[-1])
        nodes = nxt
    return nodes[0]


def _top_rounds(rows, n_rounds):
    vals, idxs = [], []
    for _ in range(n_rounds):
        m, idx = _argmax_tree([(r, n) for n, r in enumerate(rows)])
        rows = [jnp.where(idx == n, NEG_INF, r) for n, r in enumerate(rows)]
        vals.append(m)
        idxs.append(idx)
    return vals, idxs


def _peer_select_kernel(h_ref, g_ref, wq_ref, keys_ref, xn_ref, exp_ref, gate_ref, q_ref):
    xn = _norm(h_ref[...], g_ref[...])
    xn_ref[...] = xn
    xnb = xn.astype(BF16)
    q_ref[...] = jnp.dot(xnb, wq_ref[...], preferred_element_type=F32).astype(BF16)
    tb = xnb.shape[0]
    tok = (tb // LANES, LANES)

    def head_body(hd, carry):
        sv, si = [], []
        for c in range(2):
            col = pl.multiple_of(hd * PEER_DQ + c * PEER_DHALF, PEER_DHALF)
            qh = q_ref[:, pl.ds(col, PEER_DHALF)]
            st = lax.dot_general(keys_ref[hd, c], qh, (((1,), (1,)), ((), ())),
                                 preferred_element_type=F32)
            st = st.reshape((PEER_N_KEYS,) + tok)
            v, i = _top_rounds([st[n] for n in range(PEER_N_KEYS)], PEER_TOPK)
            sv.append(v)
            si.append(i)
        cands = [sv[0][i] + sv[1][j] for i, j in PAIR_CANDS]
        cand_e = [si[0][i] * PEER_N_KEYS + si[1][j] for i, j in PAIR_CANDS]
        best, experts = [], []
        for _ in range(PEER_TOPK):
            m, pos, e = _argmax_tree([(c, p, ce) for p, (c, ce) in enumerate(zip(cands, cand_e))])
            cands = [jnp.where(pos == p, NEG_INF, c) for p, c in enumerate(cands)]
            best.append(m)
            experts.append(e)
        p = [jnp.exp(b - best[0]) for b in best]
        denom = p[0]
        for pk in p[1:]:
            denom = denom + pk
        row0 = pl.multiple_of(hd * PEER_TOPK, PEER_TOPK)
        exp_ref[pl.ds(row0, PEER_TOPK)] = jnp.stack(experts, axis=0)
        gate_ref[pl.ds(row0, PEER_TOPK)] = jnp.stack([pk / denom for pk in p], axis=0)
        return carry

    lax.fori_loop(0, PEER_HEADS, head_body, 0)


def _peer_select(h2d, g, w_q, sub_keys, tb=1024):
    t, d = h2d.shape
    consts = [g.reshape(1, d), w_q.astype(BF16), sub_keys.astype(BF16)]
    slot_spec = pl.BlockSpec((PEER_SLOTS, tb // LANES, LANES), lambda i: (0, i, 0))
    xn, experts, gates = pl.pallas_call(
        _peer_select_kernel,
        grid=(t // tb,),
        in_specs=_row_specs([h2d], tb) + _const_specs(consts),
        out_specs=[pl.BlockSpec((tb, d), lambda i: (i, 0)), slot_spec, slot_spec],
        out_shape=[
            jax.ShapeDtypeStruct((t, d), F32),
            jax.ShapeDtypeStruct((PEER_SLOTS, t // LANES, LANES), jnp.int32),
            jax.ShapeDtypeStruct((PEER_SLOTS, t // LANES, LANES), F32),
        ],
        scratch_shapes=[pltpu.VMEM((tb, PEER_HEADS * PEER_DQ), BF16)],
        compiler_params=pltpu.CompilerParams(dimension_semantics=("parallel",),
                                             vmem_limit_bytes=VMEM_LIMIT),
        name="peer_select",
    )(h2d, *consts)
    return xn, experts.reshape(PEER_SLOTS, t), gates.reshape(PEER_SLOTS, t)


ROW_TILE = D_MODEL // LANES
PAIR_ROWS = 2 * ROW_TILE


def _pack_words_kernel(lo_ref, hi_ref, o_ref):
    def rounded_bits(x):
        return pltpu.bitcast(x.astype(BF16).astype(F32), jnp.uint32)

    o_ref[...] = (rounded_bits(lo_ref[...]) >> 16) | (rounded_bits(hi_ref[...]) & jnp.uint32(0xFFFF0000))


def _pack_words(tab, block, lo_map, hi_map, out_rows, rows_per_step):
    return pl.pallas_call(
        _pack_words_kernel,
        grid=(out_rows // rows_per_step,),
        in_specs=[pl.BlockSpec((rows_per_step, block), lo_map), pl.BlockSpec((rows_per_step, block), hi_map)],
        out_specs=pl.BlockSpec((rows_per_step, block), lambda i: (i, 0)),
        out_shape=jax.ShapeDtypeStruct((out_rows, block), jnp.uint32),
        compiler_params=pltpu.CompilerParams(dimension_semantics=("parallel",),
                                             vmem_limit_bytes=VMEM_LIMIT),
        name="pack_table",
    )(tab, tab)


def _pack_expert_table(tab):
    n, d = tab.shape
    rows_per_step = min(512, n // 2)
    half_blocks = n // 2 // rows_per_step
    words = _pack_words(tab, d, lambda i: (i, 0), lambda i: (i + half_blocks, 0), n // 2, rows_per_step)
    return words.reshape(n // 2 * ROW_TILE, LANES)


TOKENS_PER_ITER = 32


def _gather_tiles(idx_ref, tab_ref, g_ref, t):
    for k in range(PEER_SLOTS):
        start = pl.multiple_of(idx_ref[t, k], ROW_TILE)
        g_ref[k * ROW_TILE:(k + 1) * ROW_TILE, :] = tab_ref[pl.ds(start, ROW_TILE), :]


HALF_TILE = ROW_TILE // 2


def _pack_expert_halves(tab):
    n, d = tab.shape
    words = _pack_words(tab, d // 2, lambda i: (i, 0), lambda i: (i, 1), n, min(512, n))
    return words.reshape(n * HALF_TILE, LANES)


def _peer_u_kernel(idx_ref, x_ref, gate_ref, fold_ref, tab_ref, w_ref, sel_ref, *g_refs):
    tb = x_ref.shape[0]
    shape = (ROW_TILE, PEER_SLOTS * ROW_TILE)
    r = lax.broadcasted_iota(jnp.int32, shape, 1) & (ROW_TILE - 1)
    chunk = lax.broadcasted_iota(jnp.int32, shape, 0)
    keep = ((r >> 1) + HALF_TILE * (r & 1)) == chunk

    def token_group(i, carry):
        for j, g_ref in enumerate(g_refs):
            t = i * TOKENS_PER_ITER + j
            halves = [tab_ref[pl.ds(pl.multiple_of(idx_ref[t, k], HALF_TILE), HALF_TILE), :]
                      for k in range(PEER_SLOTS)]
            for k in range(0, PEER_SLOTS, 2):
                g_ref[k * HALF_TILE:(k + 2) * HALF_TILE, :] = jnp.concatenate(halves[k:k + 2], axis=0)
            x8 = x_ref[pl.ds(t, 1), :].reshape(ROW_TILE, LANES)
            x16 = jnp.concatenate([x8, jnp.zeros_like(x8)], axis=0).astype(BF16)
            out = lax.dot_general(x16, pltpu.bitcast(g_ref[...], BF16), (((1,), (1,)), ((), ())),
                                  preferred_element_type=F32)
            sel_ref[pl.ds(t, 1), :] = jnp.sum(jnp.where(keep, out[:ROW_TILE], 0.0), axis=0, keepdims=True)
        return carry

    lax.fori_loop(0, tb // TOKENS_PER_ITER, token_group, 0)
    sel = sel_ref[...]
    sel_hi = sel.astype(BF16)
    sel_lo = (sel - sel_hi.astype(F32)).astype(BF16)
    a = (jnp.dot(sel_hi, fold_ref[...], preferred_element_type=F32)
         + jnp.dot(sel_lo, fold_ref[...], preferred_element_type=F32))
    w_ref[...] = gate_ref[...] * jax.nn.gelu(a)


def _peer_u(idx, xn, gate, tab, tb):
    t = idx.shape[0]
    fold = jnp.repeat(jnp.eye(PEER_SLOTS, dtype=BF16), ROW_TILE, axis=0)
    g_shape = pltpu.VMEM((PEER_SLOTS * HALF_TILE, LANES), jnp.uint32)
    return pl.pallas_call(
        _peer_u_kernel,
        grid=(t // tb,),
        in_specs=[
            pl.BlockSpec((tb, PEER_SLOTS), lambda i: (i, 0), memory_space=pltpu.SMEM),
            pl.BlockSpec((tb, D_MODEL), lambda i: (i, 0)),
            pl.BlockSpec((tb, PEER_SLOTS), lambda i: (i, 0)),
            pl.BlockSpec(fold.shape, lambda i: (0, 0), pipeline_mode=pl.Buffered(1)),
            pl.BlockSpec(tab.shape, lambda i: (0, 0), pipeline_mode=pl.Buffered(1)),
        ],
        out_specs=pl.BlockSpec((tb, PEER_SLOTS), lambda i: (i, 0)),
        out_shape=jax.ShapeDtypeStruct((t, PEER_SLOTS), F32),
        scratch_shapes=[pltpu.VMEM((tb, PEER_SLOTS * ROW_TILE), F32)] + [g_shape] * TOKENS_PER_ITER,
        compiler_params=pltpu.CompilerParams(dimension_semantics=("arbitrary",),
                                             vmem_limit_bytes=PEER_VMEM_LIMIT),
        name="peer_u",
    )(idx, xn, gate, fold, tab)


def _index_halves(idx_hbm, idx_a, idx_b, sem, process):
    step, n_steps = pl.program_id(0), pl.num_programs(0)
    tb = 2 * TOKENS_PER_ITER

    def fetch(s, half, buf):
        rows = pl.ds(pl.multiple_of(s * tb + half * TOKENS_PER_ITER, TOKENS_PER_ITER), TOKENS_PER_ITER)
        return pltpu.make_async_copy(idx_hbm.at[rows], buf, sem.at[half])

    @pl.when(step == 0)
    def _():
        fetch(0, 0, idx_a).start()
        fetch(0, 1, idx_b).start()

    for half, buf in enumerate((idx_a, idx_b)):
        fetch(step, half, buf).wait()
        process(buf, half * TOKENS_PER_ITER)

        @pl.when(step + 1 < n_steps)
        def _():
            fetch(step + 1, half, buf).start()


def _peer_v_kernel(idx_hbm, hi_ref, w_ref, h_ref, route_ref, tab_ref, o_ref, wx_ref, idx_a, idx_b, sem,
                   *g_refs):
    w = w_ref[...]
    w_hi = jnp.where(hi_ref[...] == 1, w, 0.0)
    w2 = jnp.concatenate([w - w_hi, w_hi], axis=1).astype(BF16)
    wx_ref[...] = jnp.dot(w2, route_ref[...], preferred_element_type=F32)
    shape = (PAIR_ROWS, PEER_SLOTS * PAIR_ROWS)
    row_in_tile = lax.broadcasted_iota(jnp.int32, shape, 1) & (PAIR_ROWS - 1)
    chunk = lax.broadcasted_iota(jnp.int32, shape, 0)
    keep = (row_in_tile >> 1) == chunk

    def process(idx_ref, first):
        for j, g_ref in enumerate(g_refs):
            t = first + j
            _gather_tiles(idx_ref, tab_ref, g_ref, j)
            lhs = jnp.where(keep, wx_ref[t:t + 1, :], 0.0).astype(BF16)
            o = jnp.dot(lhs, pltpu.bitcast(g_ref[...], BF16), preferred_element_type=F32)
            o_ref[t:t + 1, :] = h_ref[t:t + 1, :] + o[:ROW_TILE].reshape(1, D_MODEL)

    _index_halves(idx_hbm, idx_a, idx_b, sem, process)


def _peer_v(idx, hi, w, h2d, tab):
    t = idx.shape[0]
    tb = 2 * TOKENS_PER_ITER
    eye = jnp.eye(PEER_SLOTS, dtype=BF16)
    parity = (jnp.arange(PAIR_ROWS) & 1).astype(BF16)
    route = jnp.concatenate([jnp.kron(eye, (1 - parity)[None, :]), jnp.kron(eye, parity[None, :])], axis=0)
    idx_buf = pltpu.SMEM((TOKENS_PER_ITER, PEER_SLOTS), jnp.int32)
    return pl.pallas_call(
        _peer_v_kernel,
        grid=(t // tb,),
        in_specs=[
            pl.BlockSpec(memory_space=pl.ANY),
            pl.BlockSpec((tb, PEER_SLOTS), lambda i: (i, 0)),
            pl.BlockSpec((tb, PEER_SLOTS), lambda i: (i, 0)),
            pl.BlockSpec((tb, D_MODEL), lambda i: (i, 0)),
            pl.BlockSpec(route.shape, lambda i: (0, 0), pipeline_mode=pl.Buffered(1)),
            pl.BlockSpec(tab.shape, lambda i: (0, 0), pipeline_mode=pl.Buffered(1)),
        ],
        out_specs=pl.BlockSpec((tb, D_MODEL), lambda i: (i, 0)),
        out_shape=jax.ShapeDtypeStruct((t, D_MODEL), F32),
        scratch_shapes=[pltpu.VMEM((tb, PEER_SLOTS * PAIR_ROWS), F32), idx_buf, idx_buf,
                        pltpu.SemaphoreType.DMA((2,))]
        + [pltpu.VMEM((PEER_SLOTS * ROW_TILE, LANES), jnp.uint32)] * TOKENS_PER_ITER,
        compiler_params=pltpu.CompilerParams(dimension_semantics=("arbitrary",),
                                             vmem_limit_bytes=PEER_VMEM_LIMIT),
        name="peer_v",
    )(idx, hi, w, h2d, route, tab)


def _peer_experts(h2d, xn2d, experts, gate, u_tab, v_tab, tb_u=128):
    t, d = xn2d.shape
    half = u_tab.shape[0] // 2
    hi = (experts >= half).astype(jnp.int32)
    idx = (experts - hi * half) * ROW_TILE
    w = _peer_u(experts * HALF_TILE, xn2d, gate, _pack_expert_halves(u_tab), tb_u)
    return _peer_v(idx, hi, w, h2d, _pack_expert_table(v_tab))


def kernel(x, mem, positions, attn_norm_g, w_in, mla_q_norm_g, w_uq, mla_kv_norm_g, w_ukv,
           mla_qn_g, mla_kn_g, w_o_mla, ret_gn_g, w_o_ret, mem_norm_g, w_mem_kv, x_qn_g, x_kn_g,
           w_o_cross, w_out, ffn_norm_g, peer_w_q, peer_keys, peer_u, peer_v):
    B, S, D = x.shape
    T = B * S
    h2d = x.reshape(T, D)
    offs = [0]
    for sz in IN_SPLITS:
        offs.append(offs[-1] + sz)
    for layer in range(w_in.shape[0]):
        w_pieces = [w_in[layer][:, offs[i]:offs[i + 1]].astype(BF16) for i in range(len(IN_SPLITS))]
        out_dtypes = [F32, F32, F32, F32, F32, BF16, BF16, BF16, BF16]
        c_q, c_kv, k_r, r_q, r_k, r_v, r_g, x_q, gates = _in_proj(h2d, attn_norm_g[layer], w_pieces, out_dtypes)
        o_mla = _mla_branch(c_q, c_kv, k_r, positions, mla_q_norm_g[layer], w_uq[layer],
                            mla_kv_norm_g[layer], w_ukv[layer], mla_qn_g[layer], mla_kn_g[layer], B, S)
        seq = lambda a: a.reshape(B, S, -1)
        o_ret = _retention_branch(seq(r_q), seq(r_k), seq(r_v), seq(r_g), positions, ret_gn_g[layer])
        mk, mv = _mem_kv(mem, mem_norm_g[layer], w_mem_kv[layer], x_kn_g[layer])
        o_mem = _mem_attn(seq(x_q), mk, mv, x_qn_g[layer])
        h2d = _merge(h2d, gates, o_mla, o_ret.reshape(T, -1), o_mem.reshape(T, -1),
                     w_o_mla[layer], w_o_ret[layer], w_o_cross[layer], w_out[layer])
        xn2d, experts_t, gate_t = _peer_select(h2d, ffn_norm_g[layer], peer_w_q[layer], peer_keys[layer])
        h2d = _peer_experts(h2d, xn2d, experts_t.T, gate_t.T, peer_u[layer], peer_v[layer])
    return h2d.reshape(B, S, D)
```
